```python
import jax, jax.numpy as jnp
from jax import lax
import numpy as np

D_MODEL = 2048
BATCH = 4
SEQ = 2048
DEPTH = 4
DEC_BATCH = 8
DEC_SEQ = 4
PAST_LEN = 16384
PAGE_SIZE = 128

HEAD_DIM = 64
ATTN_DIM = D_MODEL // 2
N_HEADS = ATTN_DIM // HEAD_DIM
N_KV = N_HEADS // 4
GQA = N_HEADS // N_KV
KV_DIM = N_KV * HEAD_DIM
CONV_DIM = D_MODEL - ATTN_DIM
CONV_W = 3
CMP_LEN = 32
CMP_STRIDE = 16
CMP_HIDDEN = HEAD_DIM
SEL_BLOCK = 64
N_SEL = 16
WINDOW = 512
Q_BLOCK = 64
D_FF = 5632
N_EXPERTS = 8
TOP_K = 2
D_FF_EXPERT = D_FF // TOP_K
N_DENSE = (DEPTH + 1) // 2
N_MOE = DEPTH // 2
KV_OFF = ATTN_DIM
GATE_OFF = KV_OFF + 6 * KV_DIM
CONV_OFF = GATE_OFF + 3 * N_HEADS
IN_COLS = CONV_OFF + 3 * CONV_DIM
EPS = 1e-6
NEG_INF = -1e30
FORCE_BONUS = 1e4
ATTN_SCALE = HEAD_DIM ** -0.5

kernel_name = "nsa_shortconv_hybrid_step"


def _rmsnorm(x, g):
    x32 = x.astype(jnp.float32)
    y = x32 * lax.rsqrt(jnp.mean(x32 * x32, axis=-1, keepdims=True) + EPS)
    return (y * g.astype(jnp.float32)).astype(x.dtype)


def _masked_softmax(s, mask):
    s = jnp.where(mask, s.astype(jnp.float32), NEG_INF)
    m = jnp.max(s, axis=-1, keepdims=True)
    e = jnp.exp(s - m) * mask
    return e / jnp.maximum(jnp.sum(e, axis=-1, keepdims=True), 1e-30)


def _compress(rows, w1, b1, w2, pe):
    B, L, H, D = rows.shape
    r = CMP_LEN // CMP_STRIDE
    n_chunks = L // CMP_STRIDE
    n_cmp = n_chunks - r + 1
    chunks = rows[:, :n_chunks * CMP_STRIDE].reshape(B, n_chunks, CMP_STRIDE, H, D)
    blocks = jnp.concatenate([chunks[:, i:i + n_cmp] for i in range(r)], axis=2)
    blocks = blocks + pe[:, None, :]
    flat = blocks.transpose(0, 1, 3, 2, 4).reshape(B, n_cmp, H, CMP_LEN * D)
    return jax.nn.silu(flat @ w1 + b1) @ w2


def _nsa_attention(q, gate, kc, vc, ks, vs, kw_ext, vw_ext, q0):
    B, Tq, H, D = q.shape
    qb = Q_BLOCK if Tq % Q_BLOCK == 0 else Tq
    nqb = Tq // qb
    n_cmp = kc.shape[1]
    n_sel = ks.shape[1] // SEL_BLOCK
    k_top = min(N_SEL, n_sel)
    ks_blk = ks.reshape(B, n_sel, SEL_BLOCK, N_KV, D).transpose(0, 3, 1, 2, 4)
    vs_blk = vs.reshape(B, n_sel, SEL_BLOCK, N_KV, D).transpose(0, 3, 1, 2, 4)
    cmp_end = jnp.arange(n_cmp) * CMP_STRIDE + (CMP_LEN - 1)
    r = CMP_LEN // CMP_STRIDE
    sr = SEL_BLOCK // CMP_STRIDE
    agg_idx = np.arange(n_sel)[:, None] * sr + np.arange(-(r - 1), sr)[None, :]
    agg_valid = jnp.asarray((agg_idx >= 0) & (agg_idx < n_cmp))
    agg_idx = jnp.asarray(np.clip(agg_idx, 0, max(n_cmp - 1, 0)))
    sel_ids = jnp.arange(n_sel)
    b_ix = jnp.arange(B)[:, None, None, None]
    h_ix = jnp.arange(N_KV)[None, None, :, None]

    def block(args):
        q_blk, g_blk, bi = args
        qpos = q0 + bi * qb + jnp.arange(qb)
        qg = q_blk.reshape(B, qb, N_KV, GQA, D)
        s_c = jnp.einsum('bqhgd,bnhd->bqhgn', qg, kc).astype(jnp.float32) * ATTN_SCALE
        p_c = _masked_softmax(s_c, (cmp_end[None, :] <= qpos[:, None])[None, :, None, None, :])
        o_c = jnp.einsum('bqhgn,bnhd->bqhgd', p_c.astype(vc.dtype), vc)
        imp = jnp.sum(p_c, axis=3)
        imp_s = jnp.sum(jnp.take(imp, agg_idx, axis=-1) * agg_valid, axis=-1)
        cur = qpos // SEL_BLOCK
        forced = (sel_ids[None, :] == 0) | (sel_ids[None, :] == cur[:, None]) | (sel_ids[None, :] == cur[:, None] - 1)
        causal = sel_ids[None, :] * SEL_BLOCK <= qpos[:, None]
        score = jnp.where(causal[None, :, None, :], imp_s + jnp.where(forced, FORCE_BONUS, 0.0)[None, :, None, :], NEG_INF)
        top_s, top_i = lax.top_k(score, k_top)
        k_g = ks_blk[b_ix, h_ix, top_i].reshape(B, qb, N_KV, k_top * SEL_BLOCK, D)
        v_g = vs_blk[b_ix, h_ix, top_i].reshape(B, qb, N_KV, k_top * SEL_BLOCK, D)
        kpos = top_i[..., None] * SEL_BLOCK + jnp.arange(SEL_BLOCK)
        m_s = (top_s > NEG_INF / 2)[..., None] & (kpos <= qpos[None, :, None, None, None])
        m_s = m_s.reshape(B, qb, N_KV, 1, k_top * SEL_BLOCK)
        s_s = jnp.einsum('bqhgd,bqhmd->bqhgm', qg, k_g).astype(jnp.float32) * ATTN_SCALE
        p_s = _masked_softmax(s_s, m_s)
        o_s = jnp.einsum('bqhgm,bqhmd->bqhgd', p_s.astype(v_g.dtype), v_g)
        kw = lax.dynamic_slice_in_dim(kw_ext, bi * qb, WINDOW + qb, axis=1)
        vw = lax.dynamic_slice_in_dim(vw_ext, bi * qb, WINDOW + qb, axis=1)
        kpos_w = q0 - WINDOW + bi * qb + jnp.arange(WINDOW + qb)
        diff = qpos[:, None] - kpos_w[None, :]
        m_w = (kpos_w[None, :] >= 0) & (diff >= 0) & (diff < WINDOW)
        s_w = jnp.einsum('bqhgd,bkhd->bqhgk', qg, kw).astype(jnp.float32) * ATTN_SCALE
        p_w = _masked_softmax(s_w, m_w[None, :, None, None, :])
        o_w = jnp.einsum('bqhgk,bkhd->bqhgd', p_w.astype(vw.dtype), vw)
        g = g_blk.reshape(B, qb, N_KV, GQA, 3)
        o = g[..., 0:1] * o_c + g[..., 1:2] * o_s + g[..., 2:3] * o_w
        return o.reshape(B, qb, H * D).astype(q.dtype)

    q_blocks = q.reshape(B, nqb, qb, H, D).transpose(1, 0, 2, 3, 4)
    g_blocks = gate.reshape(B, nqb, qb, H, 3).transpose(1, 0, 2, 3, 4)
    out = lax.map(block, (q_blocks, g_blocks, jnp.arange(nqb)))
    return out.transpose(1, 0, 2, 3).reshape(B, Tq, H * D)


def _short_conv(u_ext, bg, conv_w, conv_b):
    T = bg.shape[1]
    y = conv_b
    for j in range(CONV_W):
        y = y + conv_w[j] * u_ext[:, j:j + T]
    return bg * y


def _mixer(h, prev_cmp, prev_sel, prev_win, prev_conv, buf_len, w_in, q_gain, k_gain,
           cmp_w1, cmp_b1, cmp_w2, cmp_pe, conv_w, conv_b, out_gain, w_out):
    B, T, _ = h.shape
    q0 = prev_cmp.shape[1]
    p = h @ w_in
    q = _rmsnorm(p[..., :ATTN_DIM].reshape(B, T, N_HEADS, HEAD_DIM), q_gain)
    kv = p[..., KV_OFF:GATE_OFF].reshape(B, T, 3, 2, N_KV, HEAD_DIM)
    gate = jax.nn.sigmoid(p[..., GATE_OFF:CONV_OFF].astype(jnp.float32)).reshape(B, T, N_HEADS, 3)
    hc, bg, cg = jnp.split(p[..., CONV_OFF:], 3, axis=-1)
    cmp_rows = kv[:, :, 0]
    sel_rows = jnp.stack([_rmsnorm(kv[:, :, 1, 0], k_gain[1]), kv[:, :, 1, 1]], axis=2)
    win_rows = jnp.stack([_rmsnorm(kv[:, :, 2, 0], k_gain[2]), kv[:, :, 2, 1]], axis=2)
    L = q0 + T
    L_pad = -(-L // SEL_BLOCK) * SEL_BLOCK
    pad = ((0, 0), (0, L_pad - L), (0, 0), (0, 0), (0, 0))
    cmp_all = jnp.pad(jnp.concatenate([prev_cmp, cmp_rows], axis=1), pad)
    sel_all = jnp.pad(jnp.concatenate([prev_sel, sel_rows], axis=1), pad)
    kc = _rmsnorm(_compress(cmp_all[:, :, 0], cmp_w1[0], cmp_b1[0], cmp_w2[0], cmp_pe[0]), k_gain[0])
    vc = _compress(cmp_all[:, :, 1], cmp_w1[1], cmp_b1[1], cmp_w2[1], cmp_pe[1])
    win_cat = jnp.concatenate([prev_win, win_rows], axis=1)
    win_ext = jnp.pad(win_cat, ((0, 0), (WINDOW - prev_win.shape[1], 0), (0, 0), (0, 0), (0, 0)))
    attn = _nsa_attention(q, gate, kc, vc, sel_all[:, :, 0], sel_all[:, :, 1],
                          win_ext[:, :, 0], win_ext[:, :, 1], q0)
    u_ext = jnp.concatenate([prev_conv, cg * hc], axis=1)
    conv = _short_conv(u_ext, bg, conv_w, conv_b)
    o = jnp.concatenate([_rmsnorm(attn, out_gain[:ATTN_DIM]), _rmsnorm(conv, out_gain[ATTN_DIM:])], axis=-1)
    return o @ w_out, cmp_rows, sel_rows, win_cat[:, -buf_len:], u_ext[:, -(CONV_W - 1):]


def _modulation(c, w_ada, b_ada):
    return jnp.split(jax.nn.silu(c) @ w_ada + b_ada, 6, axis=-1)


def _modulate(x, g, shift, scale):
    return _rmsnorm(x, g) * (1 + scale[:, None]) + shift[:, None]


def _dense_ffn(h, w1, w3, w2):
    return (jax.nn.silu(h @ w1) * (h @ w3)) @ w2


def _moe_ffn(h, router_w, w1, w3, w2):
    logits = (h @ router_w).astype(jnp.float32)
    top_v, top_i = lax.top_k(logits, TOP_K)
    wts = jax.nn.softmax(top_v, axis=-1)
    combine = jnp.sum(jax.nn.one_hot(top_i, N_EXPERTS, dtype=jnp.float32) * wts[..., None], axis=-2).astype(h.dtype)
    out = jnp.zeros_like(h)
    for e in range(N_EXPERTS):
        he = jax.nn.silu(h @ w1[e]) * (h @ w3[e])
        out = out + combine[..., e:e + 1] * (he @ w2[e])
    return out


def setup_inputs(seed: int = 0) -> dict:
    key = jax.random.key(seed)
    ks = iter(jax.random.split(key, 40))

    def nrm(shape, s):
        return jax.random.normal(next(ks), shape, jnp.float32) * s

    n_pages = PAST_LEN // PAGE_SIZE
    n_used = DEC_BATCH * n_pages
    n_pool = n_used + max(1, n_used // 4)
    win_buf = min(WINDOW, PAST_LEN)
    page_table = jax.random.permutation(next(ks), n_pool)[:n_used].reshape(DEC_BATCH, n_pages).astype(jnp.int32)
    return {
        "x_prompt": nrm((BATCH, SEQ, D_MODEL), 1.0),
        "x_sample": nrm((DEC_BATCH, DEC_SEQ, D_MODEL), 1.0),
        "c_prompt": nrm((BATCH, D_MODEL), 1.0),
        "c_sample": nrm((DEC_BATCH, D_MODEL), 1.0),
        "cache_cmp": nrm((DEPTH, n_pool, PAGE_SIZE, 2, N_KV, HEAD_DIM), 1.0),
        "cache_sel": nrm((DEPTH, n_pool, PAGE_SIZE, 2, N_KV, HEAD_DIM), 1.0),
        "cache_win": nrm((DEPTH, DEC_BATCH, win_buf, 2, N_KV, HEAD_DIM), 1.0),
        "state_conv": nrm((DEPTH, DEC_BATCH, CONV_W - 1, CONV_DIM), 0.5),
        "page_table": page_table,
        "w_ada": nrm((DEPTH, D_MODEL, 6 * D_MODEL), 0.5 * D_MODEL ** -0.5),
        "b_ada": nrm((DEPTH, 6 * D_MODEL), 0.02),
        "norm1": 1.0 + nrm((DEPTH, D_MODEL), 0.05),
        "norm2": 1.0 + nrm((DEPTH, D_MODEL), 0.05),
        "w_in": nrm((DEPTH, D_MODEL, IN_COLS), D_MODEL ** -0.5),
        "q_gain": 1.0 + nrm((DEPTH, HEAD_DIM), 0.05),
        "k_gain": 1.0 + nrm((DEPTH, 3, HEAD_DIM), 0.05),
        "cmp_w1": nrm((DEPTH, 2, CMP_LEN * HEAD_DIM, CMP_HIDDEN), (CMP_LEN * HEAD_DIM) ** -0.5),
        "cmp_b1": nrm((DEPTH, 2, CMP_HIDDEN), 0.02),
        "cmp_w2": nrm((DEPTH, 2, CMP_HIDDEN, HEAD_DIM), CMP_HIDDEN ** -0.5),
        "cmp_pe": nrm((DEPTH, 2, CMP_LEN, HEAD_DIM), 0.1),
        "conv_w": nrm((DEPTH, CONV_W, CONV_DIM), 0.5),
        "conv_b": nrm((DEPTH, CONV_DIM), 0.02),
        "out_gain": 1.0 + nrm((DEPTH, D_MODEL), 0.05),
        "w_out": nrm((DEPTH, D_MODEL, D_MODEL), D_MODEL ** -0.5),
        "ffn_w1": nrm((N_DENSE, D_MODEL, D_FF), D_MODEL ** -0.5),
        "ffn_w3": nrm((N_DENSE, D_MODEL, D_FF), D_MODEL ** -0.5),
        "ffn_w2": nrm((N_DENSE, D_FF, D_MODEL), D_FF ** -0.5),
        "router_w": nrm((N_MOE, D_MODEL, N_EXPERTS), D_MODEL ** -0.5),
        "moe_w1": nrm((N_MOE, N_EXPERTS, D_MODEL, D_FF_EXPERT), D_MODEL ** -0.5),
        "moe_w3": nrm((N_MOE, N_EXPERTS, D_MODEL, D_FF_EXPERT), D_MODEL ** -0.5),
        "moe_w2": nrm((N_MOE, N_EXPERTS, D_FF_EXPERT, D_MODEL), D_FF_EXPERT ** -0.5),
    }


def reference(x_prompt, x_sample, c_prompt, c_sample, cache_cmp, cache_sel, cache_win, state_conv, page_table,
              w_ada, b_ada, norm1, norm2, w_in, q_gain, k_gain, cmp_w1, cmp_b1, cmp_w2, cmp_pe,
              conv_w, conv_b, out_gain, w_out, ffn_w1, ffn_w3, ffn_w2, router_w, moe_w1, moe_w3, moe_w2):
    n_db, n_pages = page_table.shape
    past_len = n_pages * cache_cmp.shape[2]
    bp = x_prompt.shape[0]
    empty = jnp.zeros((bp, 0, 2, N_KV, HEAD_DIM), x_prompt.dtype)
    conv0 = jnp.zeros((bp, CONV_W - 1, CONV_DIM), x_prompt.dtype)
    prompt_buf = min(WINDOW, x_prompt.shape[1])
    sample_buf = cache_win.shape[2]
    yp, ys = x_prompt, x_sample
    cmp_ps, sel_ps, win_ps, conv_ps = [], [], [], []
    cmp_ss, sel_ss, win_ss, conv_ss = [], [], [], []
    for l in range(DEPTH):
        lw = (w_in[l], q_gain[l], k_gain[l], cmp_w1[l], cmp_b1[l], cmp_w2[l], cmp_pe[l],
              conv_w[l], conv_b[l], out_gain[l], w_out[l])
        mp = _modulation(c_prompt, w_ada[l], b_ada[l])
        ms = _modulation(c_sample, w_ada[l], b_ada[l])
        out_p, cmp_p, sel_p, win_p, conv_p = _mixer(_modulate(yp, norm1[l], mp[0], mp[1]),
                                                    empty, empty, empty, conv0, prompt_buf, *lw)
        past_cmp = cache_cmp[l][page_table].reshape(n_db, past_len, 2, N_KV, HEAD_DIM)
        past_sel = cache_sel[l][page_table].reshape(n_db, past_len, 2, N_KV, HEAD_DIM)
        out_s, cmp_s, sel_s, win_s, conv_s = _mixer(_modulate(ys, norm1[l], ms[0], ms[1]),
                                                    past_cmp, past_sel, cache_win[l], state_conv[l], sample_buf, *lw)
        yp = yp + mp[2][:, None] * out_p
        ys = ys + ms[2][:, None] * out_s
        hp = _modulate(yp, norm2[l], mp[3], mp[4])
        hs = _modulate(ys, norm2[l], ms[3], ms[4])
        if l % 2 == 0:
            fp = _dense_ffn(hp, ffn_w1[l // 2], ffn_w3[l // 2], ffn_w2[l // 2])
            fs = _dense_ffn(hs, ffn_w1[l // 2], ffn_w3[l // 2], ffn_w2[l // 2])
        else:
            fp = _moe_ffn(hp, router_w[l // 2], moe_w1[l // 2], moe_w3[l // 2], moe_w2[l // 2])
            fs = _moe_ffn(hs, router_w[l // 2], moe_w1[l // 2], moe_w3[l // 2], moe_w2[l // 2])
        yp = yp + mp[5][:, None] * fp
        ys = ys + ms[5][:, None] * fs
        cmp_ps.append(cmp_p); sel_ps.append(sel_p); win_ps.append(win_p); conv_ps.append(conv_p)
        cmp_ss.append(cmp_s); sel_ss.append(sel_s); win_ss.append(win_s); conv_ss.append(conv_s)
    return (yp, ys,
            jnp.stack(cmp_ps), jnp.stack(sel_ps), jnp.stack(win_ps), jnp.stack(conv_ps),
            jnp.stack(cmp_ss), jnp.stack(sel_ss), jnp.stack(win_ss), jnp.stack(conv_ss))
```

```python
import functools

import numpy as np
import jax
import jax.numpy as jnp
from jax import lax
from jax.experimental import pallas as pl
from jax.experimental.pallas import tpu as pltpu

F32 = jnp.float32
BF16 = jnp.bfloat16

D_MODEL = 2048
HEAD_DIM = 64
ATTN_DIM = 1024
N_HEADS = 16
N_KV = 4
GQA = 4
KV_DIM = 256
CONV_DIM = 1024
CMP_STRIDE = 16
SEL_BLOCK = 64
N_SEL = 16
WINDOW = 512
N_EXPERTS = 8
KV_OFF = ATTN_DIM
GATE_OFF = KV_OFF + 6 * KV_DIM
CONV_OFF = GATE_OFF + 3 * N_HEADS
IN_COLS = CONV_OFF + 3 * CONV_DIM
EPS = 1e-6
NEG_INF = -1e30
FORCE_BONUS = 1e4
ATTN_SCALE = HEAD_DIM ** -0.5
LANES = 128
VMEM_LIMIT = 56 * 1024 * 1024
PAGES_PER_STEP = 16
NT_DIMS = (((1,), (1,)), ((), ()))


def _cparams(sem):
    return pltpu.CompilerParams(dimension_semantics=sem, vmem_limit_bytes=VMEM_LIMIT)


def _dot(a, b):
    return jnp.dot(a, b, preferred_element_type=F32)


def _dot_nt(a, b):
    return lax.dot_general(a, b, NT_DIMS, preferred_element_type=F32)


def _silu(x):
    return x * jax.nn.sigmoid(x)


def _split3_dot(x, w_hi):
    x1 = x.astype(BF16)
    r1 = x - x1.astype(F32)
    x2 = r1.astype(BF16)
    x3 = (r1 - x2.astype(F32)).astype(BF16)
    return _dot(x1, w_hi) + _dot(x2, w_hi) + _dot(x3, w_hi)


def _headnorm_lanes(x, gain):
    lane = lax.broadcasted_iota(jnp.int32, (1, LANES), 1)
    lo = lane < HEAD_DIM
    outs = []
    for c in range(x.shape[-1] // LANES):
        xc = x[:, c * LANES:(c + 1) * LANES]
        x2 = xc * xc
        s_lo = jnp.sum(jnp.where(lo, x2, 0.0), axis=-1, keepdims=True)
        s_hi = jnp.sum(jnp.where(lo, 0.0, x2), axis=-1, keepdims=True)
        r = jnp.where(lo, lax.rsqrt(s_lo * (1.0 / HEAD_DIM) + EPS), lax.rsqrt(s_hi * (1.0 / HEAD_DIM) + EPS))
        outs.append(xc * r)
    return jnp.concatenate(outs, axis=-1) * gain


def _masked_softmax(s, mask):
    s = jnp.where(mask, s, NEG_INF)
    m = jnp.max(s, axis=-1, keepdims=True)
    e = jnp.exp(s - m) * mask.astype(F32)
    return e / jnp.maximum(jnp.sum(e, axis=-1, keepdims=True), 1e-30)


def _rms_modulate(x, g, shift, scale):
    r = lax.rsqrt(jnp.mean(x * x, axis=-1, keepdims=True) + EPS)
    return (x * r * g) * (1.0 + scale) + shift


def _mod_kernel(c_ref, w_ref, b_ref, o_ref):
    s = _silu(c_ref[...])
    o_ref[...] = _dot(s.astype(BF16), w_ref[...].astype(BF16)) + b_ref[...]


def _modulation(c_all, w_ada, b_ada):
    depth, d, cols = w_ada.shape
    rows = c_all.shape[0]
    tn = 1024
    return pl.pallas_call(
        _mod_kernel,
        grid=(depth, cols // tn),
        in_specs=[pl.BlockSpec((rows, d), lambda l, j: (0, 0)),
                  pl.BlockSpec((None, d, tn), lambda l, j: (l, 0, j)),
                  pl.BlockSpec((None, 1, tn), lambda l, j: (l, 0, j))],
        out_specs=pl.BlockSpec((None, rows, tn), lambda l, j: (l, 0, j)),
        out_shape=jax.ShapeDtypeStruct((depth, rows, cols), F32),
        compiler_params=_cparams(("arbitrary", "arbitrary")),
        name="adaln_modulation",
    )(c_all, w_ada, b_ada.reshape(depth, 1, cols))


Q_TILE = 512
KV_TILE = 512
CONV_TILE = 256
N_Q_STEPS = ATTN_DIM // Q_TILE
N_KV_STEPS = 3
N_CONV_STEPS = CONV_DIM // CONV_TILE
N_IN_STEPS = N_Q_STEPS + N_KV_STEPS + N_CONV_STEPS


def _inproj_kernel(y_ref, sh_ref, sc_ref, g1_ref, wq_ref, wkv_ref, wg_ref, wc_ref, qg_ref, kg_ref,
                   qn_ref, kvt_ref, gate_ref, u_ref, bg_ref, h_scr):
    j = pl.program_id(1)

    @pl.when(j == 0)
    def _():
        h = _rms_modulate(y_ref[...], g1_ref[...], sh_ref[...], sc_ref[...])
        h_scr[...] = h.astype(BF16)
        gate_ref[...] = jax.nn.sigmoid(_dot_nt(h_scr[...], wg_ref[...]))

    @pl.when(j < N_Q_STEPS)
    def _():
        p = _dot_nt(h_scr[...], wq_ref[...])
        qn_ref[...] = _headnorm_lanes(p, qg_ref[...]).astype(BF16)

    @pl.when((j >= N_Q_STEPS) & (j < N_Q_STEPS + N_KV_STEPS))
    def _():
        pt = _dot_nt(wkv_ref[...], h_scr[...])
        tok = pt.shape[-1]
        k3 = pt[:KV_DIM].reshape(N_KV, HEAD_DIM, tok)
        ms = jnp.mean(k3 * k3, axis=1, keepdims=True)
        kn = k3 * lax.rsqrt(ms + EPS) * kg_ref[...]
        k3 = jnp.where(j == N_Q_STEPS, k3, kn)
        kvt_ref[:KV_DIM, :] = k3.reshape(KV_DIM, tok)
        kvt_ref[KV_DIM:, :] = pt[KV_DIM:]

    @pl.when(j >= N_Q_STEPS + N_KV_STEPS)
    def _():
        p = _dot_nt(h_scr[...], wc_ref[...])
        u_ref[...] = p[:, 2 * CONV_TILE:] * p[:, :CONV_TILE]
        bg_ref[...] = p[:, CONV_TILE:2 * CONV_TILE]


def _inproj(y, shift, scale, g1, wq, wkv, wg, wc, qg, kg, *, tm, n_seq):
    n, d = y.shape
    t = n // n_seq
    tiles_per_seq = t // tm
    tiles_per_mod = (n // tm) // shift.shape[0]
    r = shift.shape[1]

    def clamp(j, lo, cnt):
        return jnp.clip(j - lo, 0, cnt - 1)

    q_lo, kv_lo, c_lo = 0, N_Q_STEPS, N_Q_STEPS + N_KV_STEPS
    in_specs = [
        pl.BlockSpec((tm, d), lambda i, j: (i, 0)),
        pl.BlockSpec((None, r, d), lambda i, j: (i // tiles_per_mod, 0, 0)),
        pl.BlockSpec((None, r, d), lambda i, j: (i // tiles_per_mod, 0, 0)),
        pl.BlockSpec((1, d), lambda i, j: (0, 0)),
        pl.BlockSpec((Q_TILE, d), lambda i, j: (clamp(j, q_lo, N_Q_STEPS), 0)),
        pl.BlockSpec((KV_TILE, d), lambda i, j: (clamp(j, kv_lo, N_KV_STEPS), 0)),
        pl.BlockSpec((LANES, d), lambda i, j: (0, 0)),
        pl.BlockSpec((3 * CONV_TILE, d), lambda i, j: (clamp(j, c_lo, N_CONV_STEPS), 0)),
        pl.BlockSpec((1, Q_TILE), lambda i, j: (0, 0)),
        pl.BlockSpec((None, N_KV, HEAD_DIM, 1), lambda i, j: (clamp(j, kv_lo, N_KV_STEPS), 0, 0, 0)),
    ]
    out_specs = [
        pl.BlockSpec((tm, Q_TILE), lambda i, j: (i, clamp(j, q_lo, N_Q_STEPS))),
        pl.BlockSpec((None, KV_TILE, tm),
                     lambda i, j: (i // tiles_per_seq, clamp(j, kv_lo, N_KV_STEPS), i % tiles_per_seq)),
        pl.BlockSpec((tm, LANES), lambda i, j: (i, 0)),
        pl.BlockSpec((tm, CONV_TILE), lambda i, j: (i, clamp(j, c_lo, N_CONV_STEPS))),
        pl.BlockSpec((tm, CONV_TILE), lambda i, j: (i, clamp(j, c_lo, N_CONV_STEPS))),
    ]
    out_shape = [
        jax.ShapeDtypeStruct((n, ATTN_DIM), BF16),
        jax.ShapeDtypeStruct((n_seq, 6 * KV_DIM, t), F32),
        jax.ShapeDtypeStruct((n, LANES), F32),
        jax.ShapeDtypeStruct((n, CONV_DIM), F32),
        jax.ShapeDtypeStruct((n, CONV_DIM), F32),
    ]
    return pl.pallas_call(
        _inproj_kernel,
        grid=(n // tm, N_IN_STEPS),
        in_specs=in_specs,
        out_specs=out_specs,
        out_shape=out_shape,
        scratch_shapes=[pltpu.VMEM((tm, d), BF16)],
        compiler_params=_cparams(("arbitrary", "arbitrary")),
        name="input_projection",
    )(y, shift, scale, g1, wq, wkv, wg, wc, qg, kg)


def _compress_kernel(pt_ref, *refs):
    del pt_ref
    p = PAGES_PER_STEP
    pages = refs[:p]
    wbd_ref, w1_ref, pe_ref, b1_ref, w2bd_ref, kg_ref, kc_ref, vc_ref, xs, carry = refs[p:]
    g = pl.program_id(1)
    m = p * (LANES // CMP_STRIDE)

    @pl.when(g == 0)
    def _():
        carry[...] = jnp.zeros_like(carry)

    for k in range(p):
        for jj in range(4):
            tile = pages[k][jj * LANES:(jj + 1) * LANES, :]
            xs[jj, k * LANES:(k + 1) * LANES, :] = tile.T

    row = lax.broadcasted_iota(jnp.int32, (m, 1), 0)
    for jj in range(4):
        kv = jj // 2
        acc = jnp.zeros((m, 2 * LANES), F32)
        for t in range(CMP_STRIDE):
            lhs = xs[jj, pl.ds(t, m, stride=CMP_STRIDE), :].astype(BF16)
            acc = acc + _dot(lhs, wbd_ref[kv, t])
        a_part = acc[:, :LANES]
        b_part = acc[:, LANES:]
        prev = carry[jj]
        a_shift = jnp.where(row == 0, prev[7:8, :], pltpu.roll(a_part, 1, axis=0))
        carry[jj] = a_part[m - 8:, :]
        c64 = jnp.sum(pe_ref[kv] * w1_ref[kv], axis=0, keepdims=True) + b1_ref[kv]
        pre = a_shift + b_part + jnp.concatenate([c64, c64], axis=-1)
        out = _dot(_silu(pre).astype(BF16), w2bd_ref[kv])
        if kv == 0:
            out = _headnorm_lanes(out, kg_ref[...])
        dst = kc_ref if kv == 0 else vc_ref
        hb = (jj % 2) * 2
        dst[hb] = out[:, :HEAD_DIM].astype(BF16)
        dst[hb + 1] = out[:, HEAD_DIM:].astype(BF16)


def _compress(src, page_ids, page_spec_fn, wbd, w1, pe, b1, w2bd, kg0):
    n_seq, n_pages = page_ids.shape
    p = PAGES_PER_STEP
    m = p * (LANES // CMP_STRIDE)
    n_groups = n_pages // p
    n_rows = n_pages * (LANES // CMP_STRIDE)
    page_specs = [page_spec_fn(k) for k in range(p)]

    def const(shape):
        return pl.BlockSpec(shape, lambda b, g, pt: (0,) * len(shape))

    in_specs = page_specs + [
        const(wbd.shape), const(w1.shape), const(pe.shape), const(b1.shape), const(w2bd.shape), const(kg0.shape)]
    out_spec = pl.BlockSpec((None, N_KV, m, HEAD_DIM), lambda b, g, pt: (b, 0, g, 0))
    return pl.pallas_call(
        _compress_kernel,
        grid_spec=pltpu.PrefetchScalarGridSpec(
            num_scalar_prefetch=1,
            grid=(n_seq, n_groups),
            in_specs=in_specs,
            out_specs=[out_spec, out_spec],
            scratch_shapes=[pltpu.VMEM((4, p * LANES, LANES), F32), pltpu.VMEM((4, 8, LANES), F32)],
        ),
        out_shape=[jax.ShapeDtypeStruct((n_seq, N_KV, n_rows, HEAD_DIM), BF16)] * 2,
        compiler_params=_cparams(("arbitrary", "arbitrary")),
        name="compress_mlp",
    )(page_ids, *([src] * p), wbd, w1, pe, b1, w2bd, kg0)


SEL_CHUNK = 512
WIN_KEYS = 640


def _attn_kernel(q_ref, gate_ref, kc_ref, vc_ref, kst_ref, vst_ref, kwt_ref, vwt_ref, agg_ref, exp_ref,
                 o_ref, selm_scr, *, tq, t_len):
    i = pl.program_id(2)
    q0 = i * tq
    m_rows = GQA * tq
    n_sel = t_len // SEL_BLOCK
    q = q_ref[...]
    qs = jnp.concatenate([q[:, g * HEAD_DIM:(g + 1) * HEAD_DIM] for g in range(GQA)], axis=0)
    row = lax.broadcasted_iota(jnp.int32, (m_rows, 1), 0)
    qpos = q0 + (row % tq)

    s_c = _dot_nt(qs, kc_ref[...])
    r_i = lax.broadcasted_iota(jnp.int32, (1, s_c.shape[-1]), 1)
    mask_c = (r_i >= 1) & (CMP_STRIDE * r_i + (CMP_STRIDE - 1) <= qpos)
    p_c = _masked_softmax(s_c, mask_c)
    o_c = _dot(p_c.astype(BF16), vc_ref[...])
    imp = p_c[0:tq] + p_c[tq:2 * tq] + p_c[2 * tq:3 * tq] + p_c[3 * tq:4 * tq]
    imp_s = _split3_dot(imp, agg_ref[...])

    sidx = lax.broadcasted_iota(jnp.int32, (1, LANES), 1)
    qp = q0 + lax.broadcasted_iota(jnp.int32, (tq, 1), 0)
    cur = qp // SEL_BLOCK
    forced = (sidx == 0) | (sidx == cur) | (sidx == cur - 1)
    causal = sidx * SEL_BLOCK <= qp
    score = jnp.where(causal, imp_s + jnp.where(forced, FORCE_BONUS, 0.0), NEG_INF)
    cnt = jnp.zeros((tq, LANES), jnp.int32)
    for t in range(n_sel):
        col = score[:, t:t + 1]
        beats = (col > score) | ((col == score) & (t < sidx))
        cnt = cnt + beats.astype(jnp.int32)
    sel = (cnt < min(N_SEL, n_sel)) & (score > NEG_INF / 2) & (sidx < n_sel)
    selm_scr[...] = _dot(jnp.where(sel, 1.0, 0.0).astype(BF16), exp_ref[...])

    n_chunks = (q0 + tq + SEL_CHUNK - 1) // SEL_CHUNK

    def body(c, carry):
        m_old, l_old, acc = carry
        off = pl.multiple_of(c * SEL_CHUNK, SEL_CHUNK)
        kt = kst_ref[:, pl.ds(off, SEL_CHUNK)].astype(BF16)
        s = _dot(qs, kt)
        sm = selm_scr[:, pl.ds(off, SEL_CHUNK)]
        sm4 = jnp.concatenate([sm] * GQA, axis=0)
        kpos = off + lax.broadcasted_iota(jnp.int32, (1, SEL_CHUNK), 1)
        mask = (sm4 > 0.5) & (kpos <= qpos)
        s = jnp.where(mask, s, NEG_INF)
        m_new = jnp.maximum(m_old, jnp.max(s, axis=-1, keepdims=True))
        alpha = jnp.exp(m_old - m_new)
        pr = jnp.exp(s - m_new) * mask.astype(F32)
        l_new = alpha * l_old + jnp.sum(pr, axis=-1, keepdims=True)
        vt = vst_ref[:, pl.ds(off, SEL_CHUNK)].astype(BF16)
        acc = alpha * acc + _dot_nt(pr.astype(BF16), vt)
        return m_new, l_new, acc

    init = (jnp.full((m_rows, 1), NEG_INF, F32), jnp.zeros((m_rows, 1), F32), jnp.zeros((m_rows, HEAD_DIM), F32))
    _, l_s, acc_s = lax.fori_loop(0, n_chunks, body, init)
    o_s = acc_s / jnp.maximum(l_s, 1e-30)

    start = pl.multiple_of(jnp.clip(q0 - WINDOW, 0, t_len - WIN_KEYS), LANES)
    s_w = _dot(qs, kwt_ref[:, pl.ds(start, WIN_KEYS)].astype(BF16))
    kpos_w = start + lax.broadcasted_iota(jnp.int32, (1, WIN_KEYS), 1)
    dist = qpos - kpos_w
    p_w = _masked_softmax(s_w, (dist >= 0) & (dist < WINDOW))
    o_w = _dot_nt(p_w.astype(BF16), vwt_ref[:, pl.ds(start, WIN_KEYS)].astype(BF16))

    gate = gate_ref[...]

    def gcol(jb):
        return jnp.concatenate([jnp.broadcast_to(gate[:, g * 3 + jb:g * 3 + jb + 1], (tq, HEAD_DIM))
                                for g in range(GQA)], axis=0)

    o = gcol(0) * o_c + gcol(1) * o_s + gcol(2) * o_w
    o_ref[...] = jnp.concatenate([o[g * tq:(g + 1) * tq] for g in range(GQA)], axis=-1)


def _prompt_attention(qn, gate_hm, kc, vc, kvt, agg, expand, *, tq):
    n = qn.shape[0]
    n_seq, _, t_len = kvt.shape
    tiles = t_len // tq
    kern = functools.partial(_attn_kernel, tq=tq, t_len=t_len)

    def kv_spec(block_row):
        return pl.BlockSpec((None, HEAD_DIM, t_len), lambda b, h, i: (b, block_row + h, 0))

    sel_k, sel_v = (2 * KV_DIM) // HEAD_DIM, (3 * KV_DIM) // HEAD_DIM
    win_k, win_v = (4 * KV_DIM) // HEAD_DIM, (5 * KV_DIM) // HEAD_DIM
    return pl.pallas_call(
        kern,
        grid=(n_seq, N_KV, tiles),
        in_specs=[
            pl.BlockSpec((tq, GQA * HEAD_DIM), lambda b, h, i: (b * tiles + i, h)),
            pl.BlockSpec((None, tq, 3 * GQA), lambda b, h, i: (h, b * tiles + i, 0)),
            pl.BlockSpec((None, None, kc.shape[2], HEAD_DIM), lambda b, h, i: (b, h, 0, 0)),
            pl.BlockSpec((None, None, vc.shape[2], HEAD_DIM), lambda b, h, i: (b, h, 0, 0)),
            kv_spec(sel_k), kv_spec(sel_v), kv_spec(win_k), kv_spec(win_v),
            pl.BlockSpec(agg.shape, lambda b, h, i: (0, 0)),
            pl.BlockSpec(expand.shape, lambda b, h, i: (0, 0)),
        ],
        out_specs=pl.BlockSpec((tq, GQA * HEAD_DIM), lambda b, h, i: (b * tiles + i, h)),
        out_shape=jax.ShapeDtypeStruct((n, ATTN_DIM), F32),
        scratch_shapes=[pltpu.VMEM((tq, t_len), F32)],
        compiler_params=_cparams(("arbitrary", "arbitrary", "arbitrary")),
        name="prompt_attention",
    )(qn, gate_hm, kc, vc, kvt, kvt, kvt, kvt, agg, expand)


def _sample_select_kernel(qs_ref, kc_ref, vc_ref, agg_ref, oc_ref, idx_ref, *, q0, n_tok, n_sel, n_pick):
    rows = GQA * n_tok
    row = lax.broadcasted_iota(jnp.int32, (rows, 1), 0)
    qpos = q0 + (row % n_tok)
    imps = []
    for h in range(N_KV):
        s_c = _dot_nt(qs_ref[h].astype(BF16), kc_ref[h])
        r_i = lax.broadcasted_iota(jnp.int32, (1, s_c.shape[-1]), 1)
        mask_c = (r_i >= 1) & (CMP_STRIDE * r_i + (CMP_STRIDE - 1) <= qpos)
        p_c = _masked_softmax(s_c, mask_c)
        oc_ref[h] = _dot(p_c.astype(BF16), vc_ref[h])
        imp = p_c[0:n_tok]
        for g in range(1, GQA):
            imp = imp + p_c[g * n_tok:(g + 1) * n_tok]
        imps.append(imp)
    imp_all = jnp.concatenate(imps, axis=0)
    imp_s = _split3_dot(imp_all, agg_ref[...])
    width = imp_s.shape[-1]
    sidx = lax.broadcasted_iota(jnp.int32, (1, width), 1)
    r2 = lax.broadcasted_iota(jnp.int32, (N_KV * n_tok, 1), 0)
    qp = q0 + (r2 % n_tok)
    cur = qp // SEL_BLOCK
    forced = (sidx == 0) | (sidx == cur) | (sidx == cur - 1)
    causal = sidx * SEL_BLOCK <= qp
    score = jnp.where(causal, imp_s + jnp.where(forced, FORCE_BONUS, 0.0), NEG_INF)
    lowest = -3.0e38
    score = jnp.where((sidx >= n_sel - 1), lowest, score)
    out_lane = lax.broadcasted_iota(jnp.int32, (1, LANES), 1)
    picked = jnp.zeros((N_KV * n_tok, LANES), jnp.int32)
    for it in range(n_pick):
        mx = jnp.max(score, axis=-1, keepdims=True)
        first = jnp.min(jnp.where(score == mx, sidx, width), axis=-1, keepdims=True)
        picked = jnp.where(out_lane == it, first, picked)
        score = jnp.where(sidx == first, lowest, score)
    idx_ref[...] = picked


def _sample_select(qs, kc, vc, agg, *, q0, n_tok, n_sel):
    n_seq = qs.shape[0]
    n_pick = N_SEL - 1
    kern = functools.partial(_sample_select_kernel, q0=q0, n_tok=n_tok, n_sel=n_sel, n_pick=n_pick)
    return pl.pallas_call(
        kern,
        grid=(n_seq,),
        in_specs=[
            pl.BlockSpec((None, N_KV, GQA * n_tok, HEAD_DIM), lambda b: (b, 0, 0, 0)),
            pl.BlockSpec((None, N_KV, kc.shape[2], HEAD_DIM), lambda b: (b, 0, 0, 0)),
            pl.BlockSpec((None, N_KV, vc.shape[2], HEAD_DIM), lambda b: (b, 0, 0, 0)),
            pl.BlockSpec(agg.shape, lambda b: (0, 0)),
        ],
        out_specs=[
            pl.BlockSpec((None, N_KV, GQA * n_tok, HEAD_DIM), lambda b: (b, 0, 0, 0)),
            pl.BlockSpec((None, N_KV * n_tok, LANES), lambda b: (b, 0, 0)),
        ],
        out_shape=[
            jax.ShapeDtypeStruct((n_seq, N_KV, GQA * n_tok, HEAD_DIM), F32),
            jax.ShapeDtypeStruct((n_seq, N_KV * n_tok, LANES), jnp.int32),
        ],
        compiler_params=_cparams(("arbitrary",)),
        name="sample_select",
    )(qs, kc, vc, agg)


def _sample_attend_kernel(pg_ref, half_ref, q_ref, oc_ref, gate_ref, knew_ref, vnew_ref, kwp_ref, vwp_ref,
                          kwn_ref, vwn_ref, *refs, n_tok, n_pick):
    k_pages = refs[:n_pick]
    v_pages = refs[n_pick:2 * n_pick]
    o_ref = refs[2 * n_pick]
    b = pl.program_id(0)
    h = pl.program_id(1)
    t = pl.program_id(2)
    q = q_ref[...].astype(BF16)
    lane = lax.broadcasted_iota(jnp.int32, (1, LANES), 1)
    n_new = knew_ref.shape[-1]
    new_lane = lax.broadcasted_iota(jnp.int32, (1, n_new), 1)
    new_mask = (new_lane // n_tok == b) & (new_lane % n_tok <= t)

    flat = ((b * N_KV + h) * n_tok + t) * n_pick
    scores, masks = [], []
    for k in range(n_pick):
        half = half_ref[flat + k]
        scores.append(_dot(q, k_pages[k][...].astype(BF16)))
        masks.append(lane // SEL_BLOCK == half)
    s_new = _dot(q, knew_ref[...].astype(BF16))
    mx = jnp.max(jnp.where(new_mask, s_new, NEG_INF), axis=-1, keepdims=True)
    for s, mk in zip(scores, masks):
        mx = jnp.maximum(mx, jnp.max(jnp.where(mk, s, NEG_INF), axis=-1, keepdims=True))
    e_new = jnp.exp(jnp.where(new_mask, s_new, NEG_INF) - mx) * new_mask.astype(F32)
    den = jnp.sum(e_new, axis=-1, keepdims=True)
    acc = _dot_nt(e_new.astype(BF16), vnew_ref[...].astype(BF16))
    for k in range(n_pick):
        e = jnp.exp(jnp.where(masks[k], scores[k], NEG_INF) - mx) * masks[k].astype(F32)
        den = den + jnp.sum(e, axis=-1, keepdims=True)
        acc = acc + _dot_nt(e.astype(BF16), v_pages[k][...].astype(BF16))
    o_s = acc / jnp.maximum(den, 1e-30)

    n_win = kwp_ref.shape[-1]
    wl = lax.broadcasted_iota(jnp.int32, (1, n_win), 1)
    past_mask = (n_win - wl + t) < WINDOW
    s_p = jnp.where(past_mask, _dot(q, kwp_ref[...].astype(BF16)), NEG_INF)
    s_n = jnp.where(new_mask, _dot(q, kwn_ref[...].astype(BF16)), NEG_INF)
    mw = jnp.maximum(jnp.max(s_p, axis=-1, keepdims=True), jnp.max(s_n, axis=-1, keepdims=True))
    e_p = jnp.exp(s_p - mw) * past_mask.astype(F32)
    e_n = jnp.exp(s_n - mw) * new_mask.astype(F32)
    den_w = jnp.sum(e_p, axis=-1, keepdims=True) + jnp.sum(e_n, axis=-1, keepdims=True)
    acc_w = _dot_nt(e_p.astype(BF16), vwp_ref[...].astype(BF16)) + _dot_nt(e_n.astype(BF16), vwn_ref[...].astype(BF16))
    o_w = acc_w / jnp.maximum(den_w, 1e-30)

    gate = gate_ref[...]
    o_ref[...] = gate[:, 0:1] * oc_ref[...] + gate[:, 1:2] * o_s + gate[:, 2:3] * o_w


def _sample_attend(page_of, half_of, q8, oc8, gate8, kvt_new, cache_sel_l, cache_win_l, *, n_tok, n_pick):
    n_seq = q8.shape[0]
    kern = functools.partial(_sample_attend_kernel, n_tok=n_tok, n_pick=n_pick)
    n_new = kvt_new.shape[-1]
    n_win = cache_win_l.shape[-1]

    def qspec(last):
        return pl.BlockSpec((None, None, None, 8, last), lambda b, h, t, pg, hf: (b, h, t, 0, 0))

    def new_spec(block_row):
        return pl.BlockSpec((None, HEAD_DIM, n_new), lambda b, h, t, pg, hf: (0, block_row + h, 0))

    def win_spec(block_row):
        return pl.BlockSpec((None, HEAD_DIM, n_win), lambda b, h, t, pg, hf: (b, block_row + h, 0))

    def page_spec(k, block_row):
        def imap(b, h, t, pg, hf):
            return (pg[((b * N_KV + h) * n_tok + t) * n_pick + k], block_row + h, 0)
        return pl.BlockSpec((None, HEAD_DIM, LANES), imap)

    sel_k, sel_v = (2 * KV_DIM) // HEAD_DIM, (3 * KV_DIM) // HEAD_DIM
    win_k, win_v = (4 * KV_DIM) // HEAD_DIM, (5 * KV_DIM) // HEAD_DIM
    in_specs = ([qspec(HEAD_DIM), qspec(HEAD_DIM), qspec(3), new_spec(sel_k), new_spec(sel_v),
                 win_spec(0), win_spec(N_KV), new_spec(win_k), new_spec(win_v)]
                + [page_spec(k, 0) for k in range(n_pick)] + [page_spec(k, N_KV) for k in range(n_pick)])
    return pl.pallas_call(
        kern,
        grid_spec=pltpu.PrefetchScalarGridSpec(
            num_scalar_prefetch=2,
            grid=(n_seq, N_KV, n_tok),
            in_specs=in_specs,
            out_specs=qspec(HEAD_DIM),
        ),
        out_shape=jax.ShapeDtypeStruct(q8.shape, F32),
        compiler_params=_cparams(("arbitrary", "arbitrary", "arbitrary")),
        name="sample_attend",
    )(page_of, half_of, q8, oc8, gate8, kvt_new, kvt_new, cache_win_l, cache_win_l, kvt_new, kvt_new,
      *([cache_sel_l] * (2 * n_pick)))


def _mix_prologue(attn, u, u1, u2, bg, cw_ref, cb_ref, og_ref, o_scr):
    og = og_ref[...]
    an = attn * lax.rsqrt(jnp.mean(attn * attn, axis=-1, keepdims=True) + EPS) * og[:, :ATTN_DIM]
    yv = cb_ref[...] + cw_ref[0:1, :] * u2
    yv = yv + cw_ref[1:2, :] * u1
    yv = yv + cw_ref[2:3, :] * u
    cv = bg * yv
    cn = cv * lax.rsqrt(jnp.mean(cv * cv, axis=-1, keepdims=True) + EPS) * og[:, ATTN_DIM:]
    o_scr[:, :ATTN_DIM] = an.astype(BF16)
    o_scr[:, ATTN_DIM:] = cn.astype(BF16)


def _outproj_seq_kernel(attn_ref, u_ref, halo_ref, bg_ref, cw_ref, cb_ref, og_ref, w_ref, y_ref, gm_ref,
                        o_ref, o_scr, *, tiles_per_seq):
    i = pl.program_id(0)

    @pl.when(pl.program_id(1) == 0)
    def _():
        u = u_ref[...]
        tm = u.shape[0]
        row = lax.broadcasted_iota(jnp.int32, (tm, 1), 0)
        halo = jnp.where(i % tiles_per_seq == 0, 0.0, halo_ref[...])
        p1 = halo[7:8, :]
        p2 = halo[6:7, :]
        u1 = jnp.where(row == 0, p1, pltpu.roll(u, 1, axis=0))
        u2 = jnp.where(row == 0, p2, jnp.where(row == 1, p1, pltpu.roll(u, 2, axis=0)))
        _mix_prologue(attn_ref[...], u, u1, u2, bg_ref[...], cw_ref, cb_ref, og_ref, o_scr)

    o_ref[...] = y_ref[...] + gm_ref[...] * _dot(o_scr[...], w_ref[...])


def _outproj_state_kernel(attn_ref, u_ref, e1_ref, e2_ref, bg_ref, cw_ref, cb_ref, og_ref, w_ref, y_ref, gm_ref,
                          o_ref, o_scr, *, n_tok):
    @pl.when(pl.program_id(1) == 0)
    def _():
        u = u_ref[...]
        tm = u.shape[0]
        k = lax.broadcasted_iota(jnp.int32, (tm, 1), 0) % n_tok
        u1 = jnp.where(k >= 1, pltpu.roll(u, 1, axis=0), e1_ref[...])
        u2 = jnp.where(k >= 2, pltpu.roll(u, 2, axis=0), e2_ref[...])
        _mix_prologue(attn_ref[...], u, u1, u2, bg_ref[...], cw_ref, cb_ref, og_ref, o_scr)

    o_ref[...] = y_ref[...] + gm_ref[...] * _dot(o_scr[...], w_ref[...])


def _outproj(attn, u, bg, cw, cb, og, w, y, gmod, *, tm, n_seq, ext=None, n_tok=None):
    n, d = y.shape
    tn = 512
    tiles_per_seq = (n // n_seq) // tm
    tiles_per_mod = (n // tm) // gmod.shape[0]
    r = gmod.shape[1]
    full = lambda i, j: (i, 0)
    c0 = lambda i, j: (0, 0)
    common = [pl.BlockSpec((tm, CONV_DIM), full), pl.BlockSpec((3, CONV_DIM), c0), pl.BlockSpec((1, CONV_DIM), c0),
              pl.BlockSpec((1, d), c0), pl.BlockSpec((d, tn), lambda i, j: (0, j)),
              pl.BlockSpec((tm, tn), lambda i, j: (i, j)),
              pl.BlockSpec((None, r, tn), lambda i, j: (i // tiles_per_mod, 0, j))]
    head = [pl.BlockSpec((tm, ATTN_DIM), full), pl.BlockSpec((tm, CONV_DIM), full)]
    if ext is None:
        kern = functools.partial(_outproj_seq_kernel, tiles_per_seq=tiles_per_seq)
        extra = [pl.BlockSpec((8, CONV_DIM), lambda i, j: (jnp.maximum(i * (tm // 8) - 1, 0), 0))]
        args = (attn, u, u, bg, cw, cb, og, w, y, gmod)
    else:
        kern = functools.partial(_outproj_state_kernel, n_tok=n_tok)
        extra = [pl.BlockSpec((tm, CONV_DIM), full), pl.BlockSpec((tm, CONV_DIM), full)]
        args = (attn, u, ext[0], ext[1], bg, cw, cb, og, w, y, gmod)
    return pl.pallas_call(
        kern,
        grid=(n // tm, d // tn),
        in_specs=head + extra + common,
        out_specs=pl.BlockSpec((tm, tn), lambda i, j: (i, j)),
        out_shape=jax.ShapeDtypeStruct((n, d), F32),
        scratch_shapes=[pltpu.VMEM((tm, d), BF16)],
        compiler_params=_cparams(("arbitrary", "arbitrary")),
        name="output_projection",
    )(*args)


def _ffn_kernel(y_ref, sh_ref, sc_ref, gm_ref, g2_ref, w1_ref, w3_ref, w2_ref, o_ref, h_scr, acc):
    j = pl.program_id(1)

    @pl.when(j == 0)
    def _():
        h_scr[...] = _rms_modulate(y_ref[...], g2_ref[...], sh_ref[...], sc_ref[...]).astype(BF16)
        acc[...] = jnp.zeros_like(acc)

    h = h_scr[...]
    hid = _silu(_dot(h, w1_ref[...])) * _dot(h, w3_ref[...])
    acc[...] += _dot(hid.astype(BF16), w2_ref[...])

    @pl.when(j == pl.num_programs(1) - 1)
    def _():
        o_ref[...] = y_ref[...] + gm_ref[...] * acc[...]


def _dense_ffn(y, shift, scale, gmod, g2, w1, w3, w2, *, tm):
    n, d = y.shape
    f = w1.shape[1]
    tf = 512
    tiles_per_mod = (n // tm) // shift.shape[0]
    r = shift.shape[1]
    mod = pl.BlockSpec((None, r, d), lambda i, j: (i // tiles_per_mod, 0, 0))
    return pl.pallas_call(
        _ffn_kernel,
        grid=(n // tm, f // tf),
        in_specs=[pl.BlockSpec((tm, d), lambda i, j: (i, 0)), mod, mod, mod,
                  pl.BlockSpec((1, d), lambda i, j: (0, 0)),
                  pl.BlockSpec((d, tf), lambda i, j: (0, j)),
                  pl.BlockSpec((d, tf), lambda i, j: (0, j)),
                  pl.BlockSpec((tf, d), lambda i, j: (j, 0))],
        out_specs=pl.BlockSpec((tm, d), lambda i, j: (i, 0)),
        out_shape=jax.ShapeDtypeStruct((n, d), F32),
        scratch_shapes=[pltpu.VMEM((tm, d), BF16), pltpu.VMEM((tm, d), F32)],
        compiler_params=_cparams(("arbitrary", "arbitrary")),
        name="dense_ffn",
    )(y, shift, scale, gmod, g2, w1, w3, w2)


def _moe_kernel(y_ref, sh_ref, sc_ref, gm_ref, g2_ref, rw_ref, w1_ref, w3_ref, w2_ref, o_ref, h_scr, acc, comb):
    e = pl.program_id(1)
    j = pl.program_id(2)
    lane = lax.broadcasted_iota(jnp.int32, (1, LANES), 1)

    @pl.when((e == 0) & (j == 0))
    def _():
        h = _rms_modulate(y_ref[...], g2_ref[...], sh_ref[...], sc_ref[...])
        h_scr[...] = h.astype(BF16)
        acc[...] = jnp.zeros_like(acc)
        rw = rw_ref[...]
        rw_hi = rw.astype(BF16)
        rw_lo = (rw - rw_hi.astype(F32)).astype(BF16)
        h_hi = h_scr[...]
        h_lo = (h - h_hi.astype(F32)).astype(BF16)
        logits = _dot_nt(h_hi, rw_hi) + _dot_nt(h_lo, rw_hi) + _dot_nt(h_hi, rw_lo)
        lowest = -3.0e38
        logits = jnp.where(lane < N_EXPERTS, logits, lowest)
        m1 = jnp.max(logits, axis=-1, keepdims=True)
        i1 = jnp.min(jnp.where(logits == m1, lane, LANES), axis=-1, keepdims=True)
        rest = jnp.where(lane == i1, lowest, logits)
        m2 = jnp.max(rest, axis=-1, keepdims=True)
        i2 = jnp.min(jnp.where(rest == m2, lane, LANES), axis=-1, keepdims=True)
        e2 = jnp.exp(m2 - m1)
        den = 1.0 + e2
        comb[...] = jnp.where(lane == i1, 1.0 / den, 0.0) + jnp.where(lane == i2, e2 / den, 0.0)

    h = h_scr[...]
    hid = _silu(_dot(h, w1_ref[...])) * _dot(h, w3_ref[...])
    ce = jnp.sum(jnp.where(lane == e, comb[...], 0.0), axis=-1, keepdims=True)
    acc[...] += ce * _dot(hid.astype(BF16), w2_ref[...])

    @pl.when((e == pl.num_programs(1) - 1) & (j == pl.num_programs(2) - 1))
    def _():
        o_ref[...] = y_ref[...] + gm_ref[...] * acc[...]


def _moe_ffn(y, shift, scale, gmod, g2, rw, w1, w3, w2, *, tm):
    n, d = y.shape
    n_e, _, fe = w1.shape
    tf = 256
    tiles_per_mod = (n // tm) // shift.shape[0]
    r = shift.shape[1]
    mod = pl.BlockSpec((None, r, d), lambda i, e, j: (i // tiles_per_mod, 0, 0))
    return pl.pallas_call(
        _moe_kernel,
        grid=(n // tm, n_e, fe // tf),
        in_specs=[pl.BlockSpec((tm, d), lambda i, e, j: (i, 0)), mod, mod, mod,
                  pl.BlockSpec((1, d), lambda i, e, j: (0, 0)),
                  pl.BlockSpec((LANES, d), lambda i, e, j: (0, 0)),
                  pl.BlockSpec((None, d, tf), lambda i, e, j: (e, 0, j)),
                  pl.BlockSpec((None, d, tf), lambda i, e, j: (e, 0, j)),
                  pl.BlockSpec((None, tf, d), lambda i, e, j: (e, j, 0))],
        out_specs=pl.BlockSpec((tm, d), lambda i, e, j: (i, 0)),
        out_shape=jax.ShapeDtypeStruct((n, d), F32),
        scratch_shapes=[pltpu.VMEM((tm, d), BF16), pltpu.VMEM((tm, d), F32), pltpu.VMEM((tm, LANES), F32)],
        compiler_params=_cparams(("arbitrary", "arbitrary", "arbitrary")),
        name="moe_ffn",
    )(y, shift, scale, gmod, g2, rw, w1, w3, w2)


def _agg_matrix(n_rows, n_cmp, n_sel, width):
    a = np.zeros((n_rows, width), np.float32)
    ratio = SEL_BLOCK // CMP_STRIDE
    for s in range(n_sel):
        for tkn in range(s * ratio - 1, s * ratio + ratio):
            if 0 <= tkn < n_cmp and tkn + 1 < n_rows:
                a[tkn + 1, s] = 1.0
    return jnp.asarray(a, BF16)


def _expand_matrix(t_len):
    e = np.zeros((LANES, t_len), np.float32)
    for s in range(t_len // SEL_BLOCK):
        e[s, s * SEL_BLOCK:(s + 1) * SEL_BLOCK] = 1.0
    return jnp.asarray(e, BF16)


def _layer_weights(l, w_in, q_gain, k_gain, cmp_w1, cmp_b1, cmp_w2, cmp_pe, conv_w, conv_b, out_gain, w_out):
    d = w_in.shape[1]
    wt = jnp.transpose(w_in[l]).astype(BF16)
    wq = wt[:KV_OFF]
    wkv = wt[KV_OFF:GATE_OFF]
    wg = jnp.zeros((LANES, d), BF16).at[:CONV_OFF - GATE_OFF].set(wt[GATE_OFF:CONV_OFF])
    wc = wt[CONV_OFF:].reshape(3, CONV_DIM // CONV_TILE, CONV_TILE, d).transpose(1, 0, 2, 3).reshape(3 * CONV_DIM, d)
    qg = jnp.tile(q_gain[l] * ATTN_SCALE, Q_TILE // HEAD_DIM).reshape(1, Q_TILE)
    kg = jnp.broadcast_to(k_gain[l][:, None, :, None], (3, N_KV, HEAD_DIM, 1))
    w1 = cmp_w1[l]
    half = CMP_STRIDE * HEAD_DIM
    wa = w1[:, :half].reshape(2, CMP_STRIDE, HEAD_DIM, HEAD_DIM)
    wb = w1[:, half:].reshape(2, CMP_STRIDE, HEAD_DIM, HEAD_DIM)
    z = jnp.zeros_like(wa)
    top = jnp.concatenate([wa, z, wb, z], axis=-1)
    bot = jnp.concatenate([z, wa, z, wb], axis=-1)
    wbd = jnp.concatenate([top, bot], axis=-2).astype(BF16)
    w2 = cmp_w2[l]
    z2 = jnp.zeros_like(w2)
    w2bd = jnp.concatenate([jnp.concatenate([w2, z2], -1), jnp.concatenate([z2, w2], -1)], -2).astype(BF16)
    pe = cmp_pe[l].reshape(2, 2 * half, 1)
    b1 = cmp_b1[l].reshape(2, 1, HEAD_DIM)
    kg0 = jnp.tile(k_gain[l, 0], 2).reshape(1, LANES)
    return dict(wq=wq, wkv=wkv, wg=wg, wc=wc, qg=qg, kg=kg, wbd=wbd, w1=w1, pe=pe, b1=b1, w2bd=w2bd, kg0=kg0,
                cw=conv_w[l], cb=conv_b[l].reshape(1, CONV_DIM), og=out_gain[l].reshape(1, d),
                w_out=w_out[l].astype(BF16))


def kernel(x_prompt, x_sample, c_prompt, c_sample, cache_cmp, cache_sel, cache_win, state_conv, page_table, w_ada, b_ada, norm1, norm2, w_in, q_gain, k_gain, cmp_w1, cmp_b1, cmp_w2, cmp_pe, conv_w, conv_b, out_gain, w_out, ffn_w1, ffn_w3, ffn_w2, router_w, moe_w1, moe_w3, moe_w2):
    bp, t_p, d = x_prompt.shape
    bs, t_s, _ = x_sample.shape
    depth = w_in.shape[0]
    n_pool, page = cache_cmp.shape[1], cache_cmp.shape[2]
    n_pages = page_table.shape[1]
    past_len = n_pages * page
    n_p, n_s = bp * t_p, bs * t_s
    win_buf = cache_win.shape[2]
    feat = 2 * KV_DIM
    assert page == LANES and n_pages % PAGES_PER_STEP == 0 and t_p % (PAGES_PER_STEP * LANES) == 0
    n_cmp_rows = past_len // CMP_STRIDE
    assert past_len % SEL_BLOCK == 0 and t_s < CMP_STRIDE
    n_sel_s = past_len // SEL_BLOCK + 1

    cmp_t = jnp.transpose(cache_cmp, (0, 1, 3, 4, 5, 2)).reshape(depth, n_pool, feat, page)
    sel_t = jnp.transpose(cache_sel, (0, 1, 3, 4, 5, 2)).reshape(depth, n_pool, feat, page)
    win_t = jnp.transpose(cache_win, (0, 1, 3, 4, 5, 2)).reshape(depth, bs, feat, win_buf)

    c_all = jnp.concatenate([c_prompt, c_sample, jnp.zeros((16 - bp - bs, d), F32)], axis=0)
    mod = _modulation(c_all, w_ada, b_ada)

    agg_p = _agg_matrix(t_p // CMP_STRIDE, t_p // CMP_STRIDE - 1, t_p // SEL_BLOCK, LANES)
    exp_p = _expand_matrix(t_p)
    sel_w = -(-n_sel_s // LANES) * LANES
    agg_s = _agg_matrix(n_cmp_rows, n_cmp_rows + 3, n_sel_s, sel_w)
    prompt_pages = jnp.broadcast_to(jnp.arange(t_p // LANES, dtype=jnp.int32), (bp, t_p // LANES))

    yp = x_prompt.reshape(n_p, d)
    ys = x_sample.reshape(n_s, d)
    outs = {k: [] for k in ("cmp_p", "sel_p", "win_p", "conv_p", "cmp_s", "sel_s", "win_s", "conv_s")}
    tm_p = 1024
    n_pick = N_SEL - 1
    for l in range(depth):
        lw = _layer_weights(l, w_in, q_gain, k_gain, cmp_w1, cmp_b1, cmp_w2, cmp_pe, conv_w, conv_b, out_gain, w_out)
        mp = [mod[l, :bp, k * d:(k + 1) * d].reshape(bp, 1, d) for k in range(6)]
        ms = [jnp.repeat(mod[l, bp:bp + bs, k * d:(k + 1) * d], t_s, axis=0).reshape(1, n_s, d) for k in range(6)]
        g1 = norm1[l].reshape(1, d)
        g2 = norm2[l].reshape(1, d)

        qn, kvt, gate, u, bg = _inproj(yp, mp[0], mp[1], g1, lw["wq"], lw["wkv"], lw["wg"], lw["wc"], lw["qg"],
                                       lw["kg"], tm=tm_p, n_seq=bp)
        kc, vc = _compress(
            kvt, prompt_pages,
            lambda k: pl.BlockSpec((None, feat, LANES), lambda b, g, pt: (b, 0, pt[b, g * PAGES_PER_STEP + k])),
            lw["wbd"], lw["w1"], lw["pe"], lw["b1"], lw["w2bd"], lw["kg0"])
        gate_hm = gate[:, :3 * N_HEADS].reshape(n_p, N_KV, 3 * GQA).transpose(1, 0, 2)
        attn = _prompt_attention(qn, gate_hm, kc, vc, kvt, agg_p, exp_p, tq=128)
        yp = _outproj(attn, u, bg, lw["cw"], lw["cb"], lw["og"], lw["w_out"], yp, mp[2], tm=tm_p, n_seq=bp)
        kv6 = kvt.reshape(bp, 3, 2, N_KV, HEAD_DIM, t_p)
        outs["cmp_p"].append(kv6[:, 0])
        outs["sel_p"].append(kv6[:, 1])
        outs["win_p"].append(kv6[:, 2, ..., t_p - min(WINDOW, t_p):])
        outs["conv_p"].append(u.reshape(bp, t_p, CONV_DIM)[:, t_p - 2:])

        qn_s, kvt_s, gate_s, u_s, bg_s = _inproj(ys, ms[0], ms[1], g1, lw["wq"], lw["wkv"], lw["wg"], lw["wc"],
                                                 lw["qg"], lw["kg"], tm=n_s, n_seq=1)
        layer = l
        kc_s, vc_s = _compress(
            cmp_t, page_table,
            lambda k: pl.BlockSpec((None, None, feat, LANES),
                                   lambda b, g, pt: (layer, pt[b, g * PAGES_PER_STEP + k], 0, 0)),
            lw["wbd"], lw["w1"], lw["pe"], lw["b1"], lw["w2bd"], lw["kg0"])
        q5 = qn_s.astype(F32).reshape(bs, t_s, N_KV, GQA, HEAD_DIM)
        qs_sel = q5.transpose(0, 2, 3, 1, 4).reshape(bs, N_KV, GQA * t_s, HEAD_DIM)
        oc, picked = _sample_select(qs_sel, kc_s, vc_s, agg_s, q0=past_len, n_tok=t_s, n_sel=n_sel_s)
        blocks = picked[:, :, :n_pick].reshape(bs, N_KV, t_s, n_pick)
        page_of = page_table[jnp.arange(bs)[:, None, None, None], blocks // 2].reshape(-1)
        half_of = (blocks % 2).reshape(-1)
        pad8 = ((0, 0), (0, 0), (0, 0), (0, 8 - GQA), (0, 0))
        q8 = jnp.pad(q5.transpose(0, 2, 1, 3, 4), pad8)
        oc8 = jnp.pad(oc.reshape(bs, N_KV, GQA, t_s, HEAD_DIM).transpose(0, 1, 3, 2, 4), pad8)
        gate8 = jnp.pad(gate_s[:, :3 * N_HEADS].reshape(bs, t_s, N_KV, GQA, 3).transpose(0, 2, 1, 3, 4), pad8)
        attn8 = _sample_attend(page_of, half_of, q8, oc8, gate8, kvt_s, sel_t[l], win_t[l], n_tok=t_s, n_pick=n_pick)
        attn_s = attn8[:, :, :, :GQA].transpose(0, 2, 1, 3, 4).reshape(n_s, ATTN_DIM)
        st = state_conv[l]
        zero = jnp.zeros((bs, t_s - 1, CONV_DIM), F32)
        e1 = jnp.concatenate([st[:, 1:2], zero], axis=1).reshape(n_s, CONV_DIM)
        e2 = jnp.concatenate([st[:, 0:1], st[:, 1:2], zero[:, 1:]], axis=1).reshape(n_s, CONV_DIM)
        ys = _outproj(attn_s, u_s, bg_s, lw["cw"], lw["cb"], lw["og"], lw["w_out"], ys, ms[2], tm=n_s, n_seq=1,
                      ext=(e1, e2), n_tok=t_s)
        rows_s = kvt_s[0].T.reshape(bs, t_s, 3, 2, N_KV, HEAD_DIM)
        outs["cmp_s"].append(rows_s[:, :, 0])
        outs["sel_s"].append(rows_s[:, :, 1])
        new_win_t = kvt_s[0, 4 * KV_DIM:].reshape(feat, bs, t_s).transpose(1, 0, 2)
        outs["win_s"].append(jnp.concatenate([win_t[l], new_win_t], axis=-1)[..., t_s:])
        u_ext = jnp.concatenate([st, u_s.reshape(bs, t_s, CONV_DIM)], axis=1)
        outs["conv_s"].append(u_ext[:, t_s:])

        if l % 2 == 0:
            w1, w3, w2 = (ffn_w1[l // 2].astype(BF16), ffn_w3[l // 2].astype(BF16), ffn_w2[l // 2].astype(BF16))
            yp = _dense_ffn(yp, mp[3], mp[4], mp[5], g2, w1, w3, w2, tm=512)
            ys = _dense_ffn(ys, ms[3], ms[4], ms[5], g2, w1, w3, w2, tm=n_s)
        else:
            w1, w3, w2 = (moe_w1[l // 2].astype(BF16), moe_w3[l // 2].astype(BF16), moe_w2[l // 2].astype(BF16))
            rw = jnp.zeros((LANES, d), F32).at[:N_EXPERTS].set(jnp.transpose(router_w[l // 2]))
            yp = _moe_ffn(yp, mp[3], mp[4], mp[5], g2, rw, w1, w3, w2, tm=512)
            ys = _moe_ffn(ys, ms[3], ms[4], ms[5], g2, rw, w1, w3, w2, tm=n_s)

    def rows_major(x):
        return jnp.transpose(x, (0, 1, 5, 2, 3, 4))

    new_win_s = jnp.stack(outs["win_s"]).reshape(depth, bs, 2, N_KV, HEAD_DIM, win_buf)
    return (yp.reshape(bp, t_p, d), ys.reshape(bs, t_s, d),
            rows_major(jnp.stack(outs["cmp_p"])), rows_major(jnp.stack(outs["sel_p"])),
            rows_major(jnp.stack(outs["win_p"])), jnp.stack(outs["conv_p"]),
            jnp.stack(outs["cmp_s"]), jnp.stack(outs["sel_s"]),
            rows_major(new_win_s), jnp.stack(outs["conv_s"]))
```

```python
import functools

import numpy as np
import jax
import jax.numpy as jnp
from jax import lax
from jax.experimental import pallas as pl
from jax.experimental.pallas import tpu as pltpu

F32 = jnp.float32
BF16 = jnp.bfloat16

D_MODEL = 2048
HEAD_DIM = 64
ATTN_DIM = 1024
N_HEADS = 16
N_KV = 4
GQA = 4
KV_DIM = 256
CONV_DIM = 1024
CMP_STRIDE = 16
SEL_BLOCK = 64
N_SEL = 16
WINDOW = 512
N_EXPERTS = 8
TOP_K = 2
KV_OFF = ATTN_DIM
GATE_OFF = KV_OFF + 6 * KV_DIM
CONV_OFF = GATE_OFF + 3 * N_HEADS
IN_COLS = CONV_OFF + 3 * CONV_DIM
EPS = 1e-6
NEG_INF = -1e30
FORCE_BONUS = 1e4
ATTN_SCALE = HEAD_DIM ** -0.5
LANES = 128
VMEM_LIMIT = 56 * 1024 * 1024
PAGES_PER_STEP = 16
NT_DIMS = (((1,), (1,)), ((), ()))


def _cparams(sem):
    return pltpu.CompilerParams(dimension_semantics=sem, vmem_limit_bytes=VMEM_LIMIT)


def _dot(a, b):
    return jnp.dot(a, b, preferred_element_type=F32)


def _dot_nt(a, b):
    return lax.dot_general(a, b, NT_DIMS, preferred_element_type=F32)


def _silu(x):
    return x * jax.nn.sigmoid(x)


def _split3_dot(x, w_hi):
    x1 = x.astype(BF16)
    r1 = x - x1.astype(F32)
    x2 = r1.astype(BF16)
    x3 = (r1 - x2.astype(F32)).astype(BF16)
    return _dot(x1, w_hi) + _dot(x2, w_hi) + _dot(x3, w_hi)


def _headnorm_lanes(x, gain):
    lane = lax.broadcasted_iota(jnp.int32, (1, LANES), 1)
    lo = lane < HEAD_DIM
    outs = []
    for c in range(x.shape[-1] // LANES):
        xc = x[:, c * LANES:(c + 1) * LANES]
        x2 = xc * xc
        s_lo = jnp.sum(jnp.where(lo, x2, 0.0), axis=-1, keepdims=True)
        s_hi = jnp.sum(jnp.where(lo, 0.0, x2), axis=-1, keepdims=True)
        r = jnp.where(lo, lax.rsqrt(s_lo * (1.0 / HEAD_DIM) + EPS), lax.rsqrt(s_hi * (1.0 / HEAD_DIM) + EPS))
        outs.append(xc * r)
    return jnp.concatenate(outs, axis=-1) * gain


def _masked_softmax(s, mask):
    s = jnp.where(mask, s, NEG_INF)
    m = jnp.max(s, axis=-1, keepdims=True)
    e = jnp.where(mask, jnp.exp(s - m), 0.0)
    return e / jnp.maximum(jnp.sum(e, axis=-1, keepdims=True), 1e-30)


def _rms_modulate(x, g, shift, scale):
    r = lax.rsqrt(jnp.mean(x * x, axis=-1, keepdims=True) + EPS)
    return (x * r * g) * (1.0 + scale) + shift


def _mod_kernel(c_ref, w_ref, b_ref, o_ref):
    s = _silu(c_ref[...])
    o_ref[...] = _dot(s.astype(BF16), w_ref[...].astype(BF16)) + b_ref[...]


def _modulation(c_all, w_ada, b_ada):
    depth, d, cols = w_ada.shape
    rows = c_all.shape[0]
    tn = 1024
    return pl.pallas_call(
        _mod_kernel,
        grid=(depth, cols // tn),
        in_specs=[pl.BlockSpec((rows, d), lambda l, j: (0, 0)),
                  pl.BlockSpec((None, d, tn), lambda l, j: (l, 0, j)),
                  pl.BlockSpec((None, 1, tn), lambda l, j: (l, 0, j))],
        out_specs=pl.BlockSpec((None, rows, tn), lambda l, j: (l, 0, j)),
        out_shape=jax.ShapeDtypeStruct((depth, rows, cols), F32),
        compiler_params=_cparams(("arbitrary", "arbitrary")),
        name="adaln_modulation",
    )(c_all, w_ada, b_ada.reshape(depth, 1, cols))


Q_TILE = 512
KV_TILE = 512
CONV_TILE = 256
N_Q_STEPS = ATTN_DIM // Q_TILE
N_KV_STEPS = 3
N_CONV_STEPS = CONV_DIM // CONV_TILE
N_IN_STEPS = N_Q_STEPS + N_KV_STEPS + N_CONV_STEPS


def _inproj_kernel(y_ref, sh_ref, sc_ref, g1_ref, wq_ref, wkv_ref, wg_ref, wc_ref, qg_ref, kg_ref,
                   qn_ref, kvt_ref, gate_ref, u_ref, bg_ref, h_scr):
    j = pl.program_id(1)

    @pl.when(j == 0)
    def _():
        h = _rms_modulate(y_ref[...], g1_ref[...], sh_ref[...], sc_ref[...])
        h_scr[...] = h.astype(BF16)
        gate_ref[...] = jax.nn.sigmoid(_dot_nt(h_scr[...], wg_ref[...]))

    @pl.when(j < N_Q_STEPS)
    def _():
        p = _dot_nt(h_scr[...], wq_ref[...])
        qn_ref[...] = _headnorm_lanes(p, qg_ref[...]).astype(BF16)

    @pl.when((j >= N_Q_STEPS) & (j < N_Q_STEPS + N_KV_STEPS))
    def _():
        pt = _dot_nt(wkv_ref[...], h_scr[...])
        tok = pt.shape[-1]
        k3 = pt[:KV_DIM].reshape(N_KV, HEAD_DIM, tok)
        ms = jnp.mean(k3 * k3, axis=1, keepdims=True)
        kn = k3 * lax.rsqrt(ms + EPS) * kg_ref[...]
        k3 = jnp.where(j == N_Q_STEPS, k3, kn)
        kvt_ref[:KV_DIM, :] = k3.reshape(KV_DIM, tok)
        kvt_ref[KV_DIM:, :] = pt[KV_DIM:]

    @pl.when(j >= N_Q_STEPS + N_KV_STEPS)
    def _():
        p = _dot_nt(h_scr[...], wc_ref[...])
        u_ref[...] = p[:, 2 * CONV_TILE:] * p[:, :CONV_TILE]
        bg_ref[...] = p[:, CONV_TILE:2 * CONV_TILE]


def _inproj(y, shift, scale, g1, wt_all, wc_all, qg, kg, *, tm, n_seq, layer):
    n, d = y.shape
    t = n // n_seq
    tiles_per_seq = t // tm
    tiles_per_mod = (n // tm) // shift.shape[0]
    r = shift.shape[1]

    def clamp(j, lo, cnt):
        return jnp.clip(j - lo, 0, cnt - 1)

    q_lo, kv_lo, c_lo = 0, N_Q_STEPS, N_Q_STEPS + N_KV_STEPS
    in_specs = [
        pl.BlockSpec((tm, d), lambda i, j: (i, 0)),
        pl.BlockSpec((None, r, d), lambda i, j: (i // tiles_per_mod, 0, 0)),
        pl.BlockSpec((None, r, d), lambda i, j: (i // tiles_per_mod, 0, 0)),
        pl.BlockSpec((1, d), lambda i, j: (0, 0)),
        pl.BlockSpec((None, Q_TILE, d), lambda i, j: (layer, clamp(j, q_lo, N_Q_STEPS), 0)),
        pl.BlockSpec((None, KV_TILE, d), lambda i, j: (layer, KV_OFF // KV_TILE + clamp(j, kv_lo, N_KV_STEPS), 0)),
        pl.BlockSpec((None, LANES, d), lambda i, j: (layer, GATE_OFF // LANES, 0)),
        pl.BlockSpec((None, 3 * CONV_TILE, d), lambda i, j: (layer, clamp(j, c_lo, N_CONV_STEPS), 0)),
        pl.BlockSpec((1, Q_TILE), lambda i, j: (0, 0)),
        pl.BlockSpec((None, N_KV, HEAD_DIM, 1), lambda i, j: (clamp(j, kv_lo, N_KV_STEPS), 0, 0, 0)),
    ]
    out_specs = [
        pl.BlockSpec((tm, Q_TILE), lambda i, j: (i, clamp(j, q_lo, N_Q_STEPS))),
        pl.BlockSpec((None, KV_TILE, tm),
                     lambda i, j: (i // tiles_per_seq, clamp(j, kv_lo, N_KV_STEPS), i % tiles_per_seq)),
        pl.BlockSpec((tm, LANES), lambda i, j: (i, 0)),
        pl.BlockSpec((tm, CONV_TILE), lambda i, j: (i, clamp(j, c_lo, N_CONV_STEPS))),
        pl.BlockSpec((tm, CONV_TILE), lambda i, j: (i, clamp(j, c_lo, N_CONV_STEPS))),
    ]
    out_shape = [
        jax.ShapeDtypeStruct((n, ATTN_DIM), BF16),
        jax.ShapeDtypeStruct((n_seq, 6 * KV_DIM, t), F32),
        jax.ShapeDtypeStruct((n, LANES), F32),
        jax.ShapeDtypeStruct((n, CONV_DIM), F32),
        jax.ShapeDtypeStruct((n, CONV_DIM), F32),
    ]
    return pl.pallas_call(
        _inproj_kernel,
        grid=(n // tm, N_IN_STEPS),
        in_specs=in_specs,
        out_specs=out_specs,
        out_shape=out_shape,
        scratch_shapes=[pltpu.VMEM((tm, d), BF16)],
        compiler_params=_cparams(("arbitrary", "arbitrary")),
        name="input_projection",
    )(y, shift, scale, g1, wt_all, wt_all, wt_all, wc_all, qg, kg)


def _compress_kernel(pt_ref, *refs):
    del pt_ref
    p = PAGES_PER_STEP
    pages = refs[:p]
    wbd_ref, w1_ref, pe_ref, b1_ref, w2bd_ref, kg_ref, kc_ref, vc_ref, xs, carry, cconst = refs[p:]
    g = pl.program_id(1)
    cpp = LANES // CMP_STRIDE
    m = p * cpp

    @pl.when(g == 0)
    def _():
        carry[...] = jnp.zeros_like(carry)

    @pl.when((g == 0) & (pl.program_id(0) == 0))
    def _():
        for kv in range(2):
            c64 = jnp.sum(pe_ref[kv] * w1_ref[kv], axis=0, keepdims=True) + b1_ref[kv]
            cconst[kv] = jnp.broadcast_to(jnp.concatenate([c64, c64], axis=-1), (8, LANES))

    for k in range(p):
        for jj in range(4):
            tile = pages[k][jj * LANES:(jj + 1) * LANES, :]
            xs[jj, k * LANES:(k + 1) * LANES, :] = tile.T

    row = lax.broadcasted_iota(jnp.int32, (m, 1), 0)
    for jj in range(4):
        kv = jj // 2
        acc = jnp.zeros((m, 2 * LANES), F32)
        for t in range(CMP_STRIDE):
            acc = acc + _dot(xs[jj, pl.ds(t, m, stride=CMP_STRIDE), :].astype(BF16), wbd_ref[kv, t])
        a_part = acc[:, :LANES]
        b_part = acc[:, LANES:]
        prev = carry[jj]
        a_shift = jnp.where(row == 0, prev[7:8, :], pltpu.roll(a_part, 1, axis=0))
        carry[jj] = a_part[m - 8:, :]
        pre = a_shift + b_part + cconst[kv][0:1, :]
        out = _dot(_silu(pre).astype(BF16), w2bd_ref[kv])
        if kv == 0:
            out = _headnorm_lanes(out, kg_ref[...])
        dst = kc_ref if kv == 0 else vc_ref
        hb = (jj % 2) * 2
        dst[hb] = out[:, :HEAD_DIM].astype(BF16)
        dst[hb + 1] = out[:, HEAD_DIM:].astype(BF16)


def _compress(src, page_ids, page_spec_fn, wbd, w1, pe, b1, w2bd, kg0):
    n_seq, n_pages = page_ids.shape
    p = PAGES_PER_STEP
    m = p * (LANES // CMP_STRIDE)
    n_groups = n_pages // p
    n_rows = n_pages * (LANES // CMP_STRIDE)
    page_specs = [page_spec_fn(k) for k in range(p)]

    def const(shape):
        return pl.BlockSpec(shape, lambda b, g, pt: (0,) * len(shape))

    in_specs = page_specs + [const(wbd.shape), const(w1.shape), const(pe.shape), const(b1.shape),
                             const(w2bd.shape), const(kg0.shape)]
    out_spec = pl.BlockSpec((None, N_KV, m, HEAD_DIM), lambda b, g, pt: (b, 0, g, 0))
    return pl.pallas_call(
        _compress_kernel,
        grid_spec=pltpu.PrefetchScalarGridSpec(
            num_scalar_prefetch=1,
            grid=(n_seq, n_groups),
            in_specs=in_specs,
            out_specs=[out_spec, out_spec],
            scratch_shapes=[pltpu.VMEM((4, p * LANES, LANES), F32), pltpu.VMEM((4, 8, LANES), F32),
                            pltpu.VMEM((2, 8, LANES), F32)],
        ),
        out_shape=[jax.ShapeDtypeStruct((n_seq, N_KV, n_rows, HEAD_DIM), BF16)] * 2,
        compiler_params=_cparams(("arbitrary", "arbitrary")),
        name="compress_mlp",
    )(page_ids, *([src] * p), wbd, w1, pe, b1, w2bd, kg0)


SEL_CHUNK = 512
WIN_KEYS = 640


def _attn_kernel(q_ref, gate_ref, kc_ref, vc_ref, kst_ref, vst_ref, kwt_ref, vwt_ref, agg_ref, exp_ref,
                 o_ref, bias_scr, *, tq, t_len):
    i = pl.program_id(2)
    q0 = i * tq
    m_rows = GQA * tq
    n_sel = t_len // SEL_BLOCK
    q = q_ref[...]
    qs = jnp.concatenate([q[:, g * HEAD_DIM:(g + 1) * HEAD_DIM] for g in range(GQA)], axis=0)
    row = lax.broadcasted_iota(jnp.int32, (m_rows, 1), 0)
    qpos = q0 + (row % tq)
    qp = q0 + lax.broadcasted_iota(jnp.int32, (tq, 1), 0)

    s_c = _dot_nt(qs, kc_ref[...])
    r_i = lax.broadcasted_iota(jnp.int32, (1, s_c.shape[-1]), 1)
    mask_c = (r_i >= 1) & (CMP_STRIDE * r_i + (CMP_STRIDE - 1) <= qpos)
    p_c = _masked_softmax(s_c, mask_c)
    o_c = _dot(p_c.astype(BF16), vc_ref[...])
    imp = p_c[0:tq] + p_c[tq:2 * tq] + p_c[2 * tq:3 * tq] + p_c[3 * tq:4 * tq]
    imp_s = _split3_dot(imp, agg_ref[...])

    sidx = lax.broadcasted_iota(jnp.int32, (1, LANES), 1)
    cur = qp // SEL_BLOCK
    forced = (sidx == 0) | (sidx == cur) | (sidx == cur - 1)
    causal = sidx * SEL_BLOCK <= qp
    score = jnp.where(causal, imp_s + jnp.where(forced, FORCE_BONUS, 0.0), NEG_INF)
    sv = score.T[:n_sel]
    srow = lax.broadcasted_iota(jnp.int32, (n_sel, 1), 0)
    cnt = jnp.zeros((n_sel, tq), jnp.int32)
    for t in range(n_sel):
        other = sv[t:t + 1, :]
        beats = (other > sv) | ((other == sv) & (t < srow))
        cnt = cnt + beats.astype(jnp.int32)
    sel_t = jnp.where((cnt < min(N_SEL, n_sel)) & (sv > NEG_INF / 2), 1.0, 0.0)
    sel = jnp.concatenate([sel_t, jnp.zeros((LANES - n_sel, tq), F32)], axis=0).T
    selm = _dot(sel.astype(BF16), exp_ref[...])
    kpos_all = lax.broadcasted_iota(jnp.int32, (1, t_len), 1)
    bias_scr[...] = jnp.where((selm > 0.5) & (kpos_all <= qp), 0.0, NEG_INF)

    n_chunks = (q0 + tq + SEL_CHUNK - 1) // SEL_CHUNK

    def body(c, carry):
        m_old, l_old, acc = carry
        off = pl.multiple_of(c * SEL_CHUNK, SEL_CHUNK)
        kt = kst_ref[:, pl.ds(off, SEL_CHUNK)].astype(BF16)
        s = _dot(qs, kt).reshape(GQA, tq, SEL_CHUNK) + bias_scr[:, pl.ds(off, SEL_CHUNK)][None]
        s = s.reshape(m_rows, SEL_CHUNK)
        m_new = jnp.maximum(m_old, jnp.max(s, axis=-1, keepdims=True))
        alpha = jnp.exp(m_old - m_new)
        pr = jnp.exp(s - m_new)
        l_new = alpha * l_old + jnp.sum(pr, axis=-1, keepdims=True)
        vt = vst_ref[:, pl.ds(off, SEL_CHUNK)].astype(BF16)
        acc = alpha * acc + _dot_nt(pr.astype(BF16), vt)
        return m_new, l_new, acc

    init = (jnp.full((m_rows, 1), NEG_INF, F32), jnp.zeros((m_rows, 1), F32), jnp.zeros((m_rows, HEAD_DIM), F32))
    _, l_s, acc_s = lax.fori_loop(0, n_chunks, body, init)
    o_s = acc_s / jnp.maximum(l_s, 1e-30)

    start = pl.multiple_of(jnp.clip(q0 - WINDOW, 0, t_len - WIN_KEYS), LANES)
    dist = qp - (start + lax.broadcasted_iota(jnp.int32, (1, WIN_KEYS), 1))
    bias_w = jnp.where((dist >= 0) & (dist < WINDOW), 0.0, NEG_INF)
    s_w = _dot(qs, kwt_ref[:, pl.ds(start, WIN_KEYS)].astype(BF16)).reshape(GQA, tq, WIN_KEYS) + bias_w[None]
    s_w = s_w.reshape(m_rows, WIN_KEYS)
    e_w = jnp.exp(s_w - jnp.max(s_w, axis=-1, keepdims=True))
    p_w = e_w / jnp.maximum(jnp.sum(e_w, axis=-1, keepdims=True), 1e-30)
    o_w = _dot_nt(p_w.astype(BF16), vwt_ref[:, pl.ds(start, WIN_KEYS)].astype(BF16))

    gate = gate_ref[...]

    def gcol(jb):
        return jnp.concatenate([jnp.broadcast_to(gate[:, g * 3 + jb:g * 3 + jb + 1], (tq, HEAD_DIM))
                                for g in range(GQA)], axis=0)

    o = gcol(0) * o_c + gcol(1) * o_s + gcol(2) * o_w
    o_ref[...] = jnp.concatenate([o[g * tq:(g + 1) * tq] for g in range(GQA)], axis=-1)


def _prompt_attention(qn, gate_hm, kc, vc, kvt, agg, expand, *, tq):
    n = qn.shape[0]
    n_seq, _, t_len = kvt.shape
    tiles = t_len // tq
    kern = functools.partial(_attn_kernel, tq=tq, t_len=t_len)

    def kv_spec(block_row):
        return pl.BlockSpec((None, HEAD_DIM, t_len), lambda b, h, i: (b, block_row + h, 0))

    sel_k, sel_v = (2 * KV_DIM) // HEAD_DIM, (3 * KV_DIM) // HEAD_DIM
    win_k, win_v = (4 * KV_DIM) // HEAD_DIM, (5 * KV_DIM) // HEAD_DIM
    return pl.pallas_call(
        kern,
        grid=(n_seq, N_KV, tiles),
        in_specs=[
            pl.BlockSpec((tq, GQA * HEAD_DIM), lambda b, h, i: (b * tiles + i, h)),
            pl.BlockSpec((None, tq, 3 * GQA), lambda b, h, i: (h, b * tiles + i, 0)),
            pl.BlockSpec((None, None, kc.shape[2], HEAD_DIM), lambda b, h, i: (b, h, 0, 0)),
            pl.BlockSpec((None, None, vc.shape[2], HEAD_DIM), lambda b, h, i: (b, h, 0, 0)),
            kv_spec(sel_k), kv_spec(sel_v), kv_spec(win_k), kv_spec(win_v),
            pl.BlockSpec(agg.shape, lambda b, h, i: (0, 0)),
            pl.BlockSpec(expand.shape, lambda b, h, i: (0, 0)),
        ],
        out_specs=pl.BlockSpec((tq, GQA * HEAD_DIM), lambda b, h, i: (b * tiles + i, h)),
        out_shape=jax.ShapeDtypeStruct((n, ATTN_DIM), F32),
        scratch_shapes=[pltpu.VMEM((tq, t_len), F32)],
        compiler_params=_cparams(("arbitrary", "arbitrary", "arbitrary")),
        name="prompt_attention",
    )(qn, gate_hm, kc, vc, kvt, kvt, kvt, kvt, agg, expand)


def _sample_select_kernel(qs_ref, kc_ref, vc_ref, agg_ref, oc_ref, idx_ref, *, q0, n_tok, n_sel, n_pick):
    rows = GQA * n_tok
    row = lax.broadcasted_iota(jnp.int32, (rows, 1), 0)
    qpos = q0 + (row % n_tok)
    imps = []
    for h in range(N_KV):
        s_c = _dot_nt(qs_ref[h].astype(BF16), kc_ref[h])
        r_i = lax.broadcasted_iota(jnp.int32, (1, s_c.shape[-1]), 1)
        mask_c = (r_i >= 1) & (CMP_STRIDE * r_i + (CMP_STRIDE - 1) <= qpos)
        p_c = _masked_softmax(s_c, mask_c)
        oc_ref[h] = _dot(p_c.astype(BF16), vc_ref[h])
        imp = p_c[0:n_tok]
        for g in range(1, GQA):
            imp = imp + p_c[g * n_tok:(g + 1) * n_tok]
        imps.append(imp)
    imp_all = jnp.concatenate(imps, axis=0)
    imp_s = _split3_dot(imp_all, agg_ref[...])
    width = imp_s.shape[-1]
    sidx = lax.broadcasted_iota(jnp.int32, (1, width), 1)
    r2 = lax.broadcasted_iota(jnp.int32, (N_KV * n_tok, 1), 0)
    qp = q0 + (r2 % n_tok)
    cur = qp // SEL_BLOCK
    forced = (sidx == 0) | (sidx == cur) | (sidx == cur - 1)
    causal = sidx * SEL_BLOCK <= qp
    score = jnp.where(causal, imp_s + jnp.where(forced, FORCE_BONUS, 0.0), NEG_INF)
    lowest = -3.0e38
    score = jnp.where((sidx >= n_sel - 1), lowest, score)
    out_lane = lax.broadcasted_iota(jnp.int32, (1, LANES), 1)
    picked = jnp.zeros((N_KV * n_tok, LANES), jnp.int32)
    for it in range(n_pick):
        mx = jnp.max(score, axis=-1, keepdims=True)
        first = jnp.min(jnp.where(score == mx, sidx, width), axis=-1, keepdims=True)
        picked = jnp.where(out_lane == it, first, picked)
        score = jnp.where(sidx == first, lowest, score)
    idx_ref[...] = picked


def _sample_select(qs, kc, vc, agg, *, q0, n_tok, n_sel):
    n_seq = qs.shape[0]
    n_pick = N_SEL - 1
    kern = functools.partial(_sample_select_kernel, q0=q0, n_tok=n_tok, n_sel=n_sel, n_pick=n_pick)
    return pl.pallas_call(
        kern,
        grid=(n_seq,),
        in_specs=[
            pl.BlockSpec((None, N_KV, GQA * n_tok, HEAD_DIM), lambda b: (b, 0, 0, 0)),
            pl.BlockSpec((None, N_KV, kc.shape[2], HEAD_DIM), lambda b: (b, 0, 0, 0)),
            pl.BlockSpec((None, N_KV, vc.shape[2], HEAD_DIM), lambda b: (b, 0, 0, 0)),
            pl.BlockSpec(agg.shape, lambda b: (0, 0)),
        ],
        out_specs=[
            pl.BlockSpec((None, N_KV, GQA * n_tok, HEAD_DIM), lambda b: (b, 0, 0, 0)),
            pl.BlockSpec((None, N_KV * n_tok, LANES), lambda b: (b, 0, 0)),
        ],
        out_shape=[
            jax.ShapeDtypeStruct((n_seq, N_KV, GQA * n_tok, HEAD_DIM), F32),
            jax.ShapeDtypeStruct((n_seq, N_KV * n_tok, LANES), jnp.int32),
        ],
        compiler_params=_cparams(("arbitrary",)),
        name="sample_select",
    )(qs, kc, vc, agg)


def _sample_attend_kernel(pg_ref, half_ref, q_ref, oc_ref, gate_ref, knew_ref, vnew_ref, kwp_ref, vwp_ref,
                          kwn_ref, vwn_ref, *refs, n_tok, n_pick):
    k_pages = refs[:n_pick]
    v_pages = refs[n_pick:2 * n_pick]
    o_ref = refs[2 * n_pick]
    b = pl.program_id(0)
    h = pl.program_id(1)
    t = pl.program_id(2)
    q = q_ref[...].astype(BF16)
    lane = lax.broadcasted_iota(jnp.int32, (1, LANES), 1)
    n_new = knew_ref.shape[-1]
    new_lane = lax.broadcasted_iota(jnp.int32, (1, n_new), 1)
    new_mask = (new_lane // n_tok == b) & (new_lane % n_tok <= t)

    flat = ((b * N_KV + h) * n_tok + t) * n_pick
    scores, masks = [], []
    for k in range(n_pick):
        half = half_ref[flat + k]
        scores.append(_dot(q, k_pages[k][...].astype(BF16)))
        masks.append(lane // SEL_BLOCK == half)
    s_new = _dot(q, knew_ref[...].astype(BF16))
    mx = jnp.max(jnp.where(new_mask, s_new, NEG_INF), axis=-1, keepdims=True)
    for s, mk in zip(scores, masks):
        mx = jnp.maximum(mx, jnp.max(jnp.where(mk, s, NEG_INF), axis=-1, keepdims=True))
    e_new = jnp.exp(jnp.where(new_mask, s_new, NEG_INF) - mx) * new_mask.astype(F32)
    den = jnp.sum(e_new, axis=-1, keepdims=True)
    acc = _dot_nt(e_new.astype(BF16), vnew_ref[...].astype(BF16))
    for k in range(n_pick):
        e = jnp.exp(jnp.where(masks[k], scores[k], NEG_INF) - mx) * masks[k].astype(F32)
        den = den + jnp.sum(e, axis=-1, keepdims=True)
        acc = acc + _dot_nt(e.astype(BF16), v_pages[k][...].astype(BF16))
    o_s = acc / jnp.maximum(den, 1e-30)

    n_win = kwp_ref.shape[-1]
    wl = lax.broadcasted_iota(jnp.int32, (1, n_win), 1)
    past_mask = (n_win - wl + t) < WINDOW
    s_p = jnp.where(past_mask, _dot(q, kwp_ref[...].astype(BF16)), NEG_INF)
    s_n = jnp.where(new_mask, _dot(q, kwn_ref[...].astype(BF16)), NEG_INF)
    mw = jnp.maximum(jnp.max(s_p, axis=-1, keepdims=True), jnp.max(s_n, axis=-1, keepdims=True))
    e_p = jnp.exp(s_p - mw) * past_mask.astype(F32)
    e_n = jnp.exp(s_n - mw) * new_mask.astype(F32)
    den_w = jnp.sum(e_p, axis=-1, keepdims=True) + jnp.sum(e_n, axis=-1, keepdims=True)
    acc_w = _dot_nt(e_p.astype(BF16), vwp_ref[...].astype(BF16)) + _dot_nt(e_n.astype(BF16), vwn_ref[...].astype(BF16))
    o_w = acc_w / jnp.maximum(den_w, 1e-30)

    gate = gate_ref[...]
    o_ref[...] = gate[:, 0:1] * oc_ref[...] + gate[:, 1:2] * o_s + gate[:, 2:3] * o_w


def _sample_attend(page_of, half_of, q8, oc8, gate8, kvt_new, cache_sel_t, cache_win_t, *, n_tok, n_pick, layer):
    n_seq = q8.shape[0]
    kern = functools.partial(_sample_attend_kernel, n_tok=n_tok, n_pick=n_pick)
    n_new = kvt_new.shape[-1]
    n_win = cache_win_t.shape[-1]

    def qspec(last):
        return pl.BlockSpec((None, None, None, 8, last), lambda b, h, t, pg, hf: (b, h, t, 0, 0))

    def new_spec(block_row):
        return pl.BlockSpec((None, HEAD_DIM, n_new), lambda b, h, t, pg, hf: (0, block_row + h, 0))

    def win_spec(block_row):
        return pl.BlockSpec((None, None, HEAD_DIM, n_win), lambda b, h, t, pg, hf: (layer, b, block_row + h, 0))

    def page_spec(k, block_row):
        def imap(b, h, t, pg, hf):
            return (layer, pg[((b * N_KV + h) * n_tok + t) * n_pick + k], block_row + h, 0)
        return pl.BlockSpec((None, None, HEAD_DIM, LANES), imap)

    sel_k, sel_v = (2 * KV_DIM) // HEAD_DIM, (3 * KV_DIM) // HEAD_DIM
    win_k, win_v = (4 * KV_DIM) // HEAD_DIM, (5 * KV_DIM) // HEAD_DIM
    in_specs = ([qspec(HEAD_DIM), qspec(HEAD_DIM), qspec(3), new_spec(sel_k), new_spec(sel_v),
                 win_spec(0), win_spec(N_KV), new_spec(win_k), new_spec(win_v)]
                + [page_spec(k, 0) for k in range(n_pick)] + [page_spec(k, N_KV) for k in range(n_pick)])
    return pl.pallas_call(
        kern,
        grid_spec=pltpu.PrefetchScalarGridSpec(
            num_scalar_prefetch=2,
            grid=(n_seq, N_KV, n_tok),
            in_specs=in_specs,
            out_specs=qspec(HEAD_DIM),
        ),
        out_shape=jax.ShapeDtypeStruct(q8.shape, F32),
        compiler_params=_cparams(("arbitrary", "arbitrary", "arbitrary")),
        name="sample_attend",
    )(page_of, half_of, q8, oc8, gate8, kvt_new, kvt_new, cache_win_t, cache_win_t, kvt_new, kvt_new,
      *([cache_sel_t] * (2 * n_pick)))


def _mix_prologue(attn, u, u1, u2, bg, cw_ref, cb_ref, og_ref, o_scr):
    og = og_ref[...]
    an = attn * lax.rsqrt(jnp.mean(attn * attn, axis=-1, keepdims=True) + EPS) * og[:, :ATTN_DIM]
    yv = cb_ref[...] + cw_ref[0:1, :] * u2
    yv = yv + cw_ref[1:2, :] * u1
    yv = yv + cw_ref[2:3, :] * u
    cv = bg * yv
    cn = cv * lax.rsqrt(jnp.mean(cv * cv, axis=-1, keepdims=True) + EPS) * og[:, ATTN_DIM:]
    o_scr[:, :ATTN_DIM] = an.astype(BF16)
    o_scr[:, ATTN_DIM:] = cn.astype(BF16)


def _outproj_seq_kernel(attn_ref, u_ref, halo_ref, bg_ref, cw_ref, cb_ref, og_ref, w_ref, y_ref, gm_ref,
                        o_ref, o_scr, *, tiles_per_seq):
    i = pl.program_id(0)

    @pl.when(pl.program_id(1) == 0)
    def _():
        u = u_ref[...]
        tm = u.shape[0]
        row = lax.broadcasted_iota(jnp.int32, (tm, 1), 0)
        halo = jnp.where(i % tiles_per_seq == 0, 0.0, halo_ref[...])
        p1 = halo[7:8, :]
        p2 = halo[6:7, :]
        u1 = jnp.where(row == 0, p1, pltpu.roll(u, 1, axis=0))
        u2 = jnp.where(row == 0, p2, jnp.where(row == 1, p1, pltpu.roll(u, 2, axis=0)))
        _mix_prologue(attn_ref[...], u, u1, u2, bg_ref[...], cw_ref, cb_ref, og_ref, o_scr)

    o_ref[...] = y_ref[...] + gm_ref[...] * _dot(o_scr[...], w_ref[...])


def _outproj_state_kernel(attn_ref, u_ref, e1_ref, e2_ref, bg_ref, cw_ref, cb_ref, og_ref, w_ref, y_ref, gm_ref,
                          o_ref, o_scr, *, n_tok):
    @pl.when(pl.program_id(1) == 0)
    def _():
        u = u_ref[...]
        tm = u.shape[0]
        k = lax.broadcasted_iota(jnp.int32, (tm, 1), 0) % n_tok
        u1 = jnp.where(k >= 1, pltpu.roll(u, 1, axis=0), e1_ref[...])
        u2 = jnp.where(k >= 2, pltpu.roll(u, 2, axis=0), e2_ref[...])
        _mix_prologue(attn_ref[...], u, u1, u2, bg_ref[...], cw_ref, cb_ref, og_ref, o_scr)

    o_ref[...] = y_ref[...] + gm_ref[...] * _dot(o_scr[...], w_ref[...])


def _outproj(attn, u, bg, cw, cb, og, w, y, gmod, *, tm, n_seq, layer, ext=None, n_tok=None):
    n, d = y.shape
    tn = 512
    tiles_per_seq = (n // n_seq) // tm
    tiles_per_mod = (n // tm) // gmod.shape[0]
    r = gmod.shape[1]
    full = lambda i, j: (i, 0)
    c0 = lambda i, j: (0, 0)
    common = [pl.BlockSpec((tm, CONV_DIM), full), pl.BlockSpec((3, CONV_DIM), c0), pl.BlockSpec((1, CONV_DIM), c0),
              pl.BlockSpec((1, d), c0), pl.BlockSpec((None, d, tn), lambda i, j: (layer, 0, j)),
              pl.BlockSpec((tm, tn), lambda i, j: (i, j)),
              pl.BlockSpec((None, r, tn), lambda i, j: (i // tiles_per_mod, 0, j))]
    head = [pl.BlockSpec((tm, ATTN_DIM), full), pl.BlockSpec((tm, CONV_DIM), full)]
    if ext is None:
        kern = functools.partial(_outproj_seq_kernel, tiles_per_seq=tiles_per_seq)
        extra = [pl.BlockSpec((8, CONV_DIM), lambda i, j: (jnp.maximum(i * (tm // 8) - 1, 0), 0))]
        args = (attn, u, u, bg, cw, cb, og, w, y, gmod)
    else:
        kern = functools.partial(_outproj_state_kernel, n_tok=n_tok)
        extra = [pl.BlockSpec((tm, CONV_DIM), full), pl.BlockSpec((tm, CONV_DIM), full)]
        args = (attn, u, ext[0], ext[1], bg, cw, cb, og, w, y, gmod)
    return pl.pallas_call(
        kern,
        grid=(n // tm, d // tn),
        in_specs=head + extra + common,
        out_specs=pl.BlockSpec((tm, tn), lambda i, j: (i, j)),
        out_shape=jax.ShapeDtypeStruct((n, d), F32),
        scratch_shapes=[pltpu.VMEM((tm, d), BF16)],
        compiler_params=_cparams(("arbitrary", "arbitrary")),
        name="output_projection",
    )(*args)


def _ffn_kernel(y_ref, sh_ref, sc_ref, gm_ref, g2_ref, w1_ref, w3_ref, w2_ref, o_ref, h_scr, acc):
    j = pl.program_id(1)

    @pl.when(j == 0)
    def _():
        h_scr[...] = _rms_modulate(y_ref[...], g2_ref[...], sh_ref[...], sc_ref[...]).astype(BF16)
        acc[...] = jnp.zeros_like(acc)

    h = h_scr[...]
    hid = _silu(_dot(h, w1_ref[...])) * _dot(h, w3_ref[...])
    acc[...] += _dot(hid.astype(BF16), w2_ref[...])

    @pl.when(j == pl.num_programs(1) - 1)
    def _():
        o_ref[...] = y_ref[...] + gm_ref[...] * acc[...]


def _dense_ffn(y, shift, scale, gmod, g2, w1, w3, w2, *, tm, layer):
    n, d = y.shape
    f = w1.shape[-1]
    tf = 512
    tiles_per_mod = (n // tm) // shift.shape[0]
    r = shift.shape[1]
    mod = pl.BlockSpec((None, r, d), lambda i, j: (i // tiles_per_mod, 0, 0))
    return pl.pallas_call(
        _ffn_kernel,
        grid=(n // tm, f // tf),
        in_specs=[pl.BlockSpec((tm, d), lambda i, j: (i, 0)), mod, mod, mod,
                  pl.BlockSpec((1, d), lambda i, j: (0, 0)),
                  pl.BlockSpec((None, d, tf), lambda i, j: (layer, 0, j)),
                  pl.BlockSpec((None, d, tf), lambda i, j: (layer, 0, j)),
                  pl.BlockSpec((None, tf, d), lambda i, j: (layer, j, 0))],
        out_specs=pl.BlockSpec((tm, d), lambda i, j: (i, 0)),
        out_shape=jax.ShapeDtypeStruct((n, d), F32),
        scratch_shapes=[pltpu.VMEM((tm, d), BF16), pltpu.VMEM((tm, d), F32)],
        compiler_params=_cparams(("arbitrary", "arbitrary")),
        name="dense_ffn",
    )(y, shift, scale, gmod, g2, w1, w3, w2)


def _top2_route(h, h_bf, rw):
    lane = lax.broadcasted_iota(jnp.int32, (1, LANES), 1)
    rw_hi = rw.astype(BF16)
    rw_lo = (rw - rw_hi.astype(F32)).astype(BF16)
    h_lo = (h - h_bf.astype(F32)).astype(BF16)
    logits = _dot_nt(h_bf, rw_hi) + _dot_nt(h_lo, rw_hi) + _dot_nt(h_bf, rw_lo)
    lowest = -3.0e38
    logits = jnp.where(lane < N_EXPERTS, logits, lowest)
    m1 = jnp.max(logits, axis=-1, keepdims=True)
    i1 = jnp.min(jnp.where(logits == m1, lane, LANES), axis=-1, keepdims=True)
    rest = jnp.where(lane == i1, lowest, logits)
    m2 = jnp.max(rest, axis=-1, keepdims=True)
    i2 = jnp.min(jnp.where(rest == m2, lane, LANES), axis=-1, keepdims=True)
    e2 = jnp.exp(m2 - m1)
    den = 1.0 + e2
    return i1, i2, 1.0 / den, e2 / den


def _moe_kernel(y_ref, sh_ref, sc_ref, gm_ref, g2_ref, rw_ref, w1_ref, w3_ref, w2_ref, o_ref, h_scr, acc, comb):
    e = pl.program_id(1)
    j = pl.program_id(2)
    lane = lax.broadcasted_iota(jnp.int32, (1, LANES), 1)

    @pl.when((e == 0) & (j == 0))
    def _():
        h = _rms_modulate(y_ref[...], g2_ref[...], sh_ref[...], sc_ref[...])
        h_scr[...] = h.astype(BF16)
        acc[...] = jnp.zeros_like(acc)
        i1, i2, w1, w2 = _top2_route(h, h_scr[...], rw_ref[...])
        comb[...] = jnp.where(lane == i1, w1, 0.0) + jnp.where(lane == i2, w2, 0.0)

    h = h_scr[...]
    hid = _silu(_dot(h, w1_ref[...])) * _dot(h, w3_ref[...])
    ce = jnp.sum(jnp.where(lane == e, comb[...], 0.0), axis=-1, keepdims=True)
    acc[...] += ce * _dot(hid.astype(BF16), w2_ref[...])

    @pl.when((e == pl.num_programs(1) - 1) & (j == pl.num_programs(2) - 1))
    def _():
        o_ref[...] = y_ref[...] + gm_ref[...] * acc[...]


def _moe_ffn(y, shift, scale, gmod, g2, rw, w1, w3, w2, *, tm, layer):
    n, d = y.shape
    _, n_e, _, fe = w1.shape
    tf = 256
    tiles_per_mod = (n // tm) // shift.shape[0]
    r = shift.shape[1]
    mod = pl.BlockSpec((None, r, d), lambda i, e, j: (i // tiles_per_mod, 0, 0))
    return pl.pallas_call(
        _moe_kernel,
        grid=(n // tm, n_e, fe // tf),
        in_specs=[pl.BlockSpec((tm, d), lambda i, e, j: (i, 0)), mod, mod, mod,
                  pl.BlockSpec((1, d), lambda i, e, j: (0, 0)),
                  pl.BlockSpec((LANES, d), lambda i, e, j: (0, 0)),
                  pl.BlockSpec((None, None, d, tf), lambda i, e, j: (layer, e, 0, j)),
                  pl.BlockSpec((None, None, d, tf), lambda i, e, j: (layer, e, 0, j)),
                  pl.BlockSpec((None, None, tf, d), lambda i, e, j: (layer, e, j, 0))],
        out_specs=pl.BlockSpec((tm, d), lambda i, e, j: (i, 0)),
        out_shape=jax.ShapeDtypeStruct((n, d), F32),
        scratch_shapes=[pltpu.VMEM((tm, d), BF16), pltpu.VMEM((tm, d), F32), pltpu.VMEM((tm, LANES), F32)],
        compiler_params=_cparams(("arbitrary", "arbitrary", "arbitrary")),
        name="moe_ffn",
    )(y, shift, scale, gmod, g2, rw, w1, w3, w2)


MOE_TILE = 512
MOE_FT = 256


def _router_kernel(y_ref, sh_ref, sc_ref, g2_ref, rw_ref, h_ref, route_ref):
    h = _rms_modulate(y_ref[...], g2_ref[...], sh_ref[...], sc_ref[...])
    h_ref[...] = h
    i1, i2, w1, w2 = _top2_route(h, h.astype(BF16), rw_ref[...])
    lane = lax.broadcasted_iota(jnp.int32, (1, LANES), 1)
    route_ref[...] = (jnp.where(lane == 0, i1.astype(F32), 0.0) + jnp.where(lane == 1, i2.astype(F32), 0.0)
                      + jnp.where(lane == 2, w1, 0.0) + jnp.where(lane == 3, w2, 0.0))


def _router(y, shift, scale, g2, rw, *, tm):
    n, d = y.shape
    tiles_per_mod = (n // tm) // shift.shape[0]
    r = shift.shape[1]
    mod = pl.BlockSpec((None, r, d), lambda i: (i // tiles_per_mod, 0, 0))
    return pl.pallas_call(
        _router_kernel,
        grid=(n // tm,),
        in_specs=[pl.BlockSpec((tm, d), lambda i: (i, 0)), mod, mod,
                  pl.BlockSpec((1, d), lambda i: (0, 0)), pl.BlockSpec((LANES, d), lambda i: (0, 0))],
        out_specs=[pl.BlockSpec((tm, d), lambda i: (i, 0)), pl.BlockSpec((tm, LANES), lambda i: (i, 0))],
        out_shape=[jax.ShapeDtypeStruct((n, d), F32), jax.ShapeDtypeStruct((n, LANES), F32)],
        compiler_params=_cparams(("arbitrary",)),
        name="moe_router",
    )(y, shift, scale, g2, rw)


def _moe_plan(route, n_tiles):
    n = route.shape[0]
    flat_e = route[:, :2].astype(jnp.int32).reshape(-1)
    onehot = (flat_e[:, None] == jnp.arange(N_EXPERTS, dtype=jnp.int32)[None, :]).astype(jnp.int32)
    csum = jnp.cumsum(onehot, axis=0)
    pos = jnp.sum((csum - 1) * onehot, axis=1)
    sizes = csum[-1]
    padded = ((sizes + MOE_TILE - 1) // MOE_TILE) * MOE_TILE
    ends = jnp.cumsum(padded)
    dest = (ends - padded)[flat_e] + pos
    rows = n_tiles * MOE_TILE
    a = jnp.arange(2 * n, dtype=jnp.int32)
    src = jnp.zeros((rows,), jnp.int32).at[dest].set(a // 2, unique_indices=True)
    dst = jnp.full((rows,), -1, jnp.int32).at[dest].set((a % 2) * n + a // 2, unique_indices=True)
    wt = jnp.zeros((rows,), F32).at[dest].set(route[:, 2:4].reshape(-1), unique_indices=True)
    tile_start = jnp.arange(n_tiles, dtype=jnp.int32) * MOE_TILE
    te = jnp.sum((tile_start[:, None] >= ends[None, :]).astype(jnp.int32), axis=1)
    tv = (te < N_EXPERTS).astype(jnp.int32)
    return jnp.minimum(te, N_EXPERTS - 1), tv, src, dst, wt.reshape(rows, 1)


def _row_copy(src_ref, src_row, dst_ref, dst_row, sem):
    return pltpu.make_async_copy(src_ref.at[pl.ds(src_row, 1)], dst_ref.at[pl.ds(dst_row, 1)], sem)


ROW_LOOP_UNROLL = 8


def _moe_group_kernel(te_ref, tv_ref, src_ref, dst_ref, h_hbm, wt_ref, w1_ref, w3_ref, w2_ref, out_hbm,
                      xbuf, hb, acc, obuf, gsem, ssem):
    del te_ref
    i = pl.program_id(0)
    j = pl.program_id(1)
    n_t = pl.num_programs(0)
    n_j = pl.num_programs(1)
    valid = tv_ref[i] == 1
    slot = i % 2

    def row_loop(body):
        def step(r, c):
            body(r)
            return c
        lax.fori_loop(0, MOE_TILE, step, 0, unroll=ROW_LOOP_UNROLL)

    def gather_start(tile, s):
        row_loop(lambda r: _row_copy(h_hbm, src_ref[tile * MOE_TILE + r], xbuf.at[s], r, gsem.at[s]).start())

    def gather_wait(s):
        row_loop(lambda r: _row_copy(h_hbm, 0, xbuf.at[s], r, gsem.at[s]).wait())

    def scatter(tile, s, start):
        def body(r):
            row = dst_ref[tile * MOE_TILE + r]

            @pl.when(row >= 0)
            def _():
                cp = _row_copy(obuf.at[s], r, out_hbm, jnp.maximum(row, 0), ssem.at[s])
                if start:
                    cp.start()
                else:
                    cp.wait()
        row_loop(body)

    @pl.when(valid & (j == 0))
    def _():
        @pl.when(i == 0)
        def _():
            gather_start(0, 0)

        gather_wait(slot)
        nxt = jnp.minimum(i + 1, n_t - 1)

        @pl.when((i + 1 < n_t) & (tv_ref[nxt] == 1))
        def _():
            gather_start(i + 1, 1 - slot)

        hb[...] = xbuf[slot].astype(BF16)
        acc[...] = jnp.zeros_like(acc)

    @pl.when(valid)
    def _():
        h = hb[...]
        hid = _silu(_dot(h, w1_ref[...])) * _dot(h, w3_ref[...])
        acc[...] += _dot(hid.astype(BF16), w2_ref[...])

    @pl.when(valid & (j == n_j - 1))
    def _():
        obuf[slot] = wt_ref[...] * acc[...]
        scatter(i, slot, True)

        @pl.when(i >= 1)
        def _():
            scatter(i - 1, 1 - slot, False)

        @pl.when(i == n_t - 1)
        def _():
            scatter(i, slot, False)

    prev = jnp.maximum(i - 1, 0)

    @pl.when(jnp.logical_not(valid) & (j == 0) & (i >= 1) & (tv_ref[prev] == 1))
    def _():
        scatter(prev, 1 - slot, False)


def _moe_grouped(h, plan, w1, w3, w2, *, layer):
    te, tv, src, dst, wt = plan
    n, d = h.shape
    fe = w1.shape[-1]
    n_tiles = te.shape[0]
    n_f = fe // MOE_FT

    def w13(i, j, te_r, tv_r, s_r, d_r):
        return (layer, te_r[i], 0, j * tv_r[i])

    def w2m(i, j, te_r, tv_r, s_r, d_r):
        return (layer, te_r[i], j * tv_r[i], 0)

    return pl.pallas_call(
        _moe_group_kernel,
        grid_spec=pltpu.PrefetchScalarGridSpec(
            num_scalar_prefetch=4,
            grid=(n_tiles, n_f),
            in_specs=[pl.BlockSpec(memory_space=pl.ANY),
                      pl.BlockSpec((MOE_TILE, 1), lambda i, j, *_: (i, 0)),
                      pl.BlockSpec((None, None, d, MOE_FT), w13),
                      pl.BlockSpec((None, None, d, MOE_FT), w13),
                      pl.BlockSpec((None, None, MOE_FT, d), w2m)],
            out_specs=pl.BlockSpec(memory_space=pl.ANY),
            scratch_shapes=[pltpu.VMEM((2, MOE_TILE, d), F32), pltpu.VMEM((MOE_TILE, d), BF16),
                            pltpu.VMEM((MOE_TILE, d), F32), pltpu.VMEM((2, MOE_TILE, d), F32),
                            pltpu.SemaphoreType.DMA((2,)), pltpu.SemaphoreType.DMA((2,))],
        ),
        out_shape=jax.ShapeDtypeStruct((2 * n, d), F32),
        compiler_params=_cparams(("arbitrary", "arbitrary")),
        name="moe_grouped_ffn",
    )(te, tv, src, dst, h, wt, w1, w3, w2)


def _moe_combine_kernel(y_ref, gm_ref, a_ref, b_ref, o_ref):
    o_ref[...] = y_ref[...] + gm_ref[...] * (a_ref[...] + b_ref[...])


def _moe_combine(y, gmod, out01, *, tm):
    n, d = y.shape
    tiles = n // tm
    tiles_per_mod = tiles // gmod.shape[0]
    r = gmod.shape[1]
    blk = pl.BlockSpec((tm, d), lambda i: (i, 0))
    return pl.pallas_call(
        _moe_combine_kernel,
        grid=(tiles,),
        in_specs=[blk, pl.BlockSpec((None, r, d), lambda i: (i // tiles_per_mod, 0, 0)), blk,
                  pl.BlockSpec((tm, d), lambda i: (i + tiles, 0))],
        out_specs=blk,
        out_shape=jax.ShapeDtypeStruct((n, d), F32),
        compiler_params=_cparams(("arbitrary",)),
        name="moe_combine",
    )(y, gmod, out01, out01)


def _agg_matrix(n_rows, n_cmp, n_sel, width):
    a = np.zeros((n_rows, width), np.float32)
    ratio = SEL_BLOCK // CMP_STRIDE
    for s in range(n_sel):
        for tkn in range(s * ratio - 1, s * ratio + ratio):
            if 0 <= tkn < n_cmp and tkn + 1 < n_rows:
                a[tkn + 1, s] = 1.0
    return jnp.asarray(a, BF16)


def _expand_matrix(t_len):
    e = np.zeros((LANES, t_len), np.float32)
    for s in range(t_len // SEL_BLOCK):
        e[s, s * SEL_BLOCK:(s + 1) * SEL_BLOCK] = 1.0
    return jnp.asarray(e, BF16)


def _compress_weights(l, k_gain, cmp_w1, cmp_b1, cmp_w2, cmp_pe):
    w1 = cmp_w1[l]
    half = CMP_STRIDE * HEAD_DIM
    wa = w1[:, :half].reshape(2, CMP_STRIDE, HEAD_DIM, HEAD_DIM)
    wb = w1[:, half:].reshape(2, CMP_STRIDE, HEAD_DIM, HEAD_DIM)
    z = jnp.zeros_like(wa)
    top = jnp.concatenate([wa, z, wb, z], axis=-1)
    bot = jnp.concatenate([z, wa, z, wb], axis=-1)
    wbd = jnp.concatenate([top, bot], axis=-2).astype(BF16)
    w2 = cmp_w2[l]
    z2 = jnp.zeros_like(w2)
    w2bd = jnp.concatenate([jnp.concatenate([w2, z2], -1), jnp.concatenate([z2, w2], -1)], -2).astype(BF16)
    pe = cmp_pe[l].reshape(2, 2 * half, 1)
    b1 = cmp_b1[l].reshape(2, 1, HEAD_DIM)
    kg0 = jnp.tile(k_gain[l, 0], 2).reshape(1, LANES)
    return wbd, w1, pe, b1, w2bd, kg0


def kernel(x_prompt, x_sample, c_prompt, c_sample, cache_cmp, cache_sel, cache_win, state_conv, page_table, w_ada, b_ada, norm1, norm2, w_in, q_gain, k_gain, cmp_w1, cmp_b1, cmp_w2, cmp_pe, conv_w, conv_b, out_gain, w_out, ffn_w1, ffn_w3, ffn_w2, router_w, moe_w1, moe_w3, moe_w2):
    bp, t_p, d = x_prompt.shape
    bs, t_s, _ = x_sample.shape
    depth = w_in.shape[0]
    n_pool, page = cache_cmp.shape[1], cache_cmp.shape[2]
    n_pages = page_table.shape[1]
    past_len = n_pages * page
    n_p, n_s = bp * t_p, bs * t_s
    win_buf = cache_win.shape[2]
    feat = 2 * KV_DIM
    assert page == LANES and n_pages % PAGES_PER_STEP == 0 and t_p % (PAGES_PER_STEP * LANES) == 0
    n_cmp_rows = past_len // CMP_STRIDE
    assert past_len % SEL_BLOCK == 0 and t_s < CMP_STRIDE
    n_sel_s = past_len // SEL_BLOCK + 1

    cmp_t = jnp.transpose(cache_cmp, (0, 1, 3, 4, 5, 2)).reshape(depth, n_pool, feat, page)
    sel_t = jnp.transpose(cache_sel, (0, 1, 3, 4, 5, 2)).reshape(depth, n_pool, feat, page)
    win_t = jnp.transpose(cache_win, (0, 1, 3, 4, 5, 2)).reshape(depth, bs, feat, win_buf)

    wt_all = jnp.transpose(w_in, (0, 2, 1)).astype(BF16)
    wc_all = (wt_all[:, CONV_OFF:].reshape(depth, 3, CONV_DIM // CONV_TILE, CONV_TILE, d)
              .transpose(0, 2, 1, 3, 4).reshape(depth, 3 * CONV_DIM, d))
    w_out_b = w_out.astype(BF16)
    ffn_b = (ffn_w1.astype(BF16), ffn_w3.astype(BF16), ffn_w2.astype(BF16))
    moe_b = (moe_w1.astype(BF16), moe_w3.astype(BF16), moe_w2.astype(BF16))

    c_all = jnp.concatenate([c_prompt, c_sample, jnp.zeros((16 - bp - bs, d), F32)], axis=0)
    mod = _modulation(c_all, w_ada, b_ada)

    agg_p = _agg_matrix(t_p // CMP_STRIDE, t_p // CMP_STRIDE - 1, t_p // SEL_BLOCK, LANES)
    exp_p = _expand_matrix(t_p)
    sel_w = -(-n_sel_s // LANES) * LANES
    agg_s = _agg_matrix(n_cmp_rows, n_cmp_rows + 3, n_sel_s, sel_w)
    prompt_pages = jnp.broadcast_to(jnp.arange(t_p // LANES, dtype=jnp.int32), (bp, t_p // LANES))
    moe_tiles = (TOP_K * n_p) // MOE_TILE + N_EXPERTS

    yp = x_prompt.reshape(n_p, d)
    ys = x_sample.reshape(n_s, d)
    outs = {k: [] for k in ("cmp_p", "sel_p", "win_p", "conv_p", "cmp_s", "sel_s", "win_s", "conv_s")}
    tm_p = 1024
    n_pick = N_SEL - 1
    for l in range(depth):
        cw = _compress_weights(l, k_gain, cmp_w1, cmp_b1, cmp_w2, cmp_pe)
        qg = jnp.tile(q_gain[l] * ATTN_SCALE, Q_TILE // HEAD_DIM).reshape(1, Q_TILE)
        kg = jnp.broadcast_to(k_gain[l][:, None, :, None], (3, N_KV, HEAD_DIM, 1))
        conv_wl, conv_bl, og = conv_w[l], conv_b[l].reshape(1, CONV_DIM), out_gain[l].reshape(1, d)
        mp = [mod[l, :bp, k * d:(k + 1) * d].reshape(bp, 1, d) for k in range(6)]
        ms = [jnp.repeat(mod[l, bp:bp + bs, k * d:(k + 1) * d], t_s, axis=0).reshape(1, n_s, d) for k in range(6)]
        g1 = norm1[l].reshape(1, d)
        g2 = norm2[l].reshape(1, d)

        qn, kvt, gate, u, bg = _inproj(yp, mp[0], mp[1], g1, wt_all, wc_all, qg, kg, tm=tm_p, n_seq=bp, layer=l)
        kc, vc = _compress(
            kvt, prompt_pages,
            lambda k: pl.BlockSpec((None, feat, LANES), lambda b, g, pt: (b, 0, pt[b, g * PAGES_PER_STEP + k])),
            *cw)
        gate_hm = gate[:, :3 * N_HEADS].reshape(n_p, N_KV, 3 * GQA).transpose(1, 0, 2)
        attn = _prompt_attention(qn, gate_hm, kc, vc, kvt, agg_p, exp_p, tq=LANES)
        yp = _outproj(attn, u, bg, conv_wl, conv_bl, og, w_out_b, yp, mp[2], tm=tm_p, n_seq=bp, layer=l)
        kv6 = kvt.reshape(bp, 3, 2, N_KV, HEAD_DIM, t_p)
        outs["cmp_p"].append(kv6[:, 0])
        outs["sel_p"].append(kv6[:, 1])
        outs["win_p"].append(kv6[:, 2, ..., t_p - min(WINDOW, t_p):])
        outs["conv_p"].append(u.reshape(bp, t_p, CONV_DIM)[:, t_p - 2:])

        qn_s, kvt_s, gate_s, u_s, bg_s = _inproj(ys, ms[0], ms[1], g1, wt_all, wc_all, qg, kg, tm=n_s, n_seq=1,
                                                 layer=l)
        layer = l
        kc_s, vc_s = _compress(
            cmp_t, page_table,
            lambda k: pl.BlockSpec((None, None, feat, LANES),
                                   lambda b, g, pt: (layer, pt[b, g * PAGES_PER_STEP + k], 0, 0)),
            *cw)
        q5 = qn_s.astype(F32).reshape(bs, t_s, N_KV, GQA, HEAD_DIM)
        qs_sel = q5.transpose(0, 2, 3, 1, 4).reshape(bs, N_KV, GQA * t_s, HEAD_DIM)
        oc, picked = _sample_select(qs_sel, kc_s, vc_s, agg_s, q0=past_len, n_tok=t_s, n_sel=n_sel_s)
        blocks = picked[:, :, :n_pick].reshape(bs, N_KV, t_s, n_pick)
        page_of = page_table[jnp.arange(bs)[:, None, None, None], blocks // 2].reshape(-1)
        half_of = (blocks % 2).reshape(-1)
        pad8 = ((0, 0), (0, 0), (0, 0), (0, 8 - GQA), (0, 0))
        q8 = jnp.pad(q5.transpose(0, 2, 1, 3, 4), pad8)
        oc8 = jnp.pad(oc.reshape(bs, N_KV, GQA, t_s, HEAD_DIM).transpose(0, 1, 3, 2, 4), pad8)
        gate8 = jnp.pad(gate_s[:, :3 * N_HEADS].reshape(bs, t_s, N_KV, GQA, 3).transpose(0, 2, 1, 3, 4), pad8)
        attn8 = _sample_attend(page_of, half_of, q8, oc8, gate8, kvt_s, sel_t, win_t, n_tok=t_s, n_pick=n_pick,
                               layer=l)
        attn_s = attn8[:, :, :, :GQA].transpose(0, 2, 1, 3, 4).reshape(n_s, ATTN_DIM)
        st = state_conv[l]
        zero = jnp.zeros((bs, t_s - 1, CONV_DIM), F32)
        e1 = jnp.concatenate([st[:, 1:2], zero], axis=1).reshape(n_s, CONV_DIM)
        e2 = jnp.concatenate([st[:, 0:1], st[:, 1:2], zero[:, 1:]], axis=1).reshape(n_s, CONV_DIM)
        ys = _outproj(attn_s, u_s, bg_s, conv_wl, conv_bl, og, w_out_b, ys, ms[2], tm=n_s, n_seq=1, layer=l,
                      ext=(e1, e2), n_tok=t_s)
        rows_s = kvt_s[0].T.reshape(bs, t_s, 3, 2, N_KV, HEAD_DIM)
        outs["cmp_s"].append(rows_s[:, :, 0])
        outs["sel_s"].append(rows_s[:, :, 1])
        new_win_t = kvt_s[0, 4 * KV_DIM:].reshape(feat, bs, t_s).transpose(1, 0, 2)
        outs["win_s"].append(jnp.concatenate([win_t[l], new_win_t], axis=-1)[..., t_s:])
        u_ext = jnp.concatenate([st, u_s.reshape(bs, t_s, CONV_DIM)], axis=1)
        outs["conv_s"].append(u_ext[:, t_s:])

        if l % 2 == 0:
            yp = _dense_ffn(yp, mp[3], mp[4], mp[5], g2, *ffn_b, tm=512, layer=l // 2)
            ys = _dense_ffn(ys, ms[3], ms[4], ms[5], g2, *ffn_b, tm=n_s, layer=l // 2)
        else:
            rw = jnp.zeros((LANES, d), F32).at[:N_EXPERTS].set(jnp.transpose(router_w[l // 2]))
            h_p, route = _router(yp, mp[3], mp[4], g2, rw, tm=512)
            out01 = _moe_grouped(h_p, _moe_plan(route, moe_tiles), *moe_b, layer=l // 2)
            yp = _moe_combine(yp, mp[5], out01, tm=512)
            ys = _moe_ffn(ys, ms[3], ms[4], ms[5], g2, rw, *moe_b, tm=n_s, layer=l // 2)

    def rows_major(x):
        return jnp.transpose(x, (0, 1, 5, 2, 3, 4))

    new_win_s = jnp.stack(outs["win_s"]).reshape(depth, bs, 2, N_KV, HEAD_DIM, win_buf)
    return (yp.reshape(bp, t_p, d), ys.reshape(bs, t_s, d),
            rows_major(jnp.stack(outs["cmp_p"])), rows_major(jnp.stack(outs["sel_p"])),
            rows_major(jnp.stack(outs["win_p"])), jnp.stack(outs["conv_p"]),
            jnp.stack(outs["cmp_s"]), jnp.stack(outs["sel_s"]),
            rows_major(new_win_s), jnp.stack(outs["conv_s"]))
```

```python
import functools

import numpy as np
import jax
import jax.numpy as jnp
from jax import lax
from jax.experimental import pallas as pl
from jax.experimental.pallas import tpu as pltpu

F32 = jnp.float32
BF16 = jnp.bfloat16

D_MODEL = 2048
HEAD_DIM = 64
ATTN_DIM = 1024
N_HEADS = 16
N_KV = 4
GQA = 4
KV_DIM = 256
CONV_DIM = 1024
CMP_STRIDE = 16
SEL_BLOCK = 64
N_SEL = 16
WINDOW = 512
N_EXPERTS = 8
TOP_K = 2
KV_OFF = ATTN_DIM
GATE_OFF = KV_OFF + 6 * KV_DIM
CONV_OFF = GATE_OFF + 3 * N_HEADS
IN_COLS = CONV_OFF + 3 * CONV_DIM
EPS = 1e-6
NEG_INF = -1e30
FORCE_BONUS = 1e4
ATTN_SCALE = HEAD_DIM ** -0.5
LANES = 128
VMEM_LIMIT = 56 * 1024 * 1024
PAGES_PER_STEP = 16
NT_DIMS = (((1,), (1,)), ((), ()))


def _cparams(sem):
    return pltpu.CompilerParams(dimension_semantics=sem, vmem_limit_bytes=VMEM_LIMIT)


def _dot(a, b):
    return jnp.dot(a, b, preferred_element_type=F32)


def _dot_nt(a, b):
    return lax.dot_general(a, b, NT_DIMS, preferred_element_type=F32)


def _silu(x):
    return x * jax.nn.sigmoid(x)


def _split3_dot(x, w_hi):
    x1 = x.astype(BF16)
    r1 = x - x1.astype(F32)
    x2 = r1.astype(BF16)
    x3 = (r1 - x2.astype(F32)).astype(BF16)
    return _dot(x1, w_hi) + _dot(x2, w_hi) + _dot(x3, w_hi)


def _headnorm_lanes(x, gain):
    lane = lax.broadcasted_iota(jnp.int32, (1, LANES), 1)
    lo = lane < HEAD_DIM
    outs = []
    for c in range(x.shape[-1] // LANES):
        xc = x[:, c * LANES:(c + 1) * LANES]
        x2 = xc * xc
        s_lo = jnp.sum(jnp.where(lo, x2, 0.0), axis=-1, keepdims=True)
        s_hi = jnp.sum(jnp.where(lo, 0.0, x2), axis=-1, keepdims=True)
        r = jnp.where(lo, lax.rsqrt(s_lo * (1.0 / HEAD_DIM) + EPS), lax.rsqrt(s_hi * (1.0 / HEAD_DIM) + EPS))
        outs.append(xc * r)
    return jnp.concatenate(outs, axis=-1) * gain


def _masked_softmax(s, mask):
    s = jnp.where(mask, s, NEG_INF)
    m = jnp.max(s, axis=-1, keepdims=True)
    e = jnp.where(mask, jnp.exp(s - m), 0.0)
    return e / jnp.maximum(jnp.sum(e, axis=-1, keepdims=True), 1e-30)


def _rms_modulate(x, g, shift, scale):
    r = lax.rsqrt(jnp.mean(x * x, axis=-1, keepdims=True) + EPS)
    return (x * r * g) * (1.0 + scale) + shift


def _mod_kernel(c_ref, w_ref, b_ref, o_ref):
    s = _silu(c_ref[...])
    o_ref[...] = _dot(s.astype(BF16), w_ref[...].astype(BF16)) + b_ref[...]


def _modulation(c_all, w_ada, b_ada):
    depth, d, cols = w_ada.shape
    rows = c_all.shape[0]
    tn = 1024
    return pl.pallas_call(
        _mod_kernel,
        grid=(depth, cols // tn),
        in_specs=[pl.BlockSpec((rows, d), lambda l, j: (0, 0)),
                  pl.BlockSpec((None, d, tn), lambda l, j: (l, 0, j)),
                  pl.BlockSpec((None, 1, tn), lambda l, j: (l, 0, j))],
        out_specs=pl.BlockSpec((None, rows, tn), lambda l, j: (l, 0, j)),
        out_shape=jax.ShapeDtypeStruct((depth, rows, cols), F32),
        compiler_params=_cparams(("arbitrary", "arbitrary")),
        name="adaln_modulation",
    )(c_all, w_ada, b_ada.reshape(depth, 1, cols))


Q_TILE = 512
KV_TILE = 512
CONV_TILE = 256
N_Q_STEPS = ATTN_DIM // Q_TILE
N_KV_STEPS = 3
N_CONV_STEPS = CONV_DIM // CONV_TILE
N_IN_STEPS = N_Q_STEPS + N_KV_STEPS + N_CONV_STEPS


def _inproj_kernel(y_ref, sh_ref, sc_ref, g1_ref, wq_ref, wkv_ref, wg_ref, wc_ref, qg_ref, kg_ref,
                   qn_ref, kvt_ref, gate_ref, u_ref, bg_ref, h_scr):
    j = pl.program_id(1)

    @pl.when(j == 0)
    def _():
        h = _rms_modulate(y_ref[...], g1_ref[...], sh_ref[...], sc_ref[...])
        h_scr[...] = h.astype(BF16)
        gate_ref[...] = jax.nn.sigmoid(_dot_nt(h_scr[...], wg_ref[...]))

    @pl.when(j < N_Q_STEPS)
    def _():
        p = _dot_nt(h_scr[...], wq_ref[...])
        qn_ref[...] = _headnorm_lanes(p, qg_ref[...]).astype(BF16)

    @pl.when((j >= N_Q_STEPS) & (j < N_Q_STEPS + N_KV_STEPS))
    def _():
        pt = _dot_nt(wkv_ref[...], h_scr[...])
        tok = pt.shape[-1]
        k3 = pt[:KV_DIM].reshape(N_KV, HEAD_DIM, tok)
        ms = jnp.mean(k3 * k3, axis=1, keepdims=True)
        kn = k3 * lax.rsqrt(ms + EPS) * kg_ref[...]
        k3 = jnp.where(j == N_Q_STEPS, k3, kn)
        kvt_ref[:KV_DIM, :] = k3.reshape(KV_DIM, tok)
        kvt_ref[KV_DIM:, :] = pt[KV_DIM:]

    @pl.when(j >= N_Q_STEPS + N_KV_STEPS)
    def _():
        p = _dot_nt(h_scr[...], wc_ref[...])
        u_ref[...] = p[:, 2 * CONV_TILE:] * p[:, :CONV_TILE]
        bg_ref[...] = p[:, CONV_TILE:2 * CONV_TILE]


def _inproj(y, shift, scale, g1, wt_all, wc_all, qg, kg, *, tm, n_seq, layer):
    n, d = y.shape
    t = n // n_seq
    tiles_per_seq = t // tm
    tiles_per_mod = (n // tm) // shift.shape[0]
    r = shift.shape[1]

    def clamp(j, lo, cnt):
        return jnp.clip(j - lo, 0, cnt - 1)

    q_lo, kv_lo, c_lo = 0, N_Q_STEPS, N_Q_STEPS + N_KV_STEPS
    in_specs = [
        pl.BlockSpec((tm, d), lambda i, j: (i, 0)),
        pl.BlockSpec((None, r, d), lambda i, j: (i // tiles_per_mod, 0, 0)),
        pl.BlockSpec((None, r, d), lambda i, j: (i // tiles_per_mod, 0, 0)),
        pl.BlockSpec((1, d), lambda i, j: (0, 0)),
        pl.BlockSpec((None, Q_TILE, d), lambda i, j: (layer, clamp(j, q_lo, N_Q_STEPS), 0)),
        pl.BlockSpec((None, KV_TILE, d), lambda i, j: (layer, KV_OFF // KV_TILE + clamp(j, kv_lo, N_KV_STEPS), 0)),
        pl.BlockSpec((None, LANES, d), lambda i, j: (layer, GATE_OFF // LANES, 0)),
        pl.BlockSpec((None, 3 * CONV_TILE, d), lambda i, j: (layer, clamp(j, c_lo, N_CONV_STEPS), 0)),
        pl.BlockSpec((1, Q_TILE), lambda i, j: (0, 0)),
        pl.BlockSpec((None, N_KV, HEAD_DIM, 1), lambda i, j: (clamp(j, kv_lo, N_KV_STEPS), 0, 0, 0)),
    ]
    out_specs = [
        pl.BlockSpec((tm, Q_TILE), lambda i, j: (i, clamp(j, q_lo, N_Q_STEPS))),
        pl.BlockSpec((None, KV_TILE, tm),
                     lambda i, j: (i // tiles_per_seq, clamp(j, kv_lo, N_KV_STEPS), i % tiles_per_seq)),
        pl.BlockSpec((tm, LANES), lambda i, j: (i, 0)),
        pl.BlockSpec((tm, CONV_TILE), lambda i, j: (i, clamp(j, c_lo, N_CONV_STEPS))),
        pl.BlockSpec((tm, CONV_TILE), lambda i, j: (i, clamp(j, c_lo, N_CONV_STEPS))),
    ]
    out_shape = [
        jax.ShapeDtypeStruct((n, ATTN_DIM), BF16),
        jax.ShapeDtypeStruct((n_seq, 6 * KV_DIM, t), F32),
        jax.ShapeDtypeStruct((n, LANES), F32),
        jax.ShapeDtypeStruct((n, CONV_DIM), F32),
        jax.ShapeDtypeStruct((n, CONV_DIM), F32),
    ]
    return pl.pallas_call(
        _inproj_kernel,
        grid=(n // tm, N_IN_STEPS),
        in_specs=in_specs,
        out_specs=out_specs,
        out_shape=out_shape,
        scratch_shapes=[pltpu.VMEM((tm, d), BF16)],
        compiler_params=_cparams(("arbitrary", "arbitrary")),
        name="input_projection",
    )(y, shift, scale, g1, wt_all, wt_all, wt_all, wc_all, qg, kg)


def _compress_kernel(pt_ref, *refs):
    del pt_ref
    p = PAGES_PER_STEP
    pages = refs[:p]
    wbd_ref, w1_ref, pe_ref, b1_ref, w2bd_ref, kg_ref, kc_ref, vc_ref, xs, carry, cconst = refs[p:]
    g = pl.program_id(1)
    cpp = LANES // CMP_STRIDE
    m = p * cpp

    @pl.when(g == 0)
    def _():
        carry[...] = jnp.zeros_like(carry)

    @pl.when((g == 0) & (pl.program_id(0) == 0))
    def _():
        for kv in range(2):
            c64 = jnp.sum(pe_ref[kv] * w1_ref[kv], axis=0, keepdims=True) + b1_ref[kv]
            cconst[kv] = jnp.broadcast_to(jnp.concatenate([c64, c64], axis=-1), (8, LANES))

    for k in range(p):
        for jj in range(4):
            tile = pages[k][jj * LANES:(jj + 1) * LANES, :]
            xs[jj, k * LANES:(k + 1) * LANES, :] = tile.T

    row = lax.broadcasted_iota(jnp.int32, (m, 1), 0)
    for jj in range(4):
        kv = jj // 2
        acc = jnp.zeros((m, 2 * LANES), F32)
        for t in range(CMP_STRIDE):
            acc = acc + _dot(xs[jj, pl.ds(t, m, stride=CMP_STRIDE), :].astype(BF16), wbd_ref[kv, t])
        a_part = acc[:, :LANES]
        b_part = acc[:, LANES:]
        prev = carry[jj]
        a_shift = jnp.where(row == 0, prev[7:8, :], pltpu.roll(a_part, 1, axis=0))
        carry[jj] = a_part[m - 8:, :]
        pre = a_shift + b_part + cconst[kv][0:1, :]
        out = _dot(_silu(pre).astype(BF16), w2bd_ref[kv])
        if kv == 0:
            out = _headnorm_lanes(out, kg_ref[...])
        dst = kc_ref if kv == 0 else vc_ref
        hb = (jj % 2) * 2
        dst[hb] = out[:, :HEAD_DIM].astype(BF16)
        dst[hb + 1] = out[:, HEAD_DIM:].astype(BF16)


def _compress(src, page_ids, page_spec_fn, wbd, w1, pe, b1, w2bd, kg0):
    n_seq, n_pages = page_ids.shape
    p = PAGES_PER_STEP
    m = p * (LANES // CMP_STRIDE)
    n_groups = n_pages // p
    n_rows = n_pages * (LANES // CMP_STRIDE)
    page_specs = [page_spec_fn(k) for k in range(p)]

    def const(shape):
        return pl.BlockSpec(shape, lambda b, g, pt: (0,) * len(shape))

    in_specs = page_specs + [const(wbd.shape), const(w1.shape), const(pe.shape), const(b1.shape),
                             const(w2bd.shape), const(kg0.shape)]
    out_spec = pl.BlockSpec((None, N_KV, m, HEAD_DIM), lambda b, g, pt: (b, 0, g, 0))
    return pl.pallas_call(
        _compress_kernel,
        grid_spec=pltpu.PrefetchScalarGridSpec(
            num_scalar_prefetch=1,
            grid=(n_seq, n_groups),
            in_specs=in_specs,
            out_specs=[out_spec, out_spec],
            scratch_shapes=[pltpu.VMEM((4, p * LANES, LANES), F32), pltpu.VMEM((4, 8, LANES), F32),
                            pltpu.VMEM((2, 8, LANES), F32)],
        ),
        out_shape=[jax.ShapeDtypeStruct((n_seq, N_KV, n_rows, HEAD_DIM), BF16)] * 2,
        compiler_params=_cparams(("arbitrary", "arbitrary")),
        name="compress_mlp",
    )(page_ids, *([src] * p), wbd, w1, pe, b1, w2bd, kg0)


SEL_CHUNK = 512


def _attn_kernel(q_ref, gate_ref, kc_ref, vc_ref, kst_ref, vst_ref, kwt_ref, vwt_ref, agg_ref, exp_ref,
                 o_ref, bias_scr, *, tq, t_len):
    i = pl.program_id(2)
    q0 = i * tq
    m_rows = GQA * tq
    n_sel = t_len // SEL_BLOCK
    win_keys = WINDOW + tq
    q = q_ref[...]
    qs = jnp.concatenate([q[:, g * HEAD_DIM:(g + 1) * HEAD_DIM] for g in range(GQA)], axis=0)
    row = lax.broadcasted_iota(jnp.int32, (m_rows, 1), 0)
    qpos = q0 + (row % tq)
    qp = q0 + lax.broadcasted_iota(jnp.int32, (tq, 1), 0)

    s_c = _dot_nt(qs, kc_ref[...])
    r_i = lax.broadcasted_iota(jnp.int32, (1, s_c.shape[-1]), 1)
    mask_c = (r_i >= 1) & (CMP_STRIDE * r_i + (CMP_STRIDE - 1) <= qpos)
    p_c = _masked_softmax(s_c, mask_c)
    o_c = _dot(p_c.astype(BF16), vc_ref[...])
    imp = p_c[0:tq] + p_c[tq:2 * tq] + p_c[2 * tq:3 * tq] + p_c[3 * tq:4 * tq]
    imp_s = _split3_dot(imp, agg_ref[...])

    sidx = lax.broadcasted_iota(jnp.int32, (1, LANES), 1)
    cur = qp // SEL_BLOCK
    forced = (sidx == 0) | (sidx == cur) | (sidx == cur - 1)
    causal = sidx * SEL_BLOCK <= qp
    score = jnp.where(causal, imp_s + jnp.where(forced, FORCE_BONUS, 0.0), NEG_INF)
    sv = score.T[:n_sel]
    srow = lax.broadcasted_iota(jnp.int32, (n_sel, 1), 0)
    cnt = jnp.zeros((n_sel, tq), jnp.int32)
    for t in range(n_sel):
        other = sv[t:t + 1, :]
        beats = (other > sv) | ((other == sv) & (t < srow))
        cnt = cnt + beats.astype(jnp.int32)
    sel_t = jnp.where((cnt < min(N_SEL, n_sel)) & (sv > NEG_INF / 2), 1.0, 0.0)
    sel = jnp.concatenate([sel_t, jnp.zeros((LANES - n_sel, tq), F32)], axis=0).T
    selm = _dot(sel.astype(BF16), exp_ref[...])
    kpos_all = lax.broadcasted_iota(jnp.int32, (1, t_len), 1)
    bias_scr[...] = jnp.where((selm > 0.5) & (kpos_all <= qp), 0.0, NEG_INF)

    n_chunks = (q0 + tq + SEL_CHUNK - 1) // SEL_CHUNK

    def body(c, carry):
        m_old, l_old, acc = carry
        off = pl.multiple_of(c * SEL_CHUNK, SEL_CHUNK)
        kt = kst_ref[:, pl.ds(off, SEL_CHUNK)].astype(BF16)
        s = _dot(qs, kt).reshape(GQA, tq, SEL_CHUNK) + bias_scr[:, pl.ds(off, SEL_CHUNK)][None]
        s = s.reshape(m_rows, SEL_CHUNK)
        m_new = jnp.maximum(m_old, jnp.max(s, axis=-1, keepdims=True))
        alpha = jnp.exp(m_old - m_new)
        pr = jnp.exp(s - m_new)
        l_new = alpha * l_old + jnp.sum(pr, axis=-1, keepdims=True)
        vt = vst_ref[:, pl.ds(off, SEL_CHUNK)].astype(BF16)
        acc = alpha * acc + _dot_nt(pr.astype(BF16), vt)
        return m_new, l_new, acc

    init = (jnp.full((m_rows, 1), NEG_INF, F32), jnp.zeros((m_rows, 1), F32), jnp.zeros((m_rows, HEAD_DIM), F32))
    _, l_s, acc_s = lax.fori_loop(0, n_chunks, body, init)
    o_s = acc_s / jnp.maximum(l_s, 1e-30)

    start = pl.multiple_of(jnp.clip(q0 - WINDOW, 0, t_len - win_keys), LANES)
    dist = qp - (start + lax.broadcasted_iota(jnp.int32, (1, win_keys), 1))
    bias_w = jnp.where((dist >= 0) & (dist < WINDOW), 0.0, NEG_INF)
    s_w = _dot(qs, kwt_ref[:, pl.ds(start, win_keys)].astype(BF16)).reshape(GQA, tq, win_keys) + bias_w[None]
    s_w = s_w.reshape(m_rows, win_keys)
    e_w = jnp.exp(s_w - jnp.max(s_w, axis=-1, keepdims=True))
    p_w = e_w / jnp.maximum(jnp.sum(e_w, axis=-1, keepdims=True), 1e-30)
    o_w = _dot_nt(p_w.astype(BF16), vwt_ref[:, pl.ds(start, win_keys)].astype(BF16))

    gate = gate_ref[...]

    def gcol(jb):
        return jnp.concatenate([jnp.broadcast_to(gate[:, g * 3 + jb:g * 3 + jb + 1], (tq, HEAD_DIM))
                                for g in range(GQA)], axis=0)

    o = gcol(0) * o_c + gcol(1) * o_s + gcol(2) * o_w
    o_ref[...] = jnp.concatenate([o[g * tq:(g + 1) * tq] for g in range(GQA)], axis=-1)


def _prompt_attention(qn, gate_hm, kc, vc, kvt, agg, expand, *, tq):
    n = qn.shape[0]
    n_seq, _, t_len = kvt.shape
    tiles = t_len // tq
    kern = functools.partial(_attn_kernel, tq=tq, t_len=t_len)

    def kv_spec(block_row):
        return pl.BlockSpec((None, HEAD_DIM, t_len), lambda b, h, i: (b, block_row + h, 0))

    sel_k, sel_v = (2 * KV_DIM) // HEAD_DIM, (3 * KV_DIM) // HEAD_DIM
    win_k, win_v = (4 * KV_DIM) // HEAD_DIM, (5 * KV_DIM) // HEAD_DIM
    return pl.pallas_call(
        kern,
        grid=(n_seq, N_KV, tiles),
        in_specs=[
            pl.BlockSpec((tq, GQA * HEAD_DIM), lambda b, h, i: (b * tiles + i, h)),
            pl.BlockSpec((None, tq, 3 * GQA), lambda b, h, i: (h, b * tiles + i, 0)),
            pl.BlockSpec((None, None, kc.shape[2], HEAD_DIM), lambda b, h, i: (b, h, 0, 0)),
            pl.BlockSpec((None, None, vc.shape[2], HEAD_DIM), lambda b, h, i: (b, h, 0, 0)),
            kv_spec(sel_k), kv_spec(sel_v), kv_spec(win_k), kv_spec(win_v),
            pl.BlockSpec(agg.shape, lambda b, h, i: (0, 0)),
            pl.BlockSpec(expand.shape, lambda b, h, i: (0, 0)),
        ],
        out_specs=pl.BlockSpec((tq, GQA * HEAD_DIM), lambda b, h, i: (b * tiles + i, h)),
        out_shape=jax.ShapeDtypeStruct((n, ATTN_DIM), F32),
        scratch_shapes=[pltpu.VMEM((tq, t_len), F32)],
        compiler_params=_cparams(("arbitrary", "arbitrary", "arbitrary")),
        name="prompt_attention",
    )(qn, gate_hm, kc, vc, kvt, kvt, kvt, kvt, agg, expand)


def _sample_select_kernel(qs_ref, kc_ref, vc_ref, agg_ref, oc_ref, idx_ref, *, q0, n_tok, n_sel, n_pick):
    rows = GQA * n_tok
    row = lax.broadcasted_iota(jnp.int32, (rows, 1), 0)
    qpos = q0 + (row % n_tok)
    imps = []
    for h in range(N_KV):
        s_c = _dot_nt(qs_ref[h].astype(BF16), kc_ref[h])
        r_i = lax.broadcasted_iota(jnp.int32, (1, s_c.shape[-1]), 1)
        mask_c = (r_i >= 1) & (CMP_STRIDE * r_i + (CMP_STRIDE - 1) <= qpos)
        p_c = _masked_softmax(s_c, mask_c)
        oc_ref[h] = _dot(p_c.astype(BF16), vc_ref[h])
        imp = p_c[0:n_tok]
        for g in range(1, GQA):
            imp = imp + p_c[g * n_tok:(g + 1) * n_tok]
        imps.append(imp)
    imp_all = jnp.concatenate(imps, axis=0)
    imp_s = _split3_dot(imp_all, agg_ref[...])
    width = imp_s.shape[-1]
    sidx = lax.broadcasted_iota(jnp.int32, (1, width), 1)
    r2 = lax.broadcasted_iota(jnp.int32, (N_KV * n_tok, 1), 0)
    qp = q0 + (r2 % n_tok)
    cur = qp // SEL_BLOCK
    forced = (sidx == 0) | (sidx == cur) | (sidx == cur - 1)
    causal = sidx * SEL_BLOCK <= qp
    score = jnp.where(causal, imp_s + jnp.where(forced, FORCE_BONUS, 0.0), NEG_INF)
    lowest = -3.0e38
    score = jnp.where((sidx >= n_sel - 1), lowest, score)
    out_lane = lax.broadcasted_iota(jnp.int32, (1, LANES), 1)
    picked = jnp.zeros((N_KV * n_tok, LANES), jnp.int32)
    for it in range(n_pick):
        mx = jnp.max(score, axis=-1, keepdims=True)
        first = jnp.min(jnp.where(score == mx, sidx, width), axis=-1, keepdims=True)
        picked = jnp.where(out_lane == it, first, picked)
        score = jnp.where(sidx == first, lowest, score)
    idx_ref[...] = picked


def _sample_select(qs, kc, vc, agg, *, q0, n_tok, n_sel):
    n_seq = qs.shape[0]
    n_pick = N_SEL - 1
    kern = functools.partial(_sample_select_kernel, q0=q0, n_tok=n_tok, n_sel=n_sel, n_pick=n_pick)
    return pl.pallas_call(
        kern,
        grid=(n_seq,),
        in_specs=[
            pl.BlockSpec((None, N_KV, GQA * n_tok, HEAD_DIM), lambda b: (b, 0, 0, 0)),
            pl.BlockSpec((None, N_KV, kc.shape[2], HEAD_DIM), lambda b: (b, 0, 0, 0)),
            pl.BlockSpec((None, N_KV, vc.shape[2], HEAD_DIM), lambda b: (b, 0, 0, 0)),
            pl.BlockSpec(agg.shape, lambda b: (0, 0)),
        ],
        out_specs=[
            pl.BlockSpec((None, N_KV, GQA * n_tok, HEAD_DIM), lambda b: (b, 0, 0, 0)),
            pl.BlockSpec((None, N_KV * n_tok, LANES), lambda b: (b, 0, 0)),
        ],
        out_shape=[
            jax.ShapeDtypeStruct((n_seq, N_KV, GQA * n_tok, HEAD_DIM), F32),
            jax.ShapeDtypeStruct((n_seq, N_KV * n_tok, LANES), jnp.int32),
        ],
        compiler_params=_cparams(("arbitrary",)),
        name="sample_select",
    )(qs, kc, vc, agg)


def _sample_attend_kernel(pg_ref, half_ref, q_ref, oc_ref, gate_ref, knew_ref, vnew_ref, kwp_ref, vwp_ref,
                          kwn_ref, vwn_ref, *refs, n_tok, n_pick):
    k_pages = refs[:n_pick]
    v_pages = refs[n_pick:2 * n_pick]
    o_ref = refs[2 * n_pick]
    b = pl.program_id(0)
    h = pl.program_id(1)
    t = pl.program_id(2)
    q = q_ref[...].astype(BF16)
    lane = lax.broadcasted_iota(jnp.int32, (1, LANES), 1)
    n_new = knew_ref.shape[-1]
    new_lane = lax.broadcasted_iota(jnp.int32, (1, n_new), 1)
    new_mask = (new_lane // n_tok == b) & (new_lane % n_tok <= t)

    flat = ((b * N_KV + h) * n_tok + t) * n_pick
    scores, masks = [], []
    for k in range(n_pick):
        half = half_ref[flat + k]
        scores.append(_dot(q, k_pages[k][...].astype(BF16)))
        masks.append(lane // SEL_BLOCK == half)
    s_new = _dot(q, knew_ref[...].astype(BF16))
    mx = jnp.max(jnp.where(new_mask, s_new, NEG_INF), axis=-1, keepdims=True)
    for s, mk in zip(scores, masks):
        mx = jnp.maximum(mx, jnp.max(jnp.where(mk, s, NEG_INF), axis=-1, keepdims=True))
    e_new = jnp.exp(jnp.where(new_mask, s_new, NEG_INF) - mx) * new_mask.astype(F32)
    den = jnp.sum(e_new, axis=-1, keepdims=True)
    acc = _dot_nt(e_new.astype(BF16), vnew_ref[...].astype(BF16))
    for k in range(n_pick):
        e = jnp.exp(jnp.where(masks[k], scores[k], NEG_INF) - mx) * masks[k].astype(F32)
        den = den + jnp.sum(e, axis=-1, keepdims=True)
        acc = acc + _dot_nt(e.astype(BF16), v_pages[k][...].astype(BF16))
    o_s = acc / jnp.maximum(den, 1e-30)

    n_win = kwp_ref.shape[-1]
    wl = lax.broadcasted_iota(jnp.int32, (1, n_win), 1)
    past_mask = (n_win - wl + t) < WINDOW
    s_p = jnp.where(past_mask, _dot(q, kwp_ref[...].astype(BF16)), NEG_INF)
    s_n = jnp.where(new_mask, _dot(q, kwn_ref[...].astype(BF16)), NEG_INF)
    mw = jnp.maximum(jnp.max(s_p, axis=-1, keepdims=True), jnp.max(s_n, axis=-1, keepdims=True))
    e_p = jnp.exp(s_p - mw) * past_mask.astype(F32)
    e_n = jnp.exp(s_n - mw) * new_mask.astype(F32)
    den_w = jnp.sum(e_p, axis=-1, keepdims=True) + jnp.sum(e_n, axis=-1, keepdims=True)
    acc_w = _dot_nt(e_p.astype(BF16), vwp_ref[...].astype(BF16)) + _dot_nt(e_n.astype(BF16), vwn_ref[...].astype(BF16))
    o_w = acc_w / jnp.maximum(den_w, 1e-30)

    gate = gate_ref[...]
    o_ref[...] = gate[:, 0:1] * oc_ref[...] + gate[:, 1:2] * o_s + gate[:, 2:3] * o_w


def _sample_attend(page_of, half_of, q8, oc8, gate8, kvt_new, cache_sel_t, cache_win_t, *, n_tok, n_pick, layer):
    n_seq = q8.shape[0]
    kern = functools.partial(_sample_attend_kernel, n_tok=n_tok, n_pick=n_pick)
    n_new = kvt_new.shape[-1]
    n_win = cache_win_t.shape[-1]

    def qspec(last):
        return pl.BlockSpec((None, None, None, 8, last), lambda b, h, t, pg, hf: (b, h, t, 0, 0))

    def new_spec(block_row):
        return pl.BlockSpec((None, HEAD_DIM, n_new), lambda b, h, t, pg, hf: (0, block_row + h, 0))

    def win_spec(block_row):
        return pl.BlockSpec((None, None, HEAD_DIM, n_win), lambda b, h, t, pg, hf: (layer, b, block_row + h, 0))

    def page_spec(k, block_row):
        def imap(b, h, t, pg, hf):
            return (layer, pg[((b * N_KV + h) * n_tok + t) * n_pick + k], block_row + h, 0)
        return pl.BlockSpec((None, None, HEAD_DIM, LANES), imap)

    sel_k, sel_v = (2 * KV_DIM) // HEAD_DIM, (3 * KV_DIM) // HEAD_DIM
    win_k, win_v = (4 * KV_DIM) // HEAD_DIM, (5 * KV_DIM) // HEAD_DIM
    in_specs = ([qspec(HEAD_DIM), qspec(HEAD_DIM), qspec(3), new_spec(sel_k), new_spec(sel_v),
                 win_spec(0), win_spec(N_KV), new_spec(win_k), new_spec(win_v)]
                + [page_spec(k, 0) for k in range(n_pick)] + [page_spec(k, N_KV) for k in range(n_pick)])
    return pl.pallas_call(
        kern,
        grid_spec=pltpu.PrefetchScalarGridSpec(
            num_scalar_prefetch=2,
            grid=(n_seq, N_KV, n_tok),
            in_specs=in_specs,
            out_specs=qspec(HEAD_DIM),
        ),
        out_shape=jax.ShapeDtypeStruct(q8.shape, F32),
        compiler_params=_cparams(("arbitrary", "arbitrary", "arbitrary")),
        name="sample_attend",
    )(page_of, half_of, q8, oc8, gate8, kvt_new, kvt_new, cache_win_t, cache_win_t, kvt_new, kvt_new,
      *([cache_sel_t] * (2 * n_pick)))


def _mix_prologue(attn, u, u1, u2, bg, cw_ref, cb_ref, og_ref, o_scr):
    og = og_ref[...]
    an = attn * lax.rsqrt(jnp.mean(attn * attn, axis=-1, keepdims=True) + EPS) * og[:, :ATTN_DIM]
    yv = cb_ref[...] + cw_ref[0:1, :] * u2
    yv = yv + cw_ref[1:2, :] * u1
    yv = yv + cw_ref[2:3, :] * u
    cv = bg * yv
    cn = cv * lax.rsqrt(jnp.mean(cv * cv, axis=-1, keepdims=True) + EPS) * og[:, ATTN_DIM:]
    o_scr[:, :ATTN_DIM] = an.astype(BF16)
    o_scr[:, ATTN_DIM:] = cn.astype(BF16)


def _outproj_seq_kernel(attn_ref, u_ref, halo_ref, bg_ref, cw_ref, cb_ref, og_ref, w_ref, y_ref, gm_ref,
                        o_ref, o_scr, *, tiles_per_seq):
    i = pl.program_id(0)

    @pl.when(pl.program_id(1) == 0)
    def _():
        u = u_ref[...]
        tm = u.shape[0]
        row = lax.broadcasted_iota(jnp.int32, (tm, 1), 0)
        halo = jnp.where(i % tiles_per_seq == 0, 0.0, halo_ref[...])
        p1 = halo[7:8, :]
        p2 = halo[6:7, :]
        u1 = jnp.where(row == 0, p1, pltpu.roll(u, 1, axis=0))
        u2 = jnp.where(row == 0, p2, jnp.where(row == 1, p1, pltpu.roll(u, 2, axis=0)))
        _mix_prologue(attn_ref[...], u, u1, u2, bg_ref[...], cw_ref, cb_ref, og_ref, o_scr)

    o_ref[...] = y_ref[...] + gm_ref[...] * _dot(o_scr[...], w_ref[...])


def _outproj_state_kernel(attn_ref, u_ref, e1_ref, e2_ref, bg_ref, cw_ref, cb_ref, og_ref, w_ref, y_ref, gm_ref,
                          o_ref, o_scr, *, n_tok):
    @pl.when(pl.program_id(1) == 0)
    def _():
        u = u_ref[...]
        tm = u.shape[0]
        k = lax.broadcasted_iota(jnp.int32, (tm, 1), 0) % n_tok
        u1 = jnp.where(k >= 1, pltpu.roll(u, 1, axis=0), e1_ref[...])
        u2 = jnp.where(k >= 2, pltpu.roll(u, 2, axis=0), e2_ref[...])
        _mix_prologue(attn_ref[...], u, u1, u2, bg_ref[...], cw_ref, cb_ref, og_ref, o_scr)

    o_ref[...] = y_ref[...] + gm_ref[...] * _dot(o_scr[...], w_ref[...])


def _outproj(attn, u, bg, cw, cb, og, w, y, gmod, *, tm, n_seq, layer, ext=None, n_tok=None):
    n, d = y.shape
    tn = 512
    tiles_per_seq = (n // n_seq) // tm
    tiles_per_mod = (n // tm) // gmod.shape[0]
    r = gmod.shape[1]
    full = lambda i, j: (i, 0)
    c0 = lambda i, j: (0, 0)
    common = [pl.BlockSpec((tm, CONV_DIM), full), pl.BlockSpec((3, CONV_DIM), c0), pl.BlockSpec((1, CONV_DIM), c0),
              pl.BlockSpec((1, d), c0), pl.BlockSpec((None, d, tn), lambda i, j: (layer, 0, j)),
              pl.BlockSpec((tm, tn), lambda i, j: (i, j)),
              pl.BlockSpec((None, r, tn), lambda i, j: (i // tiles_per_mod, 0, j))]
    head = [pl.BlockSpec((tm, ATTN_DIM), full), pl.BlockSpec((tm, CONV_DIM), full)]
    if ext is None:
        kern = functools.partial(_outproj_seq_kernel, tiles_per_seq=tiles_per_seq)
        extra = [pl.BlockSpec((8, CONV_DIM), lambda i, j: (jnp.maximum(i * (tm // 8) - 1, 0), 0))]
        args = (attn, u, u, bg, cw, cb, og, w, y, gmod)
    else:
        kern = functools.partial(_outproj_state_kernel, n_tok=n_tok)
        extra = [pl.BlockSpec((tm, CONV_DIM), full), pl.BlockSpec((tm, CONV_DIM), full)]
        args = (attn, u, ext[0], ext[1], bg, cw, cb, og, w, y, gmod)
    return pl.pallas_call(
        kern,
        grid=(n // tm, d // tn),
        in_specs=head + extra + common,
        out_specs=pl.BlockSpec((tm, tn), lambda i, j: (i, j)),
        out_shape=jax.ShapeDtypeStruct((n, d), F32),
        scratch_shapes=[pltpu.VMEM((tm, d), BF16)],
        compiler_params=_cparams(("arbitrary", "arbitrary")),
        name="output_projection",
    )(*args)


def _ffn_kernel(y_ref, sh_ref, sc_ref, gm_ref, g2_ref, w1_ref, w3_ref, w2_ref, o_ref, h_scr, acc):
    j = pl.program_id(1)

    @pl.when(j == 0)
    def _():
        h_scr[...] = _rms_modulate(y_ref[...], g2_ref[...], sh_ref[...], sc_ref[...]).astype(BF16)
        acc[...] = jnp.zeros_like(acc)

    h = h_scr[...]
    hid = _silu(_dot(h, w1_ref[...])) * _dot(h, w3_ref[...])
    acc[...] += _dot(hid.astype(BF16), w2_ref[...])

    @pl.when(j == pl.num_programs(1) - 1)
    def _():
        o_ref[...] = y_ref[...] + gm_ref[...] * acc[...]


def _dense_ffn(y, shift, scale, gmod, g2, w1, w3, w2, *, tm, layer):
    n, d = y.shape
    f = w1.shape[-1]
    tf = 512
    tiles_per_mod = (n // tm) // shift.shape[0]
    r = shift.shape[1]
    mod = pl.BlockSpec((None, r, d), lambda i, j: (i // tiles_per_mod, 0, 0))
    return pl.pallas_call(
        _ffn_kernel,
        grid=(n // tm, f // tf),
        in_specs=[pl.BlockSpec((tm, d), lambda i, j: (i, 0)), mod, mod, mod,
                  pl.BlockSpec((1, d), lambda i, j: (0, 0)),
                  pl.BlockSpec((None, d, tf), lambda i, j: (layer, 0, j)),
                  pl.BlockSpec((None, d, tf), lambda i, j: (layer, 0, j)),
                  pl.BlockSpec((None, tf, d), lambda i, j: (layer, j, 0))],
        out_specs=pl.BlockSpec((tm, d), lambda i, j: (i, 0)),
        out_shape=jax.ShapeDtypeStruct((n, d), F32),
        scratch_shapes=[pltpu.VMEM((tm, d), BF16), pltpu.VMEM((tm, d), F32)],
        compiler_params=_cparams(("arbitrary", "arbitrary")),
        name="dense_ffn",
    )(y, shift, scale, gmod, g2, w1, w3, w2)


def _top2_route(h, h_bf, rw):
    lane = lax.broadcasted_iota(jnp.int32, (1, LANES), 1)
    rw_hi = rw.astype(BF16)
    rw_lo = (rw - rw_hi.astype(F32)).astype(BF16)
    h_lo = (h - h_bf.astype(F32)).astype(BF16)
    logits = _dot_nt(h_bf, rw_hi) + _dot_nt(h_lo, rw_hi) + _dot_nt(h_bf, rw_lo)
    lowest = -3.0e38
    logits = jnp.where(lane < N_EXPERTS, logits, lowest)
    m1 = jnp.max(logits, axis=-1, keepdims=True)
    i1 = jnp.min(jnp.where(logits == m1, lane, LANES), axis=-1, keepdims=True)
    rest = jnp.where(lane == i1, lowest, logits)
    m2 = jnp.max(rest, axis=-1, keepdims=True)
    i2 = jnp.min(jnp.where(rest == m2, lane, LANES), axis=-1, keepdims=True)
    e2 = jnp.exp(m2 - m1)
    den = 1.0 + e2
    return i1, i2, 1.0 / den, e2 / den


def _moe_kernel(y_ref, sh_ref, sc_ref, gm_ref, g2_ref, rw_ref, w1_ref, w3_ref, w2_ref, o_ref, h_scr, acc, comb):
    e = pl.program_id(1)
    j = pl.program_id(2)
    lane = lax.broadcasted_iota(jnp.int32, (1, LANES), 1)

    @pl.when((e == 0) & (j == 0))
    def _():
        h = _rms_modulate(y_ref[...], g2_ref[...], sh_ref[...], sc_ref[...])
        h_scr[...] = h.astype(BF16)
        acc[...] = jnp.zeros_like(acc)
        i1, i2, w1, w2 = _top2_route(h, h_scr[...], rw_ref[...])
        comb[...] = jnp.where(lane == i1, w1, 0.0) + jnp.where(lane == i2, w2, 0.0)

    h = h_scr[...]
    hid = _silu(_dot(h, w1_ref[...].astype(BF16))) * _dot(h, w3_ref[...].astype(BF16))
    ce = jnp.sum(jnp.where(lane == e, comb[...], 0.0), axis=-1, keepdims=True)
    acc[...] += ce * _dot(hid.astype(BF16), w2_ref[...].astype(BF16))

    @pl.when((e == pl.num_programs(1) - 1) & (j == pl.num_programs(2) - 1))
    def _():
        o_ref[...] = y_ref[...] + gm_ref[...] * acc[...]


def _moe_ffn(y, shift, scale, gmod, g2, rw, w1, w3, w2, *, tm, layer):
    n, d = y.shape
    _, n_e, _, fe = w1.shape
    tf = 256
    tiles_per_mod = (n // tm) // shift.shape[0]
    r = shift.shape[1]
    mod = pl.BlockSpec((None, r, d), lambda i, e, j: (i // tiles_per_mod, 0, 0))
    return pl.pallas_call(
        _moe_kernel,
        grid=(n // tm, n_e, fe // tf),
        in_specs=[pl.BlockSpec((tm, d), lambda i, e, j: (i, 0)), mod, mod, mod,
                  pl.BlockSpec((1, d), lambda i, e, j: (0, 0)),
                  pl.BlockSpec((LANES, d), lambda i, e, j: (0, 0)),
                  pl.BlockSpec((None, None, d, tf), lambda i, e, j: (layer, e, 0, j)),
                  pl.BlockSpec((None, None, d, tf), lambda i, e, j: (layer, e, 0, j)),
                  pl.BlockSpec((None, None, tf, d), lambda i, e, j: (layer, e, j, 0))],
        out_specs=pl.BlockSpec((tm, d), lambda i, e, j: (i, 0)),
        out_shape=jax.ShapeDtypeStruct((n, d), F32),
        scratch_shapes=[pltpu.VMEM((tm, d), BF16), pltpu.VMEM((tm, d), F32), pltpu.VMEM((tm, LANES), F32)],
        compiler_params=_cparams(("arbitrary", "arbitrary", "arbitrary")),
        name="moe_ffn",
    )(y, shift, scale, gmod, g2, rw, w1, w3, w2)


MOE_TILE = 512
MOE_FT = 256


def _router_kernel(y_ref, sh_ref, sc_ref, g2_ref, rw_ref, h_ref, route_ref):
    h = _rms_modulate(y_ref[...], g2_ref[...], sh_ref[...], sc_ref[...])
    h_ref[...] = h
    i1, i2, w1, w2 = _top2_route(h, h.astype(BF16), rw_ref[...])
    lane = lax.broadcasted_iota(jnp.int32, (1, LANES), 1)
    route_ref[...] = (jnp.where(lane == 0, i1.astype(F32), 0.0) + jnp.where(lane == 1, i2.astype(F32), 0.0)
                      + jnp.where(lane == 2, w1, 0.0) + jnp.where(lane == 3, w2, 0.0))


def _router(y, shift, scale, g2, rw, *, tm):
    n, d = y.shape
    tiles_per_mod = (n // tm) // shift.shape[0]
    r = shift.shape[1]
    mod = pl.BlockSpec((None, r, d), lambda i: (i // tiles_per_mod, 0, 0))
    return pl.pallas_call(
        _router_kernel,
        grid=(n // tm,),
        in_specs=[pl.BlockSpec((tm, d), lambda i: (i, 0)), mod, mod,
                  pl.BlockSpec((1, d), lambda i: (0, 0)), pl.BlockSpec((LANES, d), lambda i: (0, 0))],
        out_specs=[pl.BlockSpec((tm, d), lambda i: (i, 0)), pl.BlockSpec((tm, LANES), lambda i: (i, 0))],
        out_shape=[jax.ShapeDtypeStruct((n, d), F32), jax.ShapeDtypeStruct((n, LANES), F32)],
        compiler_params=_cparams(("arbitrary",)),
        name="moe_router",
    )(y, shift, scale, g2, rw)


def _moe_plan(route, n_tiles):
    n = route.shape[0]
    flat_e = route[:, :2].astype(jnp.int32).reshape(-1)
    onehot = (flat_e[:, None] == jnp.arange(N_EXPERTS, dtype=jnp.int32)[None, :]).astype(jnp.int32)
    csum = jnp.cumsum(onehot, axis=0)
    pos = jnp.sum((csum - 1) * onehot, axis=1)
    sizes = csum[-1]
    padded = ((sizes + MOE_TILE - 1) // MOE_TILE) * MOE_TILE
    ends = jnp.cumsum(padded)
    dest = (ends - padded)[flat_e] + pos
    rows = n_tiles * MOE_TILE
    a_of_row = jnp.full((rows,), -1, jnp.int32).at[dest].set(jnp.arange(2 * n, dtype=jnp.int32),
                                                             unique_indices=True)
    real = a_of_row >= 0
    a0 = jnp.maximum(a_of_row, 0)
    src = a0 // 2
    spare = 2 * n + jnp.arange(rows, dtype=jnp.int32) % MOE_TILE
    dst = jnp.where(real, (a0 % 2) * n + a0 // 2, spare)
    tile_start = jnp.arange(n_tiles, dtype=jnp.int32) * MOE_TILE
    te = jnp.sum((tile_start[:, None] >= ends[None, :]).astype(jnp.int32), axis=1)
    tv = (te < N_EXPERTS).astype(jnp.int32)
    return jnp.minimum(te, N_EXPERTS - 1), tv, src, dst


def _row_copy(src_ref, src_row, dst_ref, dst_row, sem):
    return pltpu.make_async_copy(src_ref.at[pl.ds(src_row, 1)], dst_ref.at[pl.ds(dst_row, 1)], sem)


ROW_LOOP_UNROLL = 8


def _moe_group_kernel(te_ref, tv_ref, src_ref, dst_ref, h_hbm, w1_ref, w3_ref, w2_ref, out_hbm,
                      xbuf, hb, acc, obuf, gsem, ssem):
    del te_ref
    i = pl.program_id(0)
    j = pl.program_id(1)
    n_t = pl.num_programs(0)
    n_j = pl.num_programs(1)
    valid = tv_ref[i] == 1
    slot = i % 2

    def row_loop(body):
        def step(r, c):
            body(r)
            return c
        lax.fori_loop(0, MOE_TILE, step, 0, unroll=ROW_LOOP_UNROLL)

    def gather_start(tile, s):
        row_loop(lambda r: _row_copy(h_hbm, src_ref[tile * MOE_TILE + r], xbuf.at[s], r, gsem.at[s]).start())

    def gather_wait(s):
        pltpu.make_async_copy(h_hbm.at[pl.ds(0, MOE_TILE)], xbuf.at[s], gsem.at[s]).wait()

    def scatter_start(tile, s):
        row_loop(lambda r: _row_copy(obuf.at[s], r, out_hbm, dst_ref[tile * MOE_TILE + r], ssem.at[s]).start())

    def scatter_wait(s):
        pltpu.make_async_copy(obuf.at[s], out_hbm.at[pl.ds(0, MOE_TILE)], ssem.at[s]).wait()

    @pl.when(valid & (j == 0))
    def _():
        @pl.when(i == 0)
        def _():
            gather_start(0, 0)
            obuf[1] = jnp.zeros(obuf.shape[1:], F32)
            spare = pltpu.make_async_copy(obuf.at[1], out_hbm.at[pl.ds(out_hbm.shape[0] - MOE_TILE, MOE_TILE)],
                                          ssem.at[1])
            spare.start()
            spare.wait()

        gather_wait(slot)
        nxt = jnp.minimum(i + 1, n_t - 1)

        @pl.when((i + 1 < n_t) & (tv_ref[nxt] == 1))
        def _():
            gather_start(i + 1, 1 - slot)

        hb[...] = xbuf[slot].astype(BF16)
        acc[...] = jnp.zeros_like(acc)

    @pl.when(valid)
    def _():
        h = hb[...]
        hid = _silu(_dot(h, w1_ref[...].astype(BF16))) * _dot(h, w3_ref[...].astype(BF16))
        acc[...] += _dot(hid.astype(BF16), w2_ref[...].astype(BF16))

    @pl.when(valid & (j == n_j - 1))
    def _():
        obuf[slot] = acc[...]

        @pl.when(i >= 1)
        def _():
            scatter_wait(1 - slot)

        scatter_start(i, slot)

        @pl.when(i == n_t - 1)
        def _():
            scatter_wait(slot)

    prev = jnp.maximum(i - 1, 0)

    @pl.when(jnp.logical_not(valid) & (j == 0) & (i >= 1) & (tv_ref[prev] == 1))
    def _():
        scatter_wait(1 - slot)


def _moe_grouped(h, plan, w1, w3, w2, *, layer):
    te, tv, src, dst = plan
    n, d = h.shape
    fe = w1.shape[-1]
    n_tiles = te.shape[0]
    n_f = fe // MOE_FT

    def w13(i, j, te_r, tv_r, s_r, d_r):
        return (layer, te_r[i], 0, j * tv_r[i])

    def w2m(i, j, te_r, tv_r, s_r, d_r):
        return (layer, te_r[i], j * tv_r[i], 0)

    return pl.pallas_call(
        _moe_group_kernel,
        grid_spec=pltpu.PrefetchScalarGridSpec(
            num_scalar_prefetch=4,
            grid=(n_tiles, n_f),
            in_specs=[pl.BlockSpec(memory_space=pl.ANY),
                      pl.BlockSpec((None, None, d, MOE_FT), w13),
                      pl.BlockSpec((None, None, d, MOE_FT), w13),
                      pl.BlockSpec((None, None, MOE_FT, d), w2m)],
            out_specs=pl.BlockSpec(memory_space=pl.ANY),
            scratch_shapes=[pltpu.VMEM((2, MOE_TILE, d), F32), pltpu.VMEM((MOE_TILE, d), BF16),
                            pltpu.VMEM((MOE_TILE, d), F32), pltpu.VMEM((2, MOE_TILE, d), F32),
                            pltpu.SemaphoreType.DMA((2,)), pltpu.SemaphoreType.DMA((2,))],
        ),
        out_shape=jax.ShapeDtypeStruct((2 * n + MOE_TILE, d), F32),
        compiler_params=_cparams(("arbitrary", "arbitrary")),
        name="moe_grouped_ffn",
    )(te, tv, src, dst, h, w1, w3, w2)


def _moe_combine_kernel(y_ref, gm_ref, route_ref, a_ref, b_ref, o_ref):
    route = route_ref[...]
    o_ref[...] = y_ref[...] + gm_ref[...] * (route[:, 2:3] * a_ref[...] + route[:, 3:4] * b_ref[...])


def _moe_combine(y, gmod, route, out01, *, tm):
    n, d = y.shape
    tiles = n // tm
    tiles_per_mod = tiles // gmod.shape[0]
    r = gmod.shape[1]
    blk = pl.BlockSpec((tm, d), lambda i: (i, 0))
    return pl.pallas_call(
        _moe_combine_kernel,
        grid=(tiles,),
        in_specs=[blk, pl.BlockSpec((None, r, d), lambda i: (i // tiles_per_mod, 0, 0)),
                  pl.BlockSpec((tm, LANES), lambda i: (i, 0)), blk,
                  pl.BlockSpec((tm, d), lambda i: (i + tiles, 0))],
        out_specs=blk,
        out_shape=jax.ShapeDtypeStruct((n, d), F32),
        compiler_params=_cparams(("arbitrary",)),
        name="moe_combine",
    )(y, gmod, route, out01, out01)


def _agg_matrix(n_rows, n_cmp, n_sel, width):
    a = np.zeros((n_rows, width), np.float32)
    ratio = SEL_BLOCK // CMP_STRIDE
    for s in range(n_sel):
        for tkn in range(s * ratio - 1, s * ratio + ratio):
            if 0 <= tkn < n_cmp and tkn + 1 < n_rows:
                a[tkn + 1, s] = 1.0
    return jnp.asarray(a, BF16)


def _expand_matrix(t_len):
    e = np.zeros((LANES, t_len), np.float32)
    for s in range(t_len // SEL_BLOCK):
        e[s, s * SEL_BLOCK:(s + 1) * SEL_BLOCK] = 1.0
    return jnp.asarray(e, BF16)


def _compress_weights(l, k_gain, cmp_w1, cmp_b1, cmp_w2, cmp_pe):
    w1 = cmp_w1[l]
    half = CMP_STRIDE * HEAD_DIM
    wa = w1[:, :half].reshape(2, CMP_STRIDE, HEAD_DIM, HEAD_DIM)
    wb = w1[:, half:].reshape(2, CMP_STRIDE, HEAD_DIM, HEAD_DIM)
    z = jnp.zeros_like(wa)
    top = jnp.concatenate([wa, z, wb, z], axis=-1)
    bot = jnp.concatenate([z, wa, z, wb], axis=-1)
    wbd = jnp.concatenate([top, bot], axis=-2).astype(BF16)
    w2 = cmp_w2[l]
    z2 = jnp.zeros_like(w2)
    w2bd = jnp.concatenate([jnp.concatenate([w2, z2], -1), jnp.concatenate([z2, w2], -1)], -2).astype(BF16)
    pe = cmp_pe[l].reshape(2, 2 * half, 1)
    b1 = cmp_b1[l].reshape(2, 1, HEAD_DIM)
    kg0 = jnp.tile(k_gain[l, 0], 2).reshape(1, LANES)
    return wbd, w1, pe, b1, w2bd, kg0


def kernel(x_prompt, x_sample, c_prompt, c_sample, cache_cmp, cache_sel, cache_win, state_conv, page_table, w_ada, b_ada, norm1, norm2, w_in, q_gain, k_gain, cmp_w1, cmp_b1, cmp_w2, cmp_pe, conv_w, conv_b, out_gain, w_out, ffn_w1, ffn_w3, ffn_w2, router_w, moe_w1, moe_w3, moe_w2):
    bp, t_p, d = x_prompt.shape
    bs, t_s, _ = x_sample.shape
    depth = w_in.shape[0]
    n_pool, page = cache_cmp.shape[1], cache_cmp.shape[2]
    n_pages = page_table.shape[1]
    past_len = n_pages * page
    n_p, n_s = bp * t_p, bs * t_s
    win_buf = cache_win.shape[2]
    feat = 2 * KV_DIM
    assert page == LANES and n_pages % PAGES_PER_STEP == 0 and t_p % (PAGES_PER_STEP * LANES) == 0
    n_cmp_rows = past_len // CMP_STRIDE
    assert past_len % SEL_BLOCK == 0 and t_s < CMP_STRIDE
    n_sel_s = past_len // SEL_BLOCK + 1

    cmp_t = jnp.transpose(cache_cmp, (0, 1, 3, 4, 5, 2)).reshape(depth, n_pool, feat, page)
    sel_t = jnp.transpose(cache_sel, (0, 1, 3, 4, 5, 2)).reshape(depth, n_pool, feat, page)
    win_t = jnp.transpose(cache_win, (0, 1, 3, 4, 5, 2)).reshape(depth, bs, feat, win_buf)

    wt_all = jnp.transpose(w_in, (0, 2, 1)).astype(BF16)
    wc_all = (wt_all[:, CONV_OFF:].reshape(depth, 3, CONV_DIM // CONV_TILE, CONV_TILE, d)
              .transpose(0, 2, 1, 3, 4).reshape(depth, 3 * CONV_DIM, d))
    w_out_b = w_out.astype(BF16)
    ffn_b = (ffn_w1.astype(BF16), ffn_w3.astype(BF16), ffn_w2.astype(BF16))
    moe_w = (moe_w1, moe_w3, moe_w2)

    c_all = jnp.concatenate([c_prompt, c_sample, jnp.zeros((16 - bp - bs, d), F32)], axis=0)
    mod = _modulation(c_all, w_ada, b_ada)

    agg_p = _agg_matrix(t_p // CMP_STRIDE, t_p // CMP_STRIDE - 1, t_p // SEL_BLOCK, LANES)
    exp_p = _expand_matrix(t_p)
    sel_w = -(-n_sel_s // LANES) * LANES
    agg_s = _agg_matrix(n_cmp_rows, n_cmp_rows + 3, n_sel_s, sel_w)
    prompt_pages = jnp.broadcast_to(jnp.arange(t_p // LANES, dtype=jnp.int32), (bp, t_p // LANES))
    moe_tiles = (TOP_K * n_p) // MOE_TILE + N_EXPERTS

    yp = x_prompt.reshape(n_p, d)
    ys = x_sample.reshape(n_s, d)
    outs = {k: [] for k in ("cmp_p", "sel_p", "win_p", "conv_p", "cmp_s", "sel_s", "win_s", "conv_s")}
    tm_p = 1024
    n_pick = N_SEL - 1
    for l in range(depth):
        cw = _compress_weights(l, k_gain, cmp_w1, cmp_b1, cmp_w2, cmp_pe)
        qg = jnp.tile(q_gain[l] * ATTN_SCALE, Q_TILE // HEAD_DIM).reshape(1, Q_TILE)
        kg = jnp.broadcast_to(k_gain[l][:, None, :, None], (3, N_KV, HEAD_DIM, 1))
        conv_wl, conv_bl, og = conv_w[l], conv_b[l].reshape(1, CONV_DIM), out_gain[l].reshape(1, d)
        mp = [mod[l, :bp, k * d:(k + 1) * d].reshape(bp, 1, d) for k in range(6)]
        ms = [jnp.repeat(mod[l, bp:bp + bs, k * d:(k + 1) * d], t_s, axis=0).reshape(1, n_s, d) for k in range(6)]
        g1 = norm1[l].reshape(1, d)
        g2 = norm2[l].reshape(1, d)

        qn, kvt, gate, u, bg = _inproj(yp, mp[0], mp[1], g1, wt_all, wc_all, qg, kg, tm=tm_p, n_seq=bp, layer=l)
        kc, vc = _compress(
            kvt, prompt_pages,
            lambda k: pl.BlockSpec((None, feat, LANES), lambda b, g, pt: (b, 0, pt[b, g * PAGES_PER_STEP + k])),
            *cw)
        gate_hm = gate[:, :3 * N_HEADS].reshape(n_p, N_KV, 3 * GQA).transpose(1, 0, 2)
        attn = _prompt_attention(qn, gate_hm, kc, vc, kvt, agg_p, exp_p, tq=256)
        yp = _outproj(attn, u, bg, conv_wl, conv_bl, og, w_out_b, yp, mp[2], tm=tm_p, n_seq=bp, layer=l)
        kv6 = kvt.reshape(bp, 3, 2, N_KV, HEAD_DIM, t_p)
        outs["cmp_p"].append(kv6[:, 0])
        outs["sel_p"].append(kv6[:, 1])
        outs["win_p"].append(kv6[:, 2, ..., t_p - min(WINDOW, t_p):])
        outs["conv_p"].append(u.reshape(bp, t_p, CONV_DIM)[:, t_p - 2:])

        qn_s, kvt_s, gate_s, u_s, bg_s = _inproj(ys, ms[0], ms[1], g1, wt_all, wc_all, qg, kg, tm=n_s, n_seq=1,
                                                 layer=l)
        layer = l
        kc_s, vc_s = _compress(
            cmp_t, page_table,
            lambda k: pl.BlockSpec((None, None, feat, LANES),
                                   lambda b, g, pt: (layer, pt[b, g * PAGES_PER_STEP + k], 0, 0)),
            *cw)
        q5 = qn_s.astype(F32).reshape(bs, t_s, N_KV, GQA, HEAD_DIM)
        qs_sel = q5.transpose(0, 2, 3, 1, 4).reshape(bs, N_KV, GQA * t_s, HEAD_DIM)
        oc, picked = _sample_select(qs_sel, kc_s, vc_s, agg_s, q0=past_len, n_tok=t_s, n_sel=n_sel_s)
        blocks = picked[:, :, :n_pick].reshape(bs, N_KV, t_s, n_pick)
        page_of = page_table[jnp.arange(bs)[:, None, None, None], blocks // 2].reshape(-1)
        half_of = (blocks % 2).reshape(-1)
        pad8 = ((0, 0), (0, 0), (0, 0), (0, 8 - GQA), (0, 0))
        q8 = jnp.pad(q5.transpose(0, 2, 1, 3, 4), pad8)
        oc8 = jnp.pad(oc.reshape(bs, N_KV, GQA, t_s, HEAD_DIM).transpose(0, 1, 3, 2, 4), pad8)
        gate8 = jnp.pad(gate_s[:, :3 * N_HEADS].reshape(bs, t_s, N_KV, GQA, 3).transpose(0, 2, 1, 3, 4), pad8)
        attn8 = _sample_attend(page_of, half_of, q8, oc8, gate8, kvt_s, sel_t, win_t, n_tok=t_s, n_pick=n_pick,
                               layer=l)
        attn_s = attn8[:, :, :, :GQA].transpose(0, 2, 1, 3, 4).reshape(n_s, ATTN_DIM)
        st = state_conv[l]
        zero = jnp.zeros((bs, t_s - 1, CONV_DIM), F32)
        e1 = jnp.concatenate([st[:, 1:2], zero], axis=1).reshape(n_s, CONV_DIM)
        e2 = jnp.concatenate([st[:, 0:1], st[:, 1:2], zero[:, 1:]], axis=1).reshape(n_s, CONV_DIM)
        ys = _outproj(attn_s, u_s, bg_s, conv_wl, conv_bl, og, w_out_b, ys, ms[2], tm=n_s, n_seq=1, layer=l,
                      ext=(e1, e2), n_tok=t_s)
        rows_s = kvt_s[0].T.reshape(bs, t_s, 3, 2, N_KV, HEAD_DIM)
        outs["cmp_s"].append(rows_s[:, :, 0])
        outs["sel_s"].append(rows_s[:, :, 1])
        new_win_t = kvt_s[0, 4 * KV_DIM:].reshape(feat, bs, t_s).transpose(1, 0, 2)
        outs["win_s"].append(jnp.concatenate([win_t[l], new_win_t], axis=-1)[..., t_s:])
        u_ext = jnp.concatenate([st, u_s.reshape(bs, t_s, CONV_DIM)], axis=1)
        outs["conv_s"].append(u_ext[:, t_s:])

        if l % 2 == 0:
            yp = _dense_ffn(yp, mp[3], mp[4], mp[5], g2, *ffn_b, tm=512, layer=l // 2)
            ys = _dense_ffn(ys, ms[3], ms[4], ms[5], g2, *ffn_b, tm=n_s, layer=l // 2)
        else:
            rw = jnp.zeros((LANES, d), F32).at[:N_EXPERTS].set(jnp.transpose(router_w[l // 2]))
            h_p, route = _router(yp, mp[3], mp[4], g2, rw, tm=512)
            out01 = _moe_grouped(h_p, _moe_plan(route, moe_tiles), *moe_w, layer=l // 2)
            yp = _moe_combine(yp, mp[5], route, out01, tm=512)
            ys = _moe_ffn(ys, ms[3], ms[4], ms[5], g2, rw, *moe_w, tm=n_s, layer=l // 2)

    def rows_major(x):
        return jnp.transpose(x, (0, 1, 5, 2, 3, 4))

    new_win_s = jnp.stack(outs["win_s"]).reshape(depth, bs, 2, N_KV, HEAD_DIM, win_buf)
    return (yp.reshape(bp, t_p, d), ys.reshape(bs, t_s, d),
            rows_major(jnp.stack(outs["cmp_p"])), rows_major(jnp.stack(outs["sel_p"])),
            rows_major(jnp.stack(outs["win_p"])), jnp.stack(outs["conv_p"]),
            jnp.stack(outs["cmp_s"]), jnp.stack(outs["sel_s"]),
            rows_major(new_win_s), jnp.stack(outs["conv_s"]))
```

```python
import functools

import numpy as np
import jax
import jax.numpy as jnp
from jax import lax
from jax.experimental import pallas as pl
from jax.experimental.pallas import tpu as pltpu

F32 = jnp.float32
BF16 = jnp.bfloat16

D_MODEL = 2048
HEAD_DIM = 64
ATTN_DIM = 1024
N_HEADS = 16
N_KV = 4
GQA = 4
KV_DIM = 256
CONV_DIM = 1024
CMP_STRIDE = 16
SEL_BLOCK = 64
N_SEL = 16
WINDOW = 512
N_EXPERTS = 8
TOP_K = 2
KV_OFF = ATTN_DIM
GATE_OFF = KV_OFF + 6 * KV_DIM
CONV_OFF = GATE_OFF + 3 * N_HEADS
IN_COLS = CONV_OFF + 3 * CONV_DIM
EPS = 1e-6
NEG_INF = -1e30
FORCE_BONUS = 1e4
ATTN_SCALE = HEAD_DIM ** -0.5
LANES = 128
VMEM_LIMIT = 56 * 1024 * 1024
PAGES_PER_STEP = 16
NT_DIMS = (((1,), (1,)), ((), ()))


def _cparams(sem):
    return pltpu.CompilerParams(dimension_semantics=sem, vmem_limit_bytes=VMEM_LIMIT)


def _dot(a, b):
    return jnp.dot(a, b, preferred_element_type=F32)


def _dot_nt(a, b):
    return lax.dot_general(a, b, NT_DIMS, preferred_element_type=F32)


def _silu(x):
    return x * jax.nn.sigmoid(x)


def _split3_dot(x, w_hi):
    x1 = x.astype(BF16)
    r1 = x - x1.astype(F32)
    x2 = r1.astype(BF16)
    x3 = (r1 - x2.astype(F32)).astype(BF16)
    return _dot(x1, w_hi) + _dot(x2, w_hi) + _dot(x3, w_hi)


def _headnorm_lanes(x, gain):
    lane = lax.broadcasted_iota(jnp.int32, (1, LANES), 1)
    lo = lane < HEAD_DIM
    outs = []
    for c in range(x.shape[-1] // LANES):
        xc = x[:, c * LANES:(c + 1) * LANES]
        x2 = xc * xc
        s_lo = jnp.sum(jnp.where(lo, x2, 0.0), axis=-1, keepdims=True)
        s_hi = jnp.sum(jnp.where(lo, 0.0, x2), axis=-1, keepdims=True)
        r = jnp.where(lo, lax.rsqrt(s_lo * (1.0 / HEAD_DIM) + EPS), lax.rsqrt(s_hi * (1.0 / HEAD_DIM) + EPS))
        outs.append(xc * r)
    return jnp.concatenate(outs, axis=-1) * gain


def _masked_softmax(s, mask):
    s = jnp.where(mask, s, NEG_INF)
    m = jnp.max(s, axis=-1, keepdims=True)
    e = jnp.where(mask, jnp.exp(s - m), 0.0)
    return e / jnp.maximum(jnp.sum(e, axis=-1, keepdims=True), 1e-30)


def _rms_modulate(x, g, shift, scale):
    r = lax.rsqrt(jnp.mean(x * x, axis=-1, keepdims=True) + EPS)
    return (x * r * g) * (1.0 + scale) + shift


def _mod_kernel(c_ref, w_ref, b_ref, o_ref):
    s = _silu(c_ref[...])
    o_ref[...] = _dot(s.astype(BF16), w_ref[...].astype(BF16)) + b_ref[...]


def _modulation(c_all, w_ada, b_ada):
    depth, d, cols = w_ada.shape
    rows = c_all.shape[0]
    tn = 1024
    return pl.pallas_call(
        _mod_kernel,
        grid=(depth, cols // tn),
        in_specs=[pl.BlockSpec((rows, d), lambda l, j: (0, 0)),
                  pl.BlockSpec((None, d, tn), lambda l, j: (l, 0, j)),
                  pl.BlockSpec((None, 1, tn), lambda l, j: (l, 0, j))],
        out_specs=pl.BlockSpec((None, rows, tn), lambda l, j: (l, 0, j)),
        out_shape=jax.ShapeDtypeStruct((depth, rows, cols), F32),
        compiler_params=_cparams(("arbitrary", "arbitrary")),
        name="adaln_modulation",
    )(c_all, w_ada, b_ada.reshape(depth, 1, cols))


Q_TILE = 512
KV_TILE = 512
CONV_TILE = 256
N_Q_STEPS = ATTN_DIM // Q_TILE
N_KV_STEPS = 3
N_CONV_STEPS = CONV_DIM // CONV_TILE
N_IN_STEPS = N_Q_STEPS + N_KV_STEPS + N_CONV_STEPS


def _inproj_kernel(y_ref, sh_ref, sc_ref, g1_ref, wq_ref, wkv_ref, wg_ref, wc_ref, qg_ref, kg_ref,
                   qn_ref, kvt_ref, gate_ref, u_ref, bg_ref, h_scr):
    j = pl.program_id(1)

    @pl.when(j == 0)
    def _():
        h = _rms_modulate(y_ref[...], g1_ref[...], sh_ref[...], sc_ref[...])
        h_scr[...] = h.astype(BF16)
        gate_ref[...] = jax.nn.sigmoid(_dot_nt(h_scr[...], wg_ref[...]))

    @pl.when(j < N_Q_STEPS)
    def _():
        p = _dot_nt(h_scr[...], wq_ref[...])
        qn_ref[...] = _headnorm_lanes(p, qg_ref[...]).astype(BF16)

    @pl.when((j >= N_Q_STEPS) & (j < N_Q_STEPS + N_KV_STEPS))
    def _():
        pt = _dot_nt(wkv_ref[...], h_scr[...])
        tok = pt.shape[-1]
        k3 = pt[:KV_DIM].reshape(N_KV, HEAD_DIM, tok)
        ms = jnp.mean(k3 * k3, axis=1, keepdims=True)
        kn = k3 * lax.rsqrt(ms + EPS) * kg_ref[...]
        k3 = jnp.where(j == N_Q_STEPS, k3, kn)
        kvt_ref[:KV_DIM, :] = k3.reshape(KV_DIM, tok)
        kvt_ref[KV_DIM:, :] = pt[KV_DIM:]

    @pl.when(j >= N_Q_STEPS + N_KV_STEPS)
    def _():
        p = _dot_nt(h_scr[...], wc_ref[...])
        u_ref[...] = p[:, 2 * CONV_TILE:] * p[:, :CONV_TILE]
        bg_ref[...] = p[:, CONV_TILE:2 * CONV_TILE]


def _inproj(y, shift, scale, g1, wt_all, wc_all, qg, kg, *, tm, n_seq, layer):
    n, d = y.shape
    t = n // n_seq
    tiles_per_seq = t // tm
    tiles_per_mod = (n // tm) // shift.shape[0]
    r = shift.shape[1]

    def clamp(j, lo, cnt):
        return jnp.clip(j - lo, 0, cnt - 1)

    q_lo, kv_lo, c_lo = 0, N_Q_STEPS, N_Q_STEPS + N_KV_STEPS
    in_specs = [
        pl.BlockSpec((tm, d), lambda i, j: (i, 0)),
        pl.BlockSpec((None, r, d), lambda i, j: (i // tiles_per_mod, 0, 0)),
        pl.BlockSpec((None, r, d), lambda i, j: (i // tiles_per_mod, 0, 0)),
        pl.BlockSpec((1, d), lambda i, j: (0, 0)),
        pl.BlockSpec((None, Q_TILE, d), lambda i, j: (layer, clamp(j, q_lo, N_Q_STEPS), 0)),
        pl.BlockSpec((None, KV_TILE, d), lambda i, j: (layer, KV_OFF // KV_TILE + clamp(j, kv_lo, N_KV_STEPS), 0)),
        pl.BlockSpec((None, LANES, d), lambda i, j: (layer, GATE_OFF // LANES, 0)),
        pl.BlockSpec((None, 3 * CONV_TILE, d), lambda i, j: (layer, clamp(j, c_lo, N_CONV_STEPS), 0)),
        pl.BlockSpec((1, Q_TILE), lambda i, j: (0, 0)),
        pl.BlockSpec((None, N_KV, HEAD_DIM, 1), lambda i, j: (clamp(j, kv_lo, N_KV_STEPS), 0, 0, 0)),
    ]
    out_specs = [
        pl.BlockSpec((tm, Q_TILE), lambda i, j: (i, clamp(j, q_lo, N_Q_STEPS))),
        pl.BlockSpec((None, KV_TILE, tm),
                     lambda i, j: (i // tiles_per_seq, clamp(j, kv_lo, N_KV_STEPS), i % tiles_per_seq)),
        pl.BlockSpec((tm, LANES), lambda i, j: (i, 0)),
        pl.BlockSpec((tm, CONV_TILE), lambda i, j: (i, clamp(j, c_lo, N_CONV_STEPS))),
        pl.BlockSpec((tm, CONV_TILE), lambda i, j: (i, clamp(j, c_lo, N_CONV_STEPS))),
    ]
    out_shape = [
        jax.ShapeDtypeStruct((n, ATTN_DIM), BF16),
        jax.ShapeDtypeStruct((n_seq, 6 * KV_DIM, t), F32),
        jax.ShapeDtypeStruct((n, LANES), F32),
        jax.ShapeDtypeStruct((n, CONV_DIM), F32),
        jax.ShapeDtypeStruct((n, CONV_DIM), F32),
    ]
    return pl.pallas_call(
        _inproj_kernel,
        grid=(n // tm, N_IN_STEPS),
        in_specs=in_specs,
        out_specs=out_specs,
        out_shape=out_shape,
        scratch_shapes=[pltpu.VMEM((tm, d), BF16)],
        compiler_params=_cparams(("arbitrary", "arbitrary")),
        name="input_projection",
    )(y, shift, scale, g1, wt_all, wt_all, wt_all, wc_all, qg, kg)


def _compress_kernel(pt_ref, *refs):
    del pt_ref
    p = PAGES_PER_STEP
    pages = refs[:p]
    wbd_ref, w1_ref, pe_ref, b1_ref, w2bd_ref, kg_ref, kc_ref, vc_ref, xs, carry, cconst = refs[p:]
    g = pl.program_id(1)
    cpp = LANES // CMP_STRIDE
    m = p * cpp

    @pl.when(g == 0)
    def _():
        carry[...] = jnp.zeros_like(carry)

    @pl.when((g == 0) & (pl.program_id(0) == 0))
    def _():
        for kv in range(2):
            c64 = jnp.sum(pe_ref[kv] * w1_ref[kv], axis=0, keepdims=True) + b1_ref[kv]
            cconst[kv] = jnp.broadcast_to(jnp.concatenate([c64, c64], axis=-1), (8, LANES))

    for k in range(p):
        for jj in range(4):
            tile = pages[k][jj * LANES:(jj + 1) * LANES, :]
            xs[jj, k * LANES:(k + 1) * LANES, :] = tile.T

    row = lax.broadcasted_iota(jnp.int32, (m, 1), 0)
    for jj in range(4):
        kv = jj // 2
        acc = jnp.zeros((m, 2 * LANES), F32)
        for t in range(CMP_STRIDE):
            acc = acc + _dot(xs[jj, pl.ds(t, m, stride=CMP_STRIDE), :].astype(BF16), wbd_ref[kv, t])
        a_part = acc[:, :LANES]
        b_part = acc[:, LANES:]
        prev = carry[jj]
        a_shift = jnp.where(row == 0, prev[7:8, :], pltpu.roll(a_part, 1, axis=0))
        carry[jj] = a_part[m - 8:, :]
        pre = a_shift + b_part + cconst[kv][0:1, :]
        out = _dot(_silu(pre).astype(BF16), w2bd_ref[kv])
        if kv == 0:
            out = _headnorm_lanes(out, kg_ref[...])
        dst = kc_ref if kv == 0 else vc_ref
        hb = (jj % 2) * 2
        dst[hb] = out[:, :HEAD_DIM].astype(BF16)
        dst[hb + 1] = out[:, HEAD_DIM:].astype(BF16)


def _compress(src, page_ids, page_spec_fn, wbd, w1, pe, b1, w2bd, kg0):
    n_seq, n_pages = page_ids.shape
    p = PAGES_PER_STEP
    m = p * (LANES // CMP_STRIDE)
    n_groups = n_pages // p
    n_rows = n_pages * (LANES // CMP_STRIDE)
    page_specs = [page_spec_fn(k) for k in range(p)]

    def const(shape):
        return pl.BlockSpec(shape, lambda b, g, pt: (0,) * len(shape))

    in_specs = page_specs + [const(wbd.shape), const(w1.shape), const(pe.shape), const(b1.shape),
                             const(w2bd.shape), const(kg0.shape)]
    out_spec = pl.BlockSpec((None, N_KV, m, HEAD_DIM), lambda b, g, pt: (b, 0, g, 0))
    return pl.pallas_call(
        _compress_kernel,
        grid_spec=pltpu.PrefetchScalarGridSpec(
            num_scalar_prefetch=1,
            grid=(n_seq, n_groups),
            in_specs=in_specs,
            out_specs=[out_spec, out_spec],
            scratch_shapes=[pltpu.VMEM((4, p * LANES, LANES), F32), pltpu.VMEM((4, 8, LANES), F32),
                            pltpu.VMEM((2, 8, LANES), F32)],
        ),
        out_shape=[jax.ShapeDtypeStruct((n_seq, N_KV, n_rows, HEAD_DIM), BF16)] * 2,
        compiler_params=_cparams(("arbitrary", "arbitrary")),
        name="compress_mlp",
    )(page_ids, *([src] * p), wbd, w1, pe, b1, w2bd, kg0)


SEL_CHUNK = 512


def _attn_kernel(q_ref, gate_ref, kc_ref, vc_ref, kst_ref, vst_ref, kwt_ref, vwt_ref, agg_ref, exp_ref,
                 o_ref, bias_scr, *, tq, t_len):
    i = pl.program_id(2)
    q0 = i * tq
    m_rows = GQA * tq
    n_sel = t_len // SEL_BLOCK
    win_keys = WINDOW + tq
    q = q_ref[...]
    qs = jnp.concatenate([q[:, g * HEAD_DIM:(g + 1) * HEAD_DIM] for g in range(GQA)], axis=0)
    row = lax.broadcasted_iota(jnp.int32, (m_rows, 1), 0)
    qpos = q0 + (row % tq)
    qp = q0 + lax.broadcasted_iota(jnp.int32, (tq, 1), 0)

    s_c = _dot_nt(qs, kc_ref[...])
    r_i = lax.broadcasted_iota(jnp.int32, (1, s_c.shape[-1]), 1)
    mask_c = (r_i >= 1) & (CMP_STRIDE * r_i + (CMP_STRIDE - 1) <= qpos)
    p_c = _masked_softmax(s_c, mask_c)
    o_c = _dot(p_c.astype(BF16), vc_ref[...])
    imp = p_c[0:tq] + p_c[tq:2 * tq] + p_c[2 * tq:3 * tq] + p_c[3 * tq:4 * tq]
    imp_s = _split3_dot(imp, agg_ref[...])

    sidx = lax.broadcasted_iota(jnp.int32, (1, LANES), 1)
    cur = qp // SEL_BLOCK
    forced = (sidx == 0) | (sidx == cur) | (sidx == cur - 1)
    causal = sidx * SEL_BLOCK <= qp
    score = jnp.where(causal, imp_s + jnp.where(forced, FORCE_BONUS, 0.0), NEG_INF)
    sv = score.T[:n_sel]
    srow = lax.broadcasted_iota(jnp.int32, (n_sel, 1), 0)
    cnt = jnp.zeros((n_sel, tq), jnp.int32)
    for t in range(n_sel):
        other = sv[t:t + 1, :]
        beats = (other > sv) | ((other == sv) & (t < srow))
        cnt = cnt + beats.astype(jnp.int32)
    sel_t = jnp.where((cnt < min(N_SEL, n_sel)) & (sv > NEG_INF / 2), 1.0, 0.0)
    sel = jnp.concatenate([sel_t, jnp.zeros((LANES - n_sel, tq), F32)], axis=0).T
    selm = _dot(sel.astype(BF16), exp_ref[...])
    kpos_all = lax.broadcasted_iota(jnp.int32, (1, t_len), 1)
    bias_scr[...] = jnp.where((selm > 0.5) & (kpos_all <= qp), 0.0, NEG_INF)

    n_chunks = (q0 + tq + SEL_CHUNK - 1) // SEL_CHUNK

    def body(c, carry):
        m_old, l_old, acc = carry
        off = pl.multiple_of(c * SEL_CHUNK, SEL_CHUNK)
        kt = kst_ref[:, pl.ds(off, SEL_CHUNK)].astype(BF16)
        s = _dot(qs, kt).reshape(GQA, tq, SEL_CHUNK) + bias_scr[:, pl.ds(off, SEL_CHUNK)][None]
        s = s.reshape(m_rows, SEL_CHUNK)
        m_new = jnp.maximum(m_old, jnp.max(s, axis=-1, keepdims=True))
        alpha = jnp.exp(m_old - m_new)
        pr = jnp.exp(s - m_new)
        l_new = alpha * l_old + jnp.sum(pr, axis=-1, keepdims=True)
        vt = vst_ref[:, pl.ds(off, SEL_CHUNK)].astype(BF16)
        acc = alpha * acc + _dot_nt(pr.astype(BF16), vt)
        return m_new, l_new, acc

    init = (jnp.full((m_rows, 1), NEG_INF, F32), jnp.zeros((m_rows, 1), F32), jnp.zeros((m_rows, HEAD_DIM), F32))
    _, l_s, acc_s = lax.fori_loop(0, n_chunks, body, init)
    o_s = acc_s / jnp.maximum(l_s, 1e-30)

    start = pl.multiple_of(jnp.clip(q0 - WINDOW, 0, t_len - win_keys), LANES)
    dist = qp - (start + lax.broadcasted_iota(jnp.int32, (1, win_keys), 1))
    bias_w = jnp.where((dist >= 0) & (dist < WINDOW), 0.0, NEG_INF)
    s_w = _dot(qs, kwt_ref[:, pl.ds(start, win_keys)].astype(BF16)).reshape(GQA, tq, win_keys) + bias_w[None]
    s_w = s_w.reshape(m_rows, win_keys)
    e_w = jnp.exp(s_w - jnp.max(s_w, axis=-1, keepdims=True))
    p_w = e_w / jnp.maximum(jnp.sum(e_w, axis=-1, keepdims=True), 1e-30)
    o_w = _dot_nt(p_w.astype(BF16), vwt_ref[:, pl.ds(start, win_keys)].astype(BF16))

    gate = gate_ref[...]

    def gcol(jb):
        return jnp.concatenate([jnp.broadcast_to(gate[:, g * 3 + jb:g * 3 + jb + 1], (tq, HEAD_DIM))
                                for g in range(GQA)], axis=0)

    o = gcol(0) * o_c + gcol(1) * o_s + gcol(2) * o_w
    o_ref[...] = jnp.concatenate([o[g * tq:(g + 1) * tq] for g in range(GQA)], axis=-1)


def _prompt_attention(qn, gate_hm, kc, vc, kvt, agg, expand, *, tq):
    n = qn.shape[0]
    n_seq, _, t_len = kvt.shape
    tiles = t_len // tq
    kern = functools.partial(_attn_kernel, tq=tq, t_len=t_len)

    def kv_spec(block_row):
        return pl.BlockSpec((None, HEAD_DIM, t_len), lambda b, h, i: (b, block_row + h, 0))

    sel_k, sel_v = (2 * KV_DIM) // HEAD_DIM, (3 * KV_DIM) // HEAD_DIM
    win_k, win_v = (4 * KV_DIM) // HEAD_DIM, (5 * KV_DIM) // HEAD_DIM
    return pl.pallas_call(
        kern,
        grid=(n_seq, N_KV, tiles),
        in_specs=[
            pl.BlockSpec((tq, GQA * HEAD_DIM), lambda b, h, i: (b * tiles + i, h)),
            pl.BlockSpec((None, tq, 3 * GQA), lambda b, h, i: (h, b * tiles + i, 0)),
            pl.BlockSpec((None, None, kc.shape[2], HEAD_DIM), lambda b, h, i: (b, h, 0, 0)),
            pl.BlockSpec((None, None, vc.shape[2], HEAD_DIM), lambda b, h, i: (b, h, 0, 0)),
            kv_spec(sel_k), kv_spec(sel_v), kv_spec(win_k), kv_spec(win_v),
            pl.BlockSpec(agg.shape, lambda b, h, i: (0, 0)),
            pl.BlockSpec(expand.shape, lambda b, h, i: (0, 0)),
        ],
        out_specs=pl.BlockSpec((tq, GQA * HEAD_DIM), lambda b, h, i: (b * tiles + i, h)),
        out_shape=jax.ShapeDtypeStruct((n, ATTN_DIM), F32),
        scratch_shapes=[pltpu.VMEM((tq, t_len), F32)],
        compiler_params=_cparams(("arbitrary", "arbitrary", "arbitrary")),
        name="prompt_attention",
    )(qn, gate_hm, kc, vc, kvt, kvt, kvt, kvt, agg, expand)


def _sample_select_kernel(qs_ref, kc_ref, vc_ref, agg_ref, oc_ref, idx_ref, *, q0, n_tok, n_sel, n_pick):
    rows = GQA * n_tok
    row = lax.broadcasted_iota(jnp.int32, (rows, 1), 0)
    qpos = q0 + (row % n_tok)
    imps = []
    for h in range(N_KV):
        s_c = _dot_nt(qs_ref[h].astype(BF16), kc_ref[h])
        r_i = lax.broadcasted_iota(jnp.int32, (1, s_c.shape[-1]), 1)
        mask_c = (r_i >= 1) & (CMP_STRIDE * r_i + (CMP_STRIDE - 1) <= qpos)
        p_c = _masked_softmax(s_c, mask_c)
        oc_ref[h] = _dot(p_c.astype(BF16), vc_ref[h])
        imp = p_c[0:n_tok]
        for g in range(1, GQA):
            imp = imp + p_c[g * n_tok:(g + 1) * n_tok]
        imps.append(imp)
    imp_all = jnp.concatenate(imps, axis=0)
    imp_s = _split3_dot(imp_all, agg_ref[...])
    width = imp_s.shape[-1]
    sidx = lax.broadcasted_iota(jnp.int32, (1, width), 1)
    r2 = lax.broadcasted_iota(jnp.int32, (N_KV * n_tok, 1), 0)
    qp = q0 + (r2 % n_tok)
    cur = qp // SEL_BLOCK
    forced = (sidx == 0) | (sidx == cur) | (sidx == cur - 1)
    causal = sidx * SEL_BLOCK <= qp
    score = jnp.where(causal, imp_s + jnp.where(forced, FORCE_BONUS, 0.0), NEG_INF)
    lowest = -3.0e38
    score = jnp.where((sidx >= n_sel - 1), lowest, score)
    out_lane = lax.broadcasted_iota(jnp.int32, (1, LANES), 1)
    picked = jnp.zeros((N_KV * n_tok, LANES), jnp.int32)
    for it in range(n_pick):
        mx = jnp.max(score, axis=-1, keepdims=True)
        first = jnp.min(jnp.where(score == mx, sidx, width), axis=-1, keepdims=True)
        picked = jnp.where(out_lane == it, first, picked)
        score = jnp.where(sidx == first, lowest, score)
    idx_ref[...] = picked


def _sample_select(qs, kc, vc, agg, *, q0, n_tok, n_sel):
    n_seq = qs.shape[0]
    n_pick = N_SEL - 1
    kern = functools.partial(_sample_select_kernel, q0=q0, n_tok=n_tok, n_sel=n_sel, n_pick=n_pick)
    return pl.pallas_call(
        kern,
        grid=(n_seq,),
        in_specs=[
            pl.BlockSpec((None, N_KV, GQA * n_tok, HEAD_DIM), lambda b: (b, 0, 0, 0)),
            pl.BlockSpec((None, N_KV, kc.shape[2], HEAD_DIM), lambda b: (b, 0, 0, 0)),
            pl.BlockSpec((None, N_KV, vc.shape[2], HEAD_DIM), lambda b: (b, 0, 0, 0)),
            pl.BlockSpec(agg.shape, lambda b: (0, 0)),
        ],
        out_specs=[
            pl.BlockSpec((None, N_KV, GQA * n_tok, HEAD_DIM), lambda b: (b, 0, 0, 0)),
            pl.BlockSpec((None, N_KV * n_tok, LANES), lambda b: (b, 0, 0)),
        ],
        out_shape=[
            jax.ShapeDtypeStruct((n_seq, N_KV, GQA * n_tok, HEAD_DIM), F32),
            jax.ShapeDtypeStruct((n_seq, N_KV * n_tok, LANES), jnp.int32),
        ],
        compiler_params=_cparams(("arbitrary",)),
        name="sample_select",
    )(qs, kc, vc, agg)


def _sample_attend_kernel(pg_ref, half_ref, q_ref, oc_ref, gate_ref, knew_ref, vnew_ref, kwp_ref, vwp_ref,
                          kwn_ref, vwn_ref, sel_hbm, o_ref, kbuf, vbuf, sem, *, n_tok, n_pick, layer):
    n_pages = n_tok * n_pick
    b = pl.program_id(0)
    h = pl.program_id(1)
    step = b * N_KV + h
    n_steps = pl.num_programs(0) * N_KV
    slot = step % 2

    def gather_start(stp, s):
        row0 = pl.multiple_of((stp % N_KV) * HEAD_DIM, HEAD_DIM)
        for tk in range(n_pages):
            page = pg_ref[stp * n_pages + tk]
            pltpu.make_async_copy(sel_hbm.at[layer, page, pl.ds(row0, HEAD_DIM)], kbuf.at[s, tk], sem.at[s]).start()
            pltpu.make_async_copy(sel_hbm.at[layer, page, pl.ds(KV_DIM + row0, HEAD_DIM)], vbuf.at[s, tk],
                                  sem.at[s]).start()

    @pl.when(step == 0)
    def _():
        gather_start(0, 0)

    @pl.when(step + 1 < n_steps)
    def _():
        gather_start(step + 1, 1 - slot)

    for buf in (kbuf, vbuf):
        pltpu.make_async_copy(sel_hbm.at[layer, pl.ds(0, n_pages), pl.ds(0, HEAD_DIM)], buf.at[slot], sem.at[slot]).wait()
    lane = lax.broadcasted_iota(jnp.int32, (1, LANES), 1)
    n_new = knew_ref.shape[-1]
    new_lane = lax.broadcasted_iota(jnp.int32, (1, n_new), 1)
    n_win = kwp_ref.shape[-1]
    wl = lax.broadcasted_iota(jnp.int32, (1, n_win), 1)
    k_new, v_new = knew_ref[...].astype(BF16), vnew_ref[...].astype(BF16)
    kw_past, vw_past = kwp_ref[...].astype(BF16), vwp_ref[...].astype(BF16)
    kw_new, vw_new = kwn_ref[...].astype(BF16), vwn_ref[...].astype(BF16)

    for t in range(n_tok):
        q = q_ref[t].astype(BF16)
        new_mask = (new_lane // n_tok == b) & (new_lane % n_tok <= t)

        flat = ((b * N_KV + h) * n_tok + t) * n_pick
        scores, masks = [], []
        for k in range(n_pick):
            half = half_ref[flat + k]
            scores.append(_dot(q, kbuf[slot, t * n_pick + k].astype(BF16)))
            masks.append(lane // SEL_BLOCK == half)
        s_new = _dot(q, k_new)
        mx = jnp.max(jnp.where(new_mask, s_new, NEG_INF), axis=-1, keepdims=True)
        for s, mk in zip(scores, masks):
            mx = jnp.maximum(mx, jnp.max(jnp.where(mk, s, NEG_INF), axis=-1, keepdims=True))
        e_new = jnp.exp(jnp.where(new_mask, s_new, NEG_INF) - mx) * new_mask.astype(F32)
        den = jnp.sum(e_new, axis=-1, keepdims=True)
        acc = _dot_nt(e_new.astype(BF16), v_new)
        for k in range(n_pick):
            e = jnp.exp(jnp.where(masks[k], scores[k], NEG_INF) - mx) * masks[k].astype(F32)
            den = den + jnp.sum(e, axis=-1, keepdims=True)
            acc = acc + _dot_nt(e.astype(BF16), vbuf[slot, t * n_pick + k].astype(BF16))
        o_s = acc / jnp.maximum(den, 1e-30)

        past_mask = (n_win - wl + t) < WINDOW
        s_p = jnp.where(past_mask, _dot(q, kw_past), NEG_INF)
        s_n = jnp.where(new_mask, _dot(q, kw_new), NEG_INF)
        mw = jnp.maximum(jnp.max(s_p, axis=-1, keepdims=True), jnp.max(s_n, axis=-1, keepdims=True))
        e_p = jnp.exp(s_p - mw) * past_mask.astype(F32)
        e_n = jnp.exp(s_n - mw) * new_mask.astype(F32)
        den_w = jnp.sum(e_p, axis=-1, keepdims=True) + jnp.sum(e_n, axis=-1, keepdims=True)
        acc_w = _dot_nt(e_p.astype(BF16), vw_past) + _dot_nt(e_n.astype(BF16), vw_new)
        o_w = acc_w / jnp.maximum(den_w, 1e-30)

        gate = gate_ref[t]
        o_ref[t] = gate[:, 0:1] * oc_ref[t] + gate[:, 1:2] * o_s + gate[:, 2:3] * o_w


def _sample_attend(page_of, half_of, q8, oc8, gate8, kvt_new, cache_sel_t, cache_win_t, *, n_tok, n_pick, layer):
    n_seq = q8.shape[0]
    kern = functools.partial(_sample_attend_kernel, n_tok=n_tok, n_pick=n_pick, layer=layer)
    n_new = kvt_new.shape[-1]
    n_win = cache_win_t.shape[-1]
    n_pages = n_tok * n_pick
    assert cache_sel_t.shape[1] >= n_pages

    def qspec(last):
        return pl.BlockSpec((None, None, n_tok, 8, last), lambda b, h, pg, hf: (b, h, 0, 0, 0))

    def new_spec(block_row):
        return pl.BlockSpec((None, HEAD_DIM, n_new), lambda b, h, pg, hf: (0, block_row + h, 0))

    def win_spec(block_row):
        return pl.BlockSpec((None, None, HEAD_DIM, n_win), lambda b, h, pg, hf: (layer, b, block_row + h, 0))

    sel_k, sel_v = (2 * KV_DIM) // HEAD_DIM, (3 * KV_DIM) // HEAD_DIM
    win_k, win_v = (4 * KV_DIM) // HEAD_DIM, (5 * KV_DIM) // HEAD_DIM
    in_specs = [qspec(HEAD_DIM), qspec(HEAD_DIM), qspec(3), new_spec(sel_k), new_spec(sel_v),
                win_spec(0), win_spec(N_KV), new_spec(win_k), new_spec(win_v), pl.BlockSpec(memory_space=pl.ANY)]
    return pl.pallas_call(
        kern,
        grid_spec=pltpu.PrefetchScalarGridSpec(
            num_scalar_prefetch=2,
            grid=(n_seq, N_KV),
            in_specs=in_specs,
            out_specs=qspec(HEAD_DIM),
            scratch_shapes=[pltpu.VMEM((2, n_pages, HEAD_DIM, LANES), F32), pltpu.VMEM((2, n_pages, HEAD_DIM, LANES), F32),
                            pltpu.SemaphoreType.DMA((2,))],
        ),
        out_shape=jax.ShapeDtypeStruct(q8.shape, F32),
        compiler_params=_cparams(("arbitrary", "arbitrary")),
        name="sample_attend",
    )(page_of, half_of, q8, oc8, gate8, kvt_new, kvt_new, cache_win_t, cache_win_t, kvt_new, kvt_new, cache_sel_t)


def _mix_prologue(attn, u, u1, u2, bg, cw_ref, cb_ref, og_ref, o_scr):
    og = og_ref[...]
    an = attn * lax.rsqrt(jnp.mean(attn * attn, axis=-1, keepdims=True) + EPS) * og[:, :ATTN_DIM]
    yv = cb_ref[...] + cw_ref[0:1, :] * u2
    yv = yv + cw_ref[1:2, :] * u1
    yv = yv + cw_ref[2:3, :] * u
    cv = bg * yv
    cn = cv * lax.rsqrt(jnp.mean(cv * cv, axis=-1, keepdims=True) + EPS) * og[:, ATTN_DIM:]
    o_scr[:, :ATTN_DIM] = an.astype(BF16)
    o_scr[:, ATTN_DIM:] = cn.astype(BF16)


def _outproj_seq_kernel(attn_ref, u_ref, halo_ref, bg_ref, cw_ref, cb_ref, og_ref, w_ref, y_ref, gm_ref,
                        o_ref, o_scr, *, tiles_per_seq):
    i = pl.program_id(0)

    @pl.when(pl.program_id(1) == 0)
    def _():
        u = u_ref[...]
        tm = u.shape[0]
        row = lax.broadcasted_iota(jnp.int32, (tm, 1), 0)
        halo = jnp.where(i % tiles_per_seq == 0, 0.0, halo_ref[...])
        p1 = halo[7:8, :]
        p2 = halo[6:7, :]
        u1 = jnp.where(row == 0, p1, pltpu.roll(u, 1, axis=0))
        u2 = jnp.where(row == 0, p2, jnp.where(row == 1, p1, pltpu.roll(u, 2, axis=0)))
        _mix_prologue(attn_ref[...], u, u1, u2, bg_ref[...], cw_ref, cb_ref, og_ref, o_scr)

    o_ref[...] = y_ref[...] + gm_ref[...] * _dot(o_scr[...], w_ref[...])


def _outproj_state_kernel(attn_ref, u_ref, e1_ref, e2_ref, bg_ref, cw_ref, cb_ref, og_ref, w_ref, y_ref, gm_ref,
                          o_ref, o_scr, *, n_tok):
    @pl.when(pl.program_id(1) == 0)
    def _():
        u = u_ref[...]
        tm = u.shape[0]
        k = lax.broadcasted_iota(jnp.int32, (tm, 1), 0) % n_tok
        u1 = jnp.where(k >= 1, pltpu.roll(u, 1, axis=0), e1_ref[...])
        u2 = jnp.where(k >= 2, pltpu.roll(u, 2, axis=0), e2_ref[...])
        _mix_prologue(attn_ref[...], u, u1, u2, bg_ref[...], cw_ref, cb_ref, og_ref, o_scr)

    o_ref[...] = y_ref[...] + gm_ref[...] * _dot(o_scr[...], w_ref[...])


def _outproj(attn, u, bg, cw, cb, og, w, y, gmod, *, tm, n_seq, layer, ext=None, n_tok=None):
    n, d = y.shape
    tn = 512
    tiles_per_seq = (n // n_seq) // tm
    tiles_per_mod = (n // tm) // gmod.shape[0]
    r = gmod.shape[1]
    full = lambda i, j: (i, 0)
    c0 = lambda i, j: (0, 0)
    common = [pl.BlockSpec((tm, CONV_DIM), full), pl.BlockSpec((3, CONV_DIM), c0), pl.BlockSpec((1, CONV_DIM), c0),
              pl.BlockSpec((1, d), c0), pl.BlockSpec((None, d, tn), lambda i, j: (layer, 0, j)),
              pl.BlockSpec((tm, tn), lambda i, j: (i, j)),
              pl.BlockSpec((None, r, tn), lambda i, j: (i // tiles_per_mod, 0, j))]
    head = [pl.BlockSpec((tm, ATTN_DIM), full), pl.BlockSpec((tm, CONV_DIM), full)]
    if ext is None:
        kern = functools.partial(_outproj_seq_kernel, tiles_per_seq=tiles_per_seq)
        extra = [pl.BlockSpec((8, CONV_DIM), lambda i, j: (jnp.maximum(i * (tm // 8) - 1, 0), 0))]
        args = (attn, u, u, bg, cw, cb, og, w, y, gmod)
    else:
        kern = functools.partial(_outproj_state_kernel, n_tok=n_tok)
        extra = [pl.BlockSpec((tm, CONV_DIM), full), pl.BlockSpec((tm, CONV_DIM), full)]
        args = (attn, u, ext[0], ext[1], bg, cw, cb, og, w, y, gmod)
    return pl.pallas_call(
        kern,
        grid=(n // tm, d // tn),
        in_specs=head + extra + common,
        out_specs=pl.BlockSpec((tm, tn), lambda i, j: (i, j)),
        out_shape=jax.ShapeDtypeStruct((n, d), F32),
        scratch_shapes=[pltpu.VMEM((tm, d), BF16)],
        compiler_params=_cparams(("arbitrary", "arbitrary")),
        name="output_projection",
    )(*args)


def _ffn_kernel(y_ref, sh_ref, sc_ref, gm_ref, g2_ref, w1_ref, w3_ref, w2_ref, o_ref, h_scr, acc):
    j = pl.program_id(1)

    @pl.when(j == 0)
    def _():
        h_scr[...] = _rms_modulate(y_ref[...], g2_ref[...], sh_ref[...], sc_ref[...]).astype(BF16)
        acc[...] = jnp.zeros_like(acc)

    h = h_scr[...]
    hid = _silu(_dot(h, w1_ref[...])) * _dot(h, w3_ref[...])
    acc[...] += _dot(hid.astype(BF16), w2_ref[...])

    @pl.when(j == pl.num_programs(1) - 1)
    def _():
        o_ref[...] = y_ref[...] + gm_ref[...] * acc[...]


def _dense_ffn(y, shift, scale, gmod, g2, w1, w3, w2, *, tm, layer):
    n, d = y.shape
    f = w1.shape[-1]
    tf = 512
    tiles_per_mod = (n // tm) // shift.shape[0]
    r = shift.shape[1]
    mod = pl.BlockSpec((None, r, d), lambda i, j: (i // tiles_per_mod, 0, 0))
    return pl.pallas_call(
        _ffn_kernel,
        grid=(n // tm, f // tf),
        in_specs=[pl.BlockSpec((tm, d), lambda i, j: (i, 0)), mod, mod, mod,
                  pl.BlockSpec((1, d), lambda i, j: (0, 0)),
                  pl.BlockSpec((None, d, tf), lambda i, j: (layer, 0, j)),
                  pl.BlockSpec((None, d, tf), lambda i, j: (layer, 0, j)),
                  pl.BlockSpec((None, tf, d), lambda i, j: (layer, j, 0))],
        out_specs=pl.BlockSpec((tm, d), lambda i, j: (i, 0)),
        out_shape=jax.ShapeDtypeStruct((n, d), F32),
        scratch_shapes=[pltpu.VMEM((tm, d), BF16), pltpu.VMEM((tm, d), F32)],
        compiler_params=_cparams(("arbitrary", "arbitrary")),
        name="dense_ffn",
    )(y, shift, scale, gmod, g2, w1, w3, w2)


def _top2_route(h, h_bf, rw):
    lane = lax.broadcasted_iota(jnp.int32, (1, LANES), 1)
    rw_hi = rw.astype(BF16)
    rw_lo = (rw - rw_hi.astype(F32)).astype(BF16)
    h_lo = (h - h_bf.astype(F32)).astype(BF16)
    logits = _dot_nt(h_bf, rw_hi) + _dot_nt(h_lo, rw_hi) + _dot_nt(h_bf, rw_lo)
    lowest = -3.0e38
    logits = jnp.where(lane < N_EXPERTS, logits, lowest)
    m1 = jnp.max(logits, axis=-1, keepdims=True)
    i1 = jnp.min(jnp.where(logits == m1, lane, LANES), axis=-1, keepdims=True)
    rest = jnp.where(lane == i1, lowest, logits)
    m2 = jnp.max(rest, axis=-1, keepdims=True)
    i2 = jnp.min(jnp.where(rest == m2, lane, LANES), axis=-1, keepdims=True)
    e2 = jnp.exp(m2 - m1)
    den = 1.0 + e2
    return i1, i2, 1.0 / den, e2 / den


def _moe_kernel(y_ref, sh_ref, sc_ref, gm_ref, g2_ref, rw_ref, w1_ref, w3_ref, w2_ref, o_ref, h_scr, acc, comb):
    e = pl.program_id(1)
    j = pl.program_id(2)
    lane = lax.broadcasted_iota(jnp.int32, (1, LANES), 1)

    @pl.when((e == 0) & (j == 0))
    def _():
        h = _rms_modulate(y_ref[...], g2_ref[...], sh_ref[...], sc_ref[...])
        h_scr[...] = h.astype(BF16)
        acc[...] = jnp.zeros_like(acc)
        i1, i2, w1, w2 = _top2_route(h, h_scr[...], rw_ref[...])
        comb[...] = jnp.where(lane == i1, w1, 0.0) + jnp.where(lane == i2, w2, 0.0)

    h = h_scr[...]
    hid = _silu(_dot(h, w1_ref[...].astype(BF16))) * _dot(h, w3_ref[...].astype(BF16))
    ce = jnp.sum(jnp.where(lane == e, comb[...], 0.0), axis=-1, keepdims=True)
    acc[...] += ce * _dot(hid.astype(BF16), w2_ref[...].astype(BF16))

    @pl.when((e == pl.num_programs(1) - 1) & (j == pl.num_programs(2) - 1))
    def _():
        o_ref[...] = y_ref[...] + gm_ref[...] * acc[...]


def _moe_ffn(y, shift, scale, gmod, g2, rw, w1, w3, w2, *, tm, layer):
    n, d = y.shape
    _, n_e, _, fe = w1.shape
    tf = 256
    tiles_per_mod = (n // tm) // shift.shape[0]
    r = shift.shape[1]
    mod = pl.BlockSpec((None, r, d), lambda i, e, j: (i // tiles_per_mod, 0, 0))
    return pl.pallas_call(
        _moe_kernel,
        grid=(n // tm, n_e, fe // tf),
        in_specs=[pl.BlockSpec((tm, d), lambda i, e, j: (i, 0)), mod, mod, mod,
                  pl.BlockSpec((1, d), lambda i, e, j: (0, 0)),
                  pl.BlockSpec((LANES, d), lambda i, e, j: (0, 0)),
                  pl.BlockSpec((None, None, d, tf), lambda i, e, j: (layer, e, 0, j)),
                  pl.BlockSpec((None, None, d, tf), lambda i, e, j: (layer, e, 0, j)),
                  pl.BlockSpec((None, None, tf, d), lambda i, e, j: (layer, e, j, 0))],
        out_specs=pl.BlockSpec((tm, d), lambda i, e, j: (i, 0)),
        out_shape=jax.ShapeDtypeStruct((n, d), F32),
        scratch_shapes=[pltpu.VMEM((tm, d), BF16), pltpu.VMEM((tm, d), F32), pltpu.VMEM((tm, LANES), F32)],
        compiler_params=_cparams(("arbitrary", "arbitrary", "arbitrary")),
        name="moe_ffn",
    )(y, shift, scale, gmod, g2, rw, w1, w3, w2)


MOE_TILE = 512
MOE_FT = 256


def _router_kernel(y_ref, sh_ref, sc_ref, g2_ref, rw_ref, h_ref, route_ref):
    h = _rms_modulate(y_ref[...], g2_ref[...], sh_ref[...], sc_ref[...])
    h_ref[...] = h
    i1, i2, w1, w2 = _top2_route(h, h.astype(BF16), rw_ref[...])
    lane = lax.broadcasted_iota(jnp.int32, (1, LANES), 1)
    route_ref[...] = (jnp.where(lane == 0, i1.astype(F32), 0.0) + jnp.where(lane == 1, i2.astype(F32), 0.0)
                      + jnp.where(lane == 2, w1, 0.0) + jnp.where(lane == 3, w2, 0.0))


def _router(y, shift, scale, g2, rw, *, tm):
    n, d = y.shape
    tiles_per_mod = (n // tm) // shift.shape[0]
    r = shift.shape[1]
    mod = pl.BlockSpec((None, r, d), lambda i: (i // tiles_per_mod, 0, 0))
    return pl.pallas_call(
        _router_kernel,
        grid=(n // tm,),
        in_specs=[pl.BlockSpec((tm, d), lambda i: (i, 0)), mod, mod,
                  pl.BlockSpec((1, d), lambda i: (0, 0)), pl.BlockSpec((LANES, d), lambda i: (0, 0))],
        out_specs=[pl.BlockSpec((tm, d), lambda i: (i, 0)), pl.BlockSpec((tm, LANES), lambda i: (i, 0))],
        out_shape=[jax.ShapeDtypeStruct((n, d), F32), jax.ShapeDtypeStruct((n, LANES), F32)],
        compiler_params=_cparams(("arbitrary",)),
        name="moe_router",
    )(y, shift, scale, g2, rw)


def _moe_plan(route, n_tiles):
    n = route.shape[0]
    flat_e = route[:, :2].astype(jnp.int32).reshape(-1)
    onehot = (flat_e[:, None] == jnp.arange(N_EXPERTS, dtype=jnp.int32)[None, :]).astype(jnp.int32)
    csum = jnp.cumsum(onehot, axis=0)
    pos = jnp.sum((csum - 1) * onehot, axis=1)
    sizes = csum[-1]
    padded = ((sizes + MOE_TILE - 1) // MOE_TILE) * MOE_TILE
    ends = jnp.cumsum(padded)
    dest = (ends - padded)[flat_e] + pos
    rows = n_tiles * MOE_TILE
    a_of_row = jnp.full((rows,), -1, jnp.int32).at[dest].set(jnp.arange(2 * n, dtype=jnp.int32),
                                                             unique_indices=True)
    real = a_of_row >= 0
    a0 = jnp.maximum(a_of_row, 0)
    src = a0 // 2
    spare = 2 * n + jnp.arange(rows, dtype=jnp.int32) % MOE_TILE
    dst = jnp.where(real, (a0 % 2) * n + a0 // 2, spare)
    tile_start = jnp.arange(n_tiles, dtype=jnp.int32) * MOE_TILE
    te = jnp.sum((tile_start[:, None] >= ends[None, :]).astype(jnp.int32), axis=1)
    tv = (te < N_EXPERTS).astype(jnp.int32)
    return jnp.minimum(te, N_EXPERTS - 1), tv, src, dst


def _row_copy(src_ref, src_row, dst_ref, dst_row, sem):
    return pltpu.make_async_copy(src_ref.at[pl.ds(src_row, 1)], dst_ref.at[pl.ds(dst_row, 1)], sem)


ROW_LOOP_UNROLL = 8


def _moe_group_kernel(te_ref, tv_ref, src_ref, dst_ref, h_hbm, w1a_ref, w3a_ref, w2a_ref, w1b_ref, w3b_ref, w2b_ref,
                      out_hbm, xbuf, hb, acc, obuf, gsem, ssem, *, n_f):
    del te_ref
    i = pl.program_id(0)
    j = pl.program_id(1)
    n_t = pl.num_programs(0)
    n_j = pl.num_programs(1)
    valid = tv_ref[i] == 1
    slot = i % 2

    def row_loop(body):
        def step(r, c):
            body(r)
            return c
        lax.fori_loop(0, MOE_TILE, step, 0, unroll=ROW_LOOP_UNROLL)

    def gather_start(tile, s):
        row_loop(lambda r: _row_copy(h_hbm, src_ref[tile * MOE_TILE + r], xbuf.at[s], r, gsem.at[s]).start())

    def gather_wait(s):
        pltpu.make_async_copy(h_hbm.at[pl.ds(0, MOE_TILE)], xbuf.at[s], gsem.at[s]).wait()

    def scatter_start(tile, s):
        row_loop(lambda r: _row_copy(obuf.at[s], r, out_hbm, dst_ref[tile * MOE_TILE + r], ssem.at[s]).start())

    def scatter_wait(s):
        pltpu.make_async_copy(obuf.at[s], out_hbm.at[pl.ds(0, MOE_TILE)], ssem.at[s]).wait()

    @pl.when(valid & (j == 0))
    def _():
        @pl.when(i == 0)
        def _():
            gather_start(0, 0)
            obuf[1] = jnp.zeros(obuf.shape[1:], F32)
            spare = pltpu.make_async_copy(obuf.at[1], out_hbm.at[pl.ds(out_hbm.shape[0] - MOE_TILE, MOE_TILE)],
                                          ssem.at[1])
            spare.start()
            spare.wait()

        gather_wait(slot)
        nxt = jnp.minimum(i + 1, n_t - 1)

        @pl.when((i + 1 < n_t) & (tv_ref[nxt] == 1))
        def _():
            gather_start(i + 1, 1 - slot)

        hb[...] = xbuf[slot].astype(BF16)
        acc[...] = jnp.zeros_like(acc)

    def expert_block(w1_ref, w3_ref, w2_ref):
        h = hb[...]
        hid = _silu(_dot(h, w1_ref[...].astype(BF16))) * _dot(h, w3_ref[...].astype(BF16))
        return _dot(hid.astype(BF16), w2_ref[...].astype(BF16))

    second = 2 * j + 1 < n_f

    @pl.when(valid & second)
    def _():
        acc[...] += expert_block(w1a_ref, w3a_ref, w2a_ref) + expert_block(w1b_ref, w3b_ref, w2b_ref)

    @pl.when(valid & jnp.logical_not(second))
    def _():
        acc[...] += expert_block(w1a_ref, w3a_ref, w2a_ref)

    @pl.when(valid & (j == n_j - 1))
    def _():
        obuf[slot] = acc[...]

        @pl.when(i >= 1)
        def _():
            scatter_wait(1 - slot)

        scatter_start(i, slot)

        @pl.when(i == n_t - 1)
        def _():
            scatter_wait(slot)

    prev = jnp.maximum(i - 1, 0)

    @pl.when(jnp.logical_not(valid) & (j == 0) & (i >= 1) & (tv_ref[prev] == 1))
    def _():
        scatter_wait(1 - slot)


def _moe_grouped(h, plan, w1, w3, w2, *, layer):
    te, tv, src, dst = plan
    n, d = h.shape
    fe = w1.shape[-1]
    n_tiles = te.shape[0]
    n_f = fe // MOE_FT

    def blk(j, tv_i, half):
        return jnp.minimum(2 * j + half, n_f - 1) * tv_i

    def w13(half):
        return pl.BlockSpec((None, None, d, MOE_FT),
                            lambda i, j, te_r, tv_r, s_r, d_r: (layer, te_r[i], 0, blk(j, tv_r[i], half)))

    def w2m(half):
        return pl.BlockSpec((None, None, MOE_FT, d),
                            lambda i, j, te_r, tv_r, s_r, d_r: (layer, te_r[i], blk(j, tv_r[i], half), 0))

    return pl.pallas_call(
        functools.partial(_moe_group_kernel, n_f=n_f),
        grid_spec=pltpu.PrefetchScalarGridSpec(
            num_scalar_prefetch=4,
            grid=(n_tiles, (n_f + 1) // 2),
            in_specs=[pl.BlockSpec(memory_space=pl.ANY), w13(0), w13(0), w2m(0), w13(1), w13(1), w2m(1)],
            out_specs=pl.BlockSpec(memory_space=pl.ANY),
            scratch_shapes=[pltpu.VMEM((2, MOE_TILE, d), F32), pltpu.VMEM((MOE_TILE, d), BF16),
                            pltpu.VMEM((MOE_TILE, d), F32), pltpu.VMEM((2, MOE_TILE, d), F32),
                            pltpu.SemaphoreType.DMA((2,)), pltpu.SemaphoreType.DMA((2,))],
        ),
        out_shape=jax.ShapeDtypeStruct((2 * n + MOE_TILE, d), F32),
        compiler_params=_cparams(("arbitrary", "arbitrary")),
        name="moe_grouped_ffn",
    )(te, tv, src, dst, h, w1, w3, w2, w1, w3, w2)


def _moe_combine_kernel(y_ref, gm_ref, route_ref, a_ref, b_ref, o_ref):
    route = route_ref[...]
    o_ref[...] = y_ref[...] + gm_ref[...] * (route[:, 2:3] * a_ref[...] + route[:, 3:4] * b_ref[...])


def _moe_combine(y, gmod, route, out01, *, tm):
    n, d = y.shape
    tiles = n // tm
    tiles_per_mod = tiles // gmod.shape[0]
    r = gmod.shape[1]
    blk = pl.BlockSpec((tm, d), lambda i: (i, 0))
    return pl.pallas_call(
        _moe_combine_kernel,
        grid=(tiles,),
        in_specs=[blk, pl.BlockSpec((None, r, d), lambda i: (i // tiles_per_mod, 0, 0)),
                  pl.BlockSpec((tm, LANES), lambda i: (i, 0)), blk,
                  pl.BlockSpec((tm, d), lambda i: (i + tiles, 0))],
        out_specs=blk,
        out_shape=jax.ShapeDtypeStruct((n, d), F32),
        compiler_params=_cparams(("arbitrary",)),
        name="moe_combine",
    )(y, gmod, route, out01, out01)


def _agg_matrix(n_rows, n_cmp, n_sel, width):
    a = np.zeros((n_rows, width), np.float32)
    ratio = SEL_BLOCK // CMP_STRIDE
    for s in range(n_sel):
        for tkn in range(s * ratio - 1, s * ratio + ratio):
            if 0 <= tkn < n_cmp and tkn + 1 < n_rows:
                a[tkn + 1, s] = 1.0
    return jnp.asarray(a, BF16)


def _expand_matrix(t_len):
    e = np.zeros((LANES, t_len), np.float32)
    for s in range(t_len // SEL_BLOCK):
        e[s, s * SEL_BLOCK:(s + 1) * SEL_BLOCK] = 1.0
    return jnp.asarray(e, BF16)


def _compress_weights(l, k_gain, cmp_w1, cmp_b1, cmp_w2, cmp_pe):
    w1 = cmp_w1[l]
    half = CMP_STRIDE * HEAD_DIM
    wa = w1[:, :half].reshape(2, CMP_STRIDE, HEAD_DIM, HEAD_DIM)
    wb = w1[:, half:].reshape(2, CMP_STRIDE, HEAD_DIM, HEAD_DIM)
    z = jnp.zeros_like(wa)
    top = jnp.concatenate([wa, z, wb, z], axis=-1)
    bot = jnp.concatenate([z, wa, z, wb], axis=-1)
    wbd = jnp.concatenate([top, bot], axis=-2).astype(BF16)
    w2 = cmp_w2[l]
    z2 = jnp.zeros_like(w2)
    w2bd = jnp.concatenate([jnp.concatenate([w2, z2], -1), jnp.concatenate([z2, w2], -1)], -2).astype(BF16)
    pe = cmp_pe[l].reshape(2, 2 * half, 1)
    b1 = cmp_b1[l].reshape(2, 1, HEAD_DIM)
    kg0 = jnp.tile(k_gain[l, 0], 2).reshape(1, LANES)
    return wbd, w1, pe, b1, w2bd, kg0


def kernel(x_prompt, x_sample, c_prompt, c_sample, cache_cmp, cache_sel, cache_win, state_conv, page_table, w_ada, b_ada, norm1, norm2, w_in, q_gain, k_gain, cmp_w1, cmp_b1, cmp_w2, cmp_pe, conv_w, conv_b, out_gain, w_out, ffn_w1, ffn_w3, ffn_w2, router_w, moe_w1, moe_w3, moe_w2):
    bp, t_p, d = x_prompt.shape
    bs, t_s, _ = x_sample.shape
    depth = w_in.shape[0]
    n_pool, page = cache_cmp.shape[1], cache_cmp.shape[2]
    n_pages = page_table.shape[1]
    past_len = n_pages * page
    n_p, n_s = bp * t_p, bs * t_s
    win_buf = cache_win.shape[2]
    feat = 2 * KV_DIM
    assert page == LANES and n_pages % PAGES_PER_STEP == 0 and t_p % (PAGES_PER_STEP * LANES) == 0
    n_cmp_rows = past_len // CMP_STRIDE
    assert past_len % SEL_BLOCK == 0 and t_s < CMP_STRIDE
    n_sel_s = past_len // SEL_BLOCK + 1

    cmp_t = jnp.transpose(cache_cmp, (0, 1, 3, 4, 5, 2)).reshape(depth, n_pool, feat, page)
    sel_t = jnp.transpose(cache_sel, (0, 1, 3, 4, 5, 2)).reshape(depth, n_pool, feat, page)
    win_t = jnp.transpose(cache_win, (0, 1, 3, 4, 5, 2)).reshape(depth, bs, feat, win_buf)

    wt_all = jnp.transpose(w_in, (0, 2, 1)).astype(BF16)
    wc_all = (wt_all[:, CONV_OFF:].reshape(depth, 3, CONV_DIM // CONV_TILE, CONV_TILE, d)
              .transpose(0, 2, 1, 3, 4).reshape(depth, 3 * CONV_DIM, d))
    w_out_b = w_out.astype(BF16)
    ffn_b = (ffn_w1.astype(BF16), ffn_w3.astype(BF16), ffn_w2.astype(BF16))
    moe_w = (moe_w1, moe_w3, moe_w2)

    c_all = jnp.concatenate([c_prompt, c_sample, jnp.zeros((16 - bp - bs, d), F32)], axis=0)
    mod = _modulation(c_all, w_ada, b_ada)

    agg_p = _agg_matrix(t_p // CMP_STRIDE, t_p // CMP_STRIDE - 1, t_p // SEL_BLOCK, LANES)
    exp_p = _expand_matrix(t_p)
    sel_w = -(-n_sel_s // LANES) * LANES
    agg_s = _agg_matrix(n_cmp_rows, n_cmp_rows + 3, n_sel_s, sel_w)
    prompt_pages = jnp.broadcast_to(jnp.arange(t_p // LANES, dtype=jnp.int32), (bp, t_p // LANES))
    moe_tiles = (TOP_K * n_p) // MOE_TILE + N_EXPERTS

    yp = x_prompt.reshape(n_p, d)
    ys = x_sample.reshape(n_s, d)
    outs = {k: [] for k in ("cmp_p", "sel_p", "win_p", "conv_p", "cmp_s", "sel_s", "win_s", "conv_s")}
    tm_p = 1024
    n_pick = N_SEL - 1
    for l in range(depth):
        cw = _compress_weights(l, k_gain, cmp_w1, cmp_b1, cmp_w2, cmp_pe)
        qg = jnp.tile(q_gain[l] * ATTN_SCALE, Q_TILE // HEAD_DIM).reshape(1, Q_TILE)
        kg = jnp.broadcast_to(k_gain[l][:, None, :, None], (3, N_KV, HEAD_DIM, 1))
        conv_wl, conv_bl, og = conv_w[l], conv_b[l].reshape(1, CONV_DIM), out_gain[l].reshape(1, d)
        mp = [mod[l, :bp, k * d:(k + 1) * d].reshape(bp, 1, d) for k in range(6)]
        ms = [jnp.repeat(mod[l, bp:bp + bs, k * d:(k + 1) * d], t_s, axis=0).reshape(1, n_s, d) for k in range(6)]
        g1 = norm1[l].reshape(1, d)
        g2 = norm2[l].reshape(1, d)

        qn, kvt, gate, u, bg = _inproj(yp, mp[0], mp[1], g1, wt_all, wc_all, qg, kg, tm=tm_p, n_seq=bp, layer=l)
        kc, vc = _compress(
            kvt, prompt_pages,
            lambda k: pl.BlockSpec((None, feat, LANES), lambda b, g, pt: (b, 0, pt[b, g * PAGES_PER_STEP + k])),
            *cw)
        gate_hm = gate[:, :3 * N_HEADS].reshape(n_p, N_KV, 3 * GQA).transpose(1, 0, 2)
        attn = _prompt_attention(qn, gate_hm, kc, vc, kvt, agg_p, exp_p, tq=256)
        yp = _outproj(attn, u, bg, conv_wl, conv_bl, og, w_out_b, yp, mp[2], tm=tm_p, n_seq=bp, layer=l)
        kv6 = kvt.reshape(bp, 3, 2, N_KV, HEAD_DIM, t_p)
        outs["cmp_p"].append(kv6[:, 0])
        outs["sel_p"].append(kv6[:, 1])
        outs["win_p"].append(kv6[:, 2, ..., t_p - min(WINDOW, t_p):])
        outs["conv_p"].append(u.reshape(bp, t_p, CONV_DIM)[:, t_p - 2:])

        qn_s, kvt_s, gate_s, u_s, bg_s = _inproj(ys, ms[0], ms[1], g1, wt_all, wc_all, qg, kg, tm=n_s, n_seq=1,
                                                 layer=l)
        layer = l
        kc_s, vc_s = _compress(
            cmp_t, page_table,
            lambda k: pl.BlockSpec((None, None, feat, LANES),
                                   lambda b, g, pt: (layer, pt[b, g * PAGES_PER_STEP + k], 0, 0)),
            *cw)
        q5 = qn_s.astype(F32).reshape(bs, t_s, N_KV, GQA, HEAD_DIM)
        qs_sel = q5.transpose(0, 2, 3, 1, 4).reshape(bs, N_KV, GQA * t_s, HEAD_DIM)
        oc, picked = _sample_select(qs_sel, kc_s, vc_s, agg_s, q0=past_len, n_tok=t_s, n_sel=n_sel_s)
        blocks = picked[:, :, :n_pick].reshape(bs, N_KV, t_s, n_pick)
        page_of = page_table[jnp.arange(bs)[:, None, None, None], blocks // 2].reshape(-1)
        half_of = (blocks % 2).reshape(-1)
        pad8 = ((0, 0), (0, 0), (0, 0), (0, 8 - GQA), (0, 0))
        q8 = jnp.pad(q5.transpose(0, 2, 1, 3, 4), pad8)
        oc8 = jnp.pad(oc.reshape(bs, N_KV, GQA, t_s, HEAD_DIM).transpose(0, 1, 3, 2, 4), pad8)
        gate8 = jnp.pad(gate_s[:, :3 * N_HEADS].reshape(bs, t_s, N_KV, GQA, 3).transpose(0, 2, 1, 3, 4), pad8)
        attn8 = _sample_attend(page_of, half_of, q8, oc8, gate8, kvt_s, sel_t, win_t, n_tok=t_s, n_pick=n_pick,
                               layer=l)
        attn_s = attn8[:, :, :, :GQA].transpose(0, 2, 1, 3, 4).reshape(n_s, ATTN_DIM)
        st = state_conv[l]
        zero = jnp.zeros((bs, t_s - 1, CONV_DIM), F32)
        e1 = jnp.concatenate([st[:, 1:2], zero], axis=1).reshape(n_s, CONV_DIM)
        e2 = jnp.concatenate([st[:, 0:1], st[:, 1:2], zero[:, 1:]], axis=1).reshape(n_s, CONV_DIM)
        ys = _outproj(attn_s, u_s, bg_s, conv_wl, conv_bl, og, w_out_b, ys, ms[2], tm=n_s, n_seq=1, layer=l,
                      ext=(e1, e2), n_tok=t_s)
        rows_s = kvt_s[0].T.reshape(bs, t_s, 3, 2, N_KV, HEAD_DIM)
        outs["cmp_s"].append(rows_s[:, :, 0])
        outs["sel_s"].append(rows_s[:, :, 1])
        new_win_t = kvt_s[0, 4 * KV_DIM:].reshape(feat, bs, t_s).transpose(1, 0, 2)
        outs["win_s"].append(jnp.concatenate([win_t[l], new_win_t], axis=-1)[..., t_s:])
        u_ext = jnp.concatenate([st, u_s.reshape(bs, t_s, CONV_DIM)], axis=1)
        outs["conv_s"].append(u_ext[:, t_s:])

        if l % 2 == 0:
            yp = _dense_ffn(yp, mp[3], mp[4], mp[5], g2, *ffn_b, tm=512, layer=l // 2)
            ys = _dense_ffn(ys, ms[3], ms[4], ms[5], g2, *ffn_b, tm=n_s, layer=l // 2)
        else:
            rw = jnp.zeros((LANES, d), F32).at[:N_EXPERTS].set(jnp.transpose(router_w[l // 2]))
            h_p, route = _router(yp, mp[3], mp[4], g2, rw, tm=512)
            out01 = _moe_grouped(h_p, _moe_plan(route, moe_tiles), *moe_w, layer=l // 2)
            yp = _moe_combine(yp, mp[5], route, out01, tm=512)
            ys = _moe_ffn(ys, ms[3], ms[4], ms[5], g2, rw, *moe_w, tm=n_s, layer=l // 2)

    def rows_major(x):
        return jnp.transpose(x, (0, 1, 5, 2, 3, 4))

    new_win_s = jnp.stack(outs["win_s"]).reshape(depth, bs, 2, N_KV, HEAD_DIM, win_buf)
    return (yp.reshape(bp, t_p, d), ys.reshape(bs, t_s, d),
            rows_major(jnp.stack(outs["cmp_p"])), rows_major(jnp.stack(outs["sel_p"])),
            rows_major(jnp.stack(outs["win_p"])), jnp.stack(outs["conv_p"]),
            jnp.stack(outs["cmp_s"]), jnp.stack(outs["sel_s"]),
            rows_major(new_win_s), jnp.stack(outs["conv_s"]))
```

```python
import functools

import numpy as np
import jax
import jax.numpy as jnp
from jax import lax
from jax.experimental import pallas as pl
from jax.experimental.pallas import tpu as pltpu

F32 = jnp.float32
BF16 = jnp.bfloat16

D_MODEL = 2048
HEAD_DIM = 64
ATTN_DIM = 1024
N_HEADS = 16
N_KV = 4
GQA = 4
KV_DIM = 256
CONV_DIM = 1024
CMP_STRIDE = 16
SEL_BLOCK = 64
N_SEL = 16
WINDOW = 512
N_EXPERTS = 8
TOP_K = 2
KV_OFF = ATTN_DIM
GATE_OFF = KV_OFF + 6 * KV_DIM
CONV_OFF = GATE_OFF + 3 * N_HEADS
IN_COLS = CONV_OFF + 3 * CONV_DIM
EPS = 1e-6
NEG_INF = -1e30
FORCE_BONUS = 1e4
ATTN_SCALE = HEAD_DIM ** -0.5
LANES = 128
VMEM_LIMIT = 56 * 1024 * 1024
MAX_PAGES_PER_STEP = 32
NT_DIMS = (((1,), (1,)), ((), ()))


def _cparams(sem):
    return pltpu.CompilerParams(dimension_semantics=sem, vmem_limit_bytes=VMEM_LIMIT)


def _dot(a, b):
    return jnp.dot(a, b, preferred_element_type=F32)


def _dot_nt(a, b):
    return lax.dot_general(a, b, NT_DIMS, preferred_element_type=F32)


def _silu(x):
    return x * jax.nn.sigmoid(x)


def _split3_dot(x, w_hi):
    x1 = x.astype(BF16)
    r1 = x - x1.astype(F32)
    x2 = r1.astype(BF16)
    x3 = (r1 - x2.astype(F32)).astype(BF16)
    return _dot(x1, w_hi) + _dot(x2, w_hi) + _dot(x3, w_hi)


def _headnorm_lanes(x, gain):
    lane = lax.broadcasted_iota(jnp.int32, (1, LANES), 1)
    lo = lane < HEAD_DIM
    outs = []
    for c in range(x.shape[-1] // LANES):
        xc = x[:, c * LANES:(c + 1) * LANES]
        x2 = xc * xc
        s_lo = jnp.sum(jnp.where(lo, x2, 0.0), axis=-1, keepdims=True)
        s_hi = jnp.sum(jnp.where(lo, 0.0, x2), axis=-1, keepdims=True)
        r = jnp.where(lo, lax.rsqrt(s_lo * (1.0 / HEAD_DIM) + EPS), lax.rsqrt(s_hi * (1.0 / HEAD_DIM) + EPS))
        outs.append(xc * r)
    return jnp.concatenate(outs, axis=-1) * gain


def _masked_softmax(s, mask):
    s = jnp.where(mask, s, NEG_INF)
    m = jnp.max(s, axis=-1, keepdims=True)
    e = jnp.where(mask, jnp.exp(s - m), 0.0)
    return e / jnp.maximum(jnp.sum(e, axis=-1, keepdims=True), 1e-30)


def _rms_modulate(x, g, shift, scale):
    r = lax.rsqrt(jnp.mean(x * x, axis=-1, keepdims=True) + EPS)
    return (x * r * g) * (1.0 + scale) + shift


def _mod_kernel(c_ref, w_ref, b_ref, o_ref):
    s = _silu(c_ref[...])
    o_ref[...] = _dot(s.astype(BF16), w_ref[...].astype(BF16)) + b_ref[...]


def _modulation(c_all, w_ada, b_ada):
    depth, d, cols = w_ada.shape
    rows = c_all.shape[0]
    tn = 1024
    return pl.pallas_call(
        _mod_kernel,
        grid=(depth, cols // tn),
        in_specs=[pl.BlockSpec((rows, d), lambda l, j: (0, 0)),
                  pl.BlockSpec((None, d, tn), lambda l, j: (l, 0, j)),
                  pl.BlockSpec((None, 1, tn), lambda l, j: (l, 0, j))],
        out_specs=pl.BlockSpec((None, rows, tn), lambda l, j: (l, 0, j)),
        out_shape=jax.ShapeDtypeStruct((depth, rows, cols), F32),
        compiler_params=_cparams(("arbitrary", "arbitrary")),
        name="adaln_modulation",
    )(c_all, w_ada, b_ada.reshape(depth, 1, cols))


Q_TILE = 512
KV_TILE = 512
CONV_TILE = 256
N_Q_STEPS = ATTN_DIM // Q_TILE
N_KV_STEPS = 3
N_CONV_STEPS = CONV_DIM // CONV_TILE
N_IN_STEPS = N_Q_STEPS + N_KV_STEPS + N_CONV_STEPS


def _inproj_kernel(y_ref, sh_ref, sc_ref, g1_ref, wq_ref, wkv_ref, wg_ref, wc_ref, qg_ref, kg_ref,
                   qn_ref, kvt_ref, gate_ref, u_ref, bg_ref, h_scr):
    j = pl.program_id(1)

    @pl.when(j == 0)
    def _():
        h = _rms_modulate(y_ref[...], g1_ref[...], sh_ref[...], sc_ref[...])
        h_scr[...] = h.astype(BF16)
        gate_ref[...] = jax.nn.sigmoid(_dot_nt(h_scr[...], wg_ref[...]))

    @pl.when(j < N_Q_STEPS)
    def _():
        p = _dot_nt(h_scr[...], wq_ref[...])
        qn_ref[...] = _headnorm_lanes(p, qg_ref[...]).astype(BF16)

    @pl.when((j >= N_Q_STEPS) & (j < N_Q_STEPS + N_KV_STEPS))
    def _():
        pt = _dot_nt(wkv_ref[...], h_scr[...])
        tok = pt.shape[-1]
        k3 = pt[:KV_DIM].reshape(N_KV, HEAD_DIM, tok)
        ms = jnp.mean(k3 * k3, axis=1, keepdims=True)
        kn = k3 * lax.rsqrt(ms + EPS) * kg_ref[...]
        k3 = jnp.where(j == N_Q_STEPS, k3, kn)
        kvt_ref[:KV_DIM, :] = k3.reshape(KV_DIM, tok)
        kvt_ref[KV_DIM:, :] = pt[KV_DIM:]

    @pl.when(j >= N_Q_STEPS + N_KV_STEPS)
    def _():
        p = _dot_nt(h_scr[...], wc_ref[...])
        u_ref[...] = p[:, 2 * CONV_TILE:] * p[:, :CONV_TILE]
        bg_ref[...] = p[:, CONV_TILE:2 * CONV_TILE]


def _inproj(y, shift, scale, g1, wt_all, wc_all, qg, kg, *, tm, n_seq, layer):
    n, d = y.shape
    t = n // n_seq
    tiles_per_seq = t // tm
    tiles_per_mod = (n // tm) // shift.shape[0]
    r = shift.shape[1]

    def clamp(j, lo, cnt):
        return jnp.clip(j - lo, 0, cnt - 1)

    q_lo, kv_lo, c_lo = 0, N_Q_STEPS, N_Q_STEPS + N_KV_STEPS
    in_specs = [
        pl.BlockSpec((tm, d), lambda i, j: (i, 0)),
        pl.BlockSpec((None, r, d), lambda i, j: (i // tiles_per_mod, 0, 0)),
        pl.BlockSpec((None, r, d), lambda i, j: (i // tiles_per_mod, 0, 0)),
        pl.BlockSpec((1, d), lambda i, j: (0, 0)),
        pl.BlockSpec((None, Q_TILE, d), lambda i, j: (layer, clamp(j, q_lo, N_Q_STEPS), 0)),
        pl.BlockSpec((None, KV_TILE, d), lambda i, j: (layer, KV_OFF // KV_TILE + clamp(j, kv_lo, N_KV_STEPS), 0)),
        pl.BlockSpec((None, LANES, d), lambda i, j: (layer, GATE_OFF // LANES, 0)),
        pl.BlockSpec((None, 3 * CONV_TILE, d), lambda i, j: (layer, clamp(j, c_lo, N_CONV_STEPS), 0)),
        pl.BlockSpec((1, Q_TILE), lambda i, j: (0, 0)),
        pl.BlockSpec((None, N_KV, HEAD_DIM, 1), lambda i, j: (clamp(j, kv_lo, N_KV_STEPS), 0, 0, 0)),
    ]
    out_specs = [
        pl.BlockSpec((tm, Q_TILE), lambda i, j: (i, clamp(j, q_lo, N_Q_STEPS))),
        pl.BlockSpec((None, KV_TILE, tm),
                     lambda i, j: (i // tiles_per_seq, clamp(j, kv_lo, N_KV_STEPS), i % tiles_per_seq)),
        pl.BlockSpec((tm, LANES), lambda i, j: (i, 0)),
        pl.BlockSpec((tm, CONV_TILE), lambda i, j: (i, clamp(j, c_lo, N_CONV_STEPS))),
        pl.BlockSpec((tm, CONV_TILE), lambda i, j: (i, clamp(j, c_lo, N_CONV_STEPS))),
    ]
    out_shape = [
        jax.ShapeDtypeStruct((n, ATTN_DIM), BF16),
        jax.ShapeDtypeStruct((n_seq, 6 * KV_DIM, t), F32),
        jax.ShapeDtypeStruct((n, LANES), F32),
        jax.ShapeDtypeStruct((n, CONV_DIM), F32),
        jax.ShapeDtypeStruct((n, CONV_DIM), F32),
    ]
    return pl.pallas_call(
        _inproj_kernel,
        grid=(n // tm, N_IN_STEPS),
        in_specs=in_specs,
        out_specs=out_specs,
        out_shape=out_shape,
        scratch_shapes=[pltpu.VMEM((tm, d), BF16)],
        compiler_params=_cparams(("arbitrary", "arbitrary")),
        name="input_projection",
    )(y, shift, scale, g1, wt_all, wt_all, wt_all, wc_all, qg, kg)


def _compress_kernel(pt_ref, *refs, p):
    del pt_ref
    pages = refs[:p]
    wbd_ref, w1_ref, pe_ref, b1_ref, w2bd_ref, kg_ref, kc_ref, vc_ref, xs, carry, cconst = refs[p:]
    g = pl.program_id(1)
    cpp = LANES // CMP_STRIDE
    m = p * cpp

    @pl.when(g == 0)
    def _():
        carry[...] = jnp.zeros_like(carry)

    @pl.when((g == 0) & (pl.program_id(0) == 0))
    def _():
        for kv in range(2):
            c64 = jnp.sum(pe_ref[kv] * w1_ref[kv], axis=0, keepdims=True) + b1_ref[kv]
            cconst[kv] = jnp.broadcast_to(jnp.concatenate([c64, c64], axis=-1), (8, LANES))

    for k in range(p):
        for jj in range(4):
            tile = pages[k][jj * LANES:(jj + 1) * LANES, :]
            xs[jj, k * LANES:(k + 1) * LANES, :] = tile.T

    row = lax.broadcasted_iota(jnp.int32, (m, 1), 0)
    for jj in range(4):
        kv = jj // 2
        acc = jnp.zeros((m, 2 * LANES), F32)
        for t in range(CMP_STRIDE):
            acc = acc + _dot(xs[jj, pl.ds(t, m, stride=CMP_STRIDE), :].astype(BF16), wbd_ref[kv, t])
        a_part = acc[:, :LANES]
        b_part = acc[:, LANES:]
        prev = carry[jj]
        a_shift = jnp.where(row == 0, prev[7:8, :], pltpu.roll(a_part, 1, axis=0))
        carry[jj] = a_part[m - 8:, :]
        pre = a_shift + b_part + cconst[kv][0:1, :]
        out = _dot(_silu(pre).astype(BF16), w2bd_ref[kv])
        if kv == 0:
            out = _headnorm_lanes(out, kg_ref[...])
        dst = kc_ref if kv == 0 else vc_ref
        hb = (jj % 2) * 2
        dst[hb] = out[:, :HEAD_DIM].astype(BF16)
        dst[hb + 1] = out[:, HEAD_DIM:].astype(BF16)


def _compress(src, page_ids, page_spec_fn, wbd, w1, pe, b1, w2bd, kg0):
    n_seq, n_pages = page_ids.shape
    p = min(MAX_PAGES_PER_STEP, n_pages)
    assert n_pages % p == 0
    m = p * (LANES // CMP_STRIDE)
    n_groups = n_pages // p
    n_rows = n_pages * (LANES // CMP_STRIDE)
    page_specs = [page_spec_fn(k, p) for k in range(p)]

    def const(shape):
        return pl.BlockSpec(shape, lambda b, g, pt: (0,) * len(shape))

    in_specs = page_specs + [const(wbd.shape), const(w1.shape), const(pe.shape), const(b1.shape),
                             const(w2bd.shape), const(kg0.shape)]
    out_spec = pl.BlockSpec((None, N_KV, m, HEAD_DIM), lambda b, g, pt: (b, 0, g, 0))
    return pl.pallas_call(
        functools.partial(_compress_kernel, p=p),
        grid_spec=pltpu.PrefetchScalarGridSpec(
            num_scalar_prefetch=1,
            grid=(n_seq, n_groups),
            in_specs=in_specs,
            out_specs=[out_spec, out_spec],
            scratch_shapes=[pltpu.VMEM((4, p * LANES, LANES), F32), pltpu.VMEM((4, 8, LANES), F32),
                            pltpu.VMEM((2, 8, LANES), F32)],
        ),
        out_shape=[jax.ShapeDtypeStruct((n_seq, N_KV, n_rows, HEAD_DIM), BF16)] * 2,
        compiler_params=_cparams(("arbitrary", "arbitrary")),
        name="compress_mlp",
    )(page_ids, *([src] * p), wbd, w1, pe, b1, w2bd, kg0)


SEL_CHUNK = 512


def _attn_kernel(q_ref, gate_ref, kc_ref, vc_ref, kst_ref, vst_ref, kwt_ref, vwt_ref, agg_ref, exp_ref,
                 o_ref, bias_scr, *, tq, t_len):
    i = pl.program_id(2)
    q0 = i * tq
    m_rows = GQA * tq
    n_sel = t_len // SEL_BLOCK
    win_keys = WINDOW + tq
    q = q_ref[...]
    qs = jnp.concatenate([q[:, g * HEAD_DIM:(g + 1) * HEAD_DIM] for g in range(GQA)], axis=0)
    row = lax.broadcasted_iota(jnp.int32, (m_rows, 1), 0)
    qpos = q0 + (row % tq)
    qp = q0 + lax.broadcasted_iota(jnp.int32, (tq, 1), 0)

    s_c = _dot_nt(qs, kc_ref[...])
    r_i = lax.broadcasted_iota(jnp.int32, (1, s_c.shape[-1]), 1)
    mask_c = (r_i >= 1) & (CMP_STRIDE * r_i + (CMP_STRIDE - 1) <= qpos)
    p_c = _masked_softmax(s_c, mask_c)
    o_c = _dot(p_c.astype(BF16), vc_ref[...])
    imp = p_c[0:tq] + p_c[tq:2 * tq] + p_c[2 * tq:3 * tq] + p_c[3 * tq:4 * tq]
    imp_s = _split3_dot(imp, agg_ref[...])

    sidx = lax.broadcasted_iota(jnp.int32, (1, LANES), 1)
    cur = qp // SEL_BLOCK
    forced = (sidx == 0) | (sidx == cur) | (sidx == cur - 1)
    causal = sidx * SEL_BLOCK <= qp
    score = jnp.where(causal, imp_s + jnp.where(forced, FORCE_BONUS, 0.0), NEG_INF)
    sv = score.T[:n_sel]
    srow = lax.broadcasted_iota(jnp.int32, (n_sel, 1), 0)
    cnt = jnp.zeros((n_sel, tq), jnp.int32)
    for t in range(n_sel):
        other = sv[t:t + 1, :]
        beats = (other > sv) | ((other == sv) & (t < srow))
        cnt = cnt + beats.astype(jnp.int32)
    sel_t = jnp.where((cnt < min(N_SEL, n_sel)) & (sv > NEG_INF / 2), 1.0, 0.0)
    sel = jnp.concatenate([sel_t, jnp.zeros((LANES - n_sel, tq), F32)], axis=0).T
    selm = _dot(sel.astype(BF16), exp_ref[...])
    kpos_all = lax.broadcasted_iota(jnp.int32, (1, t_len), 1)
    bias_scr[...] = jnp.where((selm > 0.5) & (kpos_all <= qp), 0.0, NEG_INF)

    n_chunks = (q0 + tq + SEL_CHUNK - 1) // SEL_CHUNK

    def body(c, carry):
        m_old, l_old, acc = carry
        off = pl.multiple_of(c * SEL_CHUNK, SEL_CHUNK)
        kt = kst_ref[:, pl.ds(off, SEL_CHUNK)].astype(BF16)
        s = _dot(qs, kt).reshape(GQA, tq, SEL_CHUNK) + bias_scr[:, pl.ds(off, SEL_CHUNK)][None]
        s = s.reshape(m_rows, SEL_CHUNK)
        m_new = jnp.maximum(m_old, jnp.max(s, axis=-1, keepdims=True))
        alpha = jnp.exp(m_old - m_new)
        pr = jnp.exp(s - m_new)
        l_new = alpha * l_old + jnp.sum(pr, axis=-1, keepdims=True)
        vt = vst_ref[:, pl.ds(off, SEL_CHUNK)].astype(BF16)
        acc = alpha * acc + _dot_nt(pr.astype(BF16), vt)
        return m_new, l_new, acc

    init = (jnp.full((m_rows, 1), NEG_INF, F32), jnp.zeros((m_rows, 1), F32), jnp.zeros((m_rows, HEAD_DIM), F32))
    _, l_s, acc_s = lax.fori_loop(0, n_chunks, body, init)
    o_s = acc_s / jnp.maximum(l_s, 1e-30)

    start = pl.multiple_of(jnp.clip(q0 - WINDOW, 0, t_len - win_keys), LANES)
    dist = qp - (start + lax.broadcasted_iota(jnp.int32, (1, win_keys), 1))
    bias_w = jnp.where((dist >= 0) & (dist < WINDOW), 0.0, NEG_INF)
    s_w = _dot(qs, kwt_ref[:, pl.ds(start, win_keys)].astype(BF16)).reshape(GQA, tq, win_keys) + bias_w[None]
    s_w = s_w.reshape(m_rows, win_keys)
    e_w = jnp.exp(s_w - jnp.max(s_w, axis=-1, keepdims=True))
    p_w = e_w / jnp.maximum(jnp.sum(e_w, axis=-1, keepdims=True), 1e-30)
    o_w = _dot_nt(p_w.astype(BF16), vwt_ref[:, pl.ds(start, win_keys)].astype(BF16))

    gate = gate_ref[...]

    def gcol(jb):
        return jnp.concatenate([jnp.broadcast_to(gate[:, g * 3 + jb:g * 3 + jb + 1], (tq, HEAD_DIM))
                                for g in range(GQA)], axis=0)

    o = gcol(0) * o_c + gcol(1) * o_s + gcol(2) * o_w
    o_ref[...] = jnp.concatenate([o[g * tq:(g + 1) * tq] for g in range(GQA)], axis=-1)


def _prompt_attention(qn, gate_hm, kc, vc, kvt, agg, expand, *, tq):
    n = qn.shape[0]
    n_seq, _, t_len = kvt.shape
    tiles = t_len // tq
    kern = functools.partial(_attn_kernel, tq=tq, t_len=t_len)

    def kv_spec(block_row):
        return pl.BlockSpec((None, HEAD_DIM, t_len), lambda b, h, i: (b, block_row + h, 0))

    sel_k, sel_v = (2 * KV_DIM) // HEAD_DIM, (3 * KV_DIM) // HEAD_DIM
    win_k, win_v = (4 * KV_DIM) // HEAD_DIM, (5 * KV_DIM) // HEAD_DIM
    return pl.pallas_call(
        kern,
        grid=(n_seq, N_KV, tiles),
        in_specs=[
            pl.BlockSpec((tq, GQA * HEAD_DIM), lambda b, h, i: (b * tiles + i, h)),
            pl.BlockSpec((None, tq, 3 * GQA), lambda b, h, i: (h, b * tiles + i, 0)),
            pl.BlockSpec((None, None, kc.shape[2], HEAD_DIM), lambda b, h, i: (b, h, 0, 0)),
            pl.BlockSpec((None, None, vc.shape[2], HEAD_DIM), lambda b, h, i: (b, h, 0, 0)),
            kv_spec(sel_k), kv_spec(sel_v), kv_spec(win_k), kv_spec(win_v),
            pl.BlockSpec(agg.shape, lambda b, h, i: (0, 0)),
            pl.BlockSpec(expand.shape, lambda b, h, i: (0, 0)),
        ],
        out_specs=pl.BlockSpec((tq, GQA * HEAD_DIM), lambda b, h, i: (b * tiles + i, h)),
        out_shape=jax.ShapeDtypeStruct((n, ATTN_DIM), F32),
        scratch_shapes=[pltpu.VMEM((tq, t_len), F32)],
        compiler_params=_cparams(("arbitrary", "arbitrary", "arbitrary")),
        name="prompt_attention",
    )(qn, gate_hm, kc, vc, kvt, kvt, kvt, kvt, agg, expand)


def _sample_select_kernel(qs_ref, kc_ref, vc_ref, agg_ref, oc_ref, idx_ref, *, q0, n_tok, n_sel, n_pick):
    rows = GQA * n_tok
    row = lax.broadcasted_iota(jnp.int32, (rows, 1), 0)
    qpos = q0 + (row % n_tok)
    imps = []
    for h in range(N_KV):
        s_c = _dot_nt(qs_ref[h].astype(BF16), kc_ref[h])
        r_i = lax.broadcasted_iota(jnp.int32, (1, s_c.shape[-1]), 1)
        mask_c = (r_i >= 1) & (CMP_STRIDE * r_i + (CMP_STRIDE - 1) <= qpos)
        p_c = _masked_softmax(s_c, mask_c)
        oc_ref[h] = _dot(p_c.astype(BF16), vc_ref[h])
        imp = p_c[0:n_tok]
        for g in range(1, GQA):
            imp = imp + p_c[g * n_tok:(g + 1) * n_tok]
        imps.append(imp)
    imp_all = jnp.concatenate(imps, axis=0)
    imp_s = _split3_dot(imp_all, agg_ref[...])
    width = imp_s.shape[-1]
    sidx = lax.broadcasted_iota(jnp.int32, (1, width), 1)
    r2 = lax.broadcasted_iota(jnp.int32, (N_KV * n_tok, 1), 0)
    qp = q0 + (r2 % n_tok)
    cur = qp // SEL_BLOCK
    forced = (sidx == 0) | (sidx == cur) | (sidx == cur - 1)
    causal = sidx * SEL_BLOCK <= qp
    score = jnp.where(causal, imp_s + jnp.where(forced, FORCE_BONUS, 0.0), NEG_INF)
    lowest = -3.0e38
    score = jnp.where((sidx >= n_sel - 1), lowest, score)
    out_lane = lax.broadcasted_iota(jnp.int32, (1, LANES), 1)
    picked = jnp.zeros((N_KV * n_tok, LANES), jnp.int32)
    for it in range(n_pick):
        mx = jnp.max(score, axis=-1, keepdims=True)
        first = jnp.min(jnp.where(score == mx, sidx, width), axis=-1, keepdims=True)
        picked = jnp.where(out_lane == it, first, picked)
        score = jnp.where(sidx == first, lowest, score)
    idx_ref[...] = picked


def _sample_select(qs, kc, vc, agg, *, q0, n_tok, n_sel):
    n_seq = qs.shape[0]
    n_pick = N_SEL - 1
    kern = functools.partial(_sample_select_kernel, q0=q0, n_tok=n_tok, n_sel=n_sel, n_pick=n_pick)
    return pl.pallas_call(
        kern,
        grid=(n_seq,),
        in_specs=[
            pl.BlockSpec((None, N_KV, GQA * n_tok, HEAD_DIM), lambda b: (b, 0, 0, 0)),
            pl.BlockSpec((None, N_KV, kc.shape[2], HEAD_DIM), lambda b: (b, 0, 0, 0)),
            pl.BlockSpec((None, N_KV, vc.shape[2], HEAD_DIM), lambda b: (b, 0, 0, 0)),
            pl.BlockSpec(agg.shape, lambda b: (0, 0)),
        ],
        out_specs=[
            pl.BlockSpec((None, N_KV, GQA * n_tok, HEAD_DIM), lambda b: (b, 0, 0, 0)),
            pl.BlockSpec((None, N_KV * n_tok, LANES), lambda b: (b, 0, 0)),
        ],
        out_shape=[
            jax.ShapeDtypeStruct((n_seq, N_KV, GQA * n_tok, HEAD_DIM), F32),
            jax.ShapeDtypeStruct((n_seq, N_KV * n_tok, LANES), jnp.int32),
        ],
        compiler_params=_cparams(("arbitrary",)),
        name="sample_select",
    )(qs, kc, vc, agg)


def _sample_attend_kernel(pg_ref, half_ref, q_ref, oc_ref, gate_ref, knew_ref, vnew_ref, kwp_ref, vwp_ref,
                          kwn_ref, vwn_ref, sel_hbm, o_ref, kbuf, vbuf, sem, *, n_tok, n_pick, layer):
    n_pages = n_tok * n_pick
    b = pl.program_id(0)
    h = pl.program_id(1)
    step = b * N_KV + h
    n_steps = pl.num_programs(0) * N_KV
    slot = step % 2

    def gather_start(stp, s):
        row0 = pl.multiple_of((stp % N_KV) * HEAD_DIM, HEAD_DIM)
        for tk in range(n_pages):
            page = pg_ref[stp * n_pages + tk]
            pltpu.make_async_copy(sel_hbm.at[layer, page, pl.ds(row0, HEAD_DIM)], kbuf.at[s, tk], sem.at[s]).start()
            pltpu.make_async_copy(sel_hbm.at[layer, page, pl.ds(KV_DIM + row0, HEAD_DIM)], vbuf.at[s, tk],
                                  sem.at[s]).start()

    @pl.when(step == 0)
    def _():
        gather_start(0, 0)

    @pl.when(step + 1 < n_steps)
    def _():
        gather_start(step + 1, 1 - slot)

    for buf in (kbuf, vbuf):
        pltpu.make_async_copy(sel_hbm.at[layer, pl.ds(0, n_pages), pl.ds(0, HEAD_DIM)], buf.at[slot], sem.at[slot]).wait()
    lane = lax.broadcasted_iota(jnp.int32, (1, LANES), 1)
    n_new = knew_ref.shape[-1]
    new_lane = lax.broadcasted_iota(jnp.int32, (1, n_new), 1)
    n_win = kwp_ref.shape[-1]
    wl = lax.broadcasted_iota(jnp.int32, (1, n_win), 1)
    k_new, v_new = knew_ref[...].astype(BF16), vnew_ref[...].astype(BF16)
    kw_past, vw_past = kwp_ref[...].astype(BF16), vwp_ref[...].astype(BF16)
    kw_new, vw_new = kwn_ref[...].astype(BF16), vwn_ref[...].astype(BF16)

    for t in range(n_tok):
        q = q_ref[t].astype(BF16)
        new_mask = (new_lane // n_tok == b) & (new_lane % n_tok <= t)

        flat = ((b * N_KV + h) * n_tok + t) * n_pick
        scores, masks = [], []
        for k in range(n_pick):
            half = half_ref[flat + k]
            scores.append(_dot(q, kbuf[slot, t * n_pick + k].astype(BF16)))
            masks.append(lane // SEL_BLOCK == half)
        s_new = _dot(q, k_new)
        mx = jnp.max(jnp.where(new_mask, s_new, NEG_INF), axis=-1, keepdims=True)
        for s, mk in zip(scores, masks):
            mx = jnp.maximum(mx, jnp.max(jnp.where(mk, s, NEG_INF), axis=-1, keepdims=True))
        e_new = jnp.exp(jnp.where(new_mask, s_new, NEG_INF) - mx) * new_mask.astype(F32)
        den = jnp.sum(e_new, axis=-1, keepdims=True)
        acc = _dot_nt(e_new.astype(BF16), v_new)
        for k in range(n_pick):
            e = jnp.exp(jnp.where(masks[k], scores[k], NEG_INF) - mx) * masks[k].astype(F32)
            den = den + jnp.sum(e, axis=-1, keepdims=True)
            acc = acc + _dot_nt(e.astype(BF16), vbuf[slot, t * n_pick + k].astype(BF16))
        o_s = acc / jnp.maximum(den, 1e-30)

        past_mask = (n_win - wl + t) < WINDOW
        s_p = jnp.where(past_mask, _dot(q, kw_past), NEG_INF)
        s_n = jnp.where(new_mask, _dot(q, kw_new), NEG_INF)
        mw = jnp.maximum(jnp.max(s_p, axis=-1, keepdims=True), jnp.max(s_n, axis=-1, keepdims=True))
        e_p = jnp.exp(s_p - mw) * past_mask.astype(F32)
        e_n = jnp.exp(s_n - mw) * new_mask.astype(F32)
        den_w = jnp.sum(e_p, axis=-1, keepdims=True) + jnp.sum(e_n, axis=-1, keepdims=True)
        acc_w = _dot_nt(e_p.astype(BF16), vw_past) + _dot_nt(e_n.astype(BF16), vw_new)
        o_w = acc_w / jnp.maximum(den_w, 1e-30)

        gate = gate_ref[t]
        o_ref[t] = gate[:, 0:1] * oc_ref[t] + gate[:, 1:2] * o_s + gate[:, 2:3] * o_w


def _sample_attend(page_of, half_of, q8, oc8, gate8, kvt_new, cache_sel_t, cache_win_t, *, n_tok, n_pick, layer):
    n_seq = q8.shape[0]
    kern = functools.partial(_sample_attend_kernel, n_tok=n_tok, n_pick=n_pick, layer=layer)
    n_new = kvt_new.shape[-1]
    n_win = cache_win_t.shape[-1]
    n_pages = n_tok * n_pick
    assert cache_sel_t.shape[1] >= n_pages

    def qspec(last):
        return pl.BlockSpec((None, None, n_tok, 8, last), lambda b, h, pg, hf: (b, h, 0, 0, 0))

    def new_spec(block_row):
        return pl.BlockSpec((None, HEAD_DIM, n_new), lambda b, h, pg, hf: (0, block_row + h, 0))

    def win_spec(block_row):
        return pl.BlockSpec((None, None, HEAD_DIM, n_win), lambda b, h, pg, hf: (layer, b, block_row + h, 0))

    sel_k, sel_v = (2 * KV_DIM) // HEAD_DIM, (3 * KV_DIM) // HEAD_DIM
    win_k, win_v = (4 * KV_DIM) // HEAD_DIM, (5 * KV_DIM) // HEAD_DIM
    in_specs = [qspec(HEAD_DIM), qspec(HEAD_DIM), qspec(3), new_spec(sel_k), new_spec(sel_v),
                win_spec(0), win_spec(N_KV), new_spec(win_k), new_spec(win_v), pl.BlockSpec(memory_space=pl.ANY)]
    return pl.pallas_call(
        kern,
        grid_spec=pltpu.PrefetchScalarGridSpec(
            num_scalar_prefetch=2,
            grid=(n_seq, N_KV),
            in_specs=in_specs,
            out_specs=qspec(HEAD_DIM),
            scratch_shapes=[pltpu.VMEM((2, n_pages, HEAD_DIM, LANES), F32), pltpu.VMEM((2, n_pages, HEAD_DIM, LANES), F32),
                            pltpu.SemaphoreType.DMA((2,))],
        ),
        out_shape=jax.ShapeDtypeStruct(q8.shape, F32),
        compiler_params=_cparams(("arbitrary", "arbitrary")),
        name="sample_attend",
    )(page_of, half_of, q8, oc8, gate8, kvt_new, kvt_new, cache_win_t, cache_win_t, kvt_new, kvt_new, cache_sel_t)


def _mix_prologue(attn, u, u1, u2, bg, cw_ref, cb_ref, og_ref, o_scr):
    og = og_ref[...]
    an = attn * lax.rsqrt(jnp.mean(attn * attn, axis=-1, keepdims=True) + EPS) * og[:, :ATTN_DIM]
    yv = cb_ref[...] + cw_ref[0:1, :] * u2
    yv = yv + cw_ref[1:2, :] * u1
    yv = yv + cw_ref[2:3, :] * u
    cv = bg * yv
    cn = cv * lax.rsqrt(jnp.mean(cv * cv, axis=-1, keepdims=True) + EPS) * og[:, ATTN_DIM:]
    o_scr[:, :ATTN_DIM] = an.astype(BF16)
    o_scr[:, ATTN_DIM:] = cn.astype(BF16)


def _outproj_seq_kernel(attn_ref, u_ref, halo_ref, bg_ref, cw_ref, cb_ref, og_ref, w_ref, y_ref, gm_ref,
                        o_ref, o_scr, *, tiles_per_seq):
    i = pl.program_id(0)

    @pl.when(pl.program_id(1) == 0)
    def _():
        u = u_ref[...]
        tm = u.shape[0]
        row = lax.broadcasted_iota(jnp.int32, (tm, 1), 0)
        halo = jnp.where(i % tiles_per_seq == 0, 0.0, halo_ref[...])
        p1 = halo[7:8, :]
        p2 = halo[6:7, :]
        u1 = jnp.where(row == 0, p1, pltpu.roll(u, 1, axis=0))
        u2 = jnp.where(row == 0, p2, jnp.where(row == 1, p1, pltpu.roll(u, 2, axis=0)))
        _mix_prologue(attn_ref[...], u, u1, u2, bg_ref[...], cw_ref, cb_ref, og_ref, o_scr)

    o_ref[...] = y_ref[...] + gm_ref[...] * _dot(o_scr[...], w_ref[...])


def _outproj_state_kernel(attn_ref, u_ref, e1_ref, e2_ref, bg_ref, cw_ref, cb_ref, og_ref, w_ref, y_ref, gm_ref,
                          o_ref, o_scr, *, n_tok):
    @pl.when(pl.program_id(1) == 0)
    def _():
        u = u_ref[...]
        tm = u.shape[0]
        k = lax.broadcasted_iota(jnp.int32, (tm, 1), 0) % n_tok
        u1 = jnp.where(k >= 1, pltpu.roll(u, 1, axis=0), e1_ref[...])
        u2 = jnp.where(k >= 2, pltpu.roll(u, 2, axis=0), e2_ref[...])
        _mix_prologue(attn_ref[...], u, u1, u2, bg_ref[...], cw_ref, cb_ref, og_ref, o_scr)

    o_ref[...] = y_ref[...] + gm_ref[...] * _dot(o_scr[...], w_ref[...])


def _outproj(attn, u, bg, cw, cb, og, w, y, gmod, *, tm, n_seq, layer, ext=None, n_tok=None):
    n, d = y.shape
    tn = 512
    tiles_per_seq = (n // n_seq) // tm
    tiles_per_mod = (n // tm) // gmod.shape[0]
    r = gmod.shape[1]
    full = lambda i, j: (i, 0)
    c0 = lambda i, j: (0, 0)
    common = [pl.BlockSpec((tm, CONV_DIM), full), pl.BlockSpec((3, CONV_DIM), c0), pl.BlockSpec((1, CONV_DIM), c0),
              pl.BlockSpec((1, d), c0), pl.BlockSpec((None, d, tn), lambda i, j: (layer, 0, j)),
              pl.BlockSpec((tm, tn), lambda i, j: (i, j)),
              pl.BlockSpec((None, r, tn), lambda i, j: (i // tiles_per_mod, 0, j))]
    head = [pl.BlockSpec((tm, ATTN_DIM), full), pl.BlockSpec((tm, CONV_DIM), full)]
    if ext is None:
        kern = functools.partial(_outproj_seq_kernel, tiles_per_seq=tiles_per_seq)
        extra = [pl.BlockSpec((8, CONV_DIM), lambda i, j: (jnp.maximum(i * (tm // 8) - 1, 0), 0))]
        args = (attn, u, u, bg, cw, cb, og, w, y, gmod)
    else:
        kern = functools.partial(_outproj_state_kernel, n_tok=n_tok)
        extra = [pl.BlockSpec((tm, CONV_DIM), full), pl.BlockSpec((tm, CONV_DIM), full)]
        args = (attn, u, ext[0], ext[1], bg, cw, cb, og, w, y, gmod)
    return pl.pallas_call(
        kern,
        grid=(n // tm, d // tn),
        in_specs=head + extra + common,
        out_specs=pl.BlockSpec((tm, tn), lambda i, j: (i, j)),
        out_shape=jax.ShapeDtypeStruct((n, d), F32),
        scratch_shapes=[pltpu.VMEM((tm, d), BF16)],
        compiler_params=_cparams(("arbitrary", "arbitrary")),
        name="output_projection",
    )(*args)


def _ffn_kernel(y_ref, sh_ref, sc_ref, gm_ref, g2_ref, w1_ref, w3_ref, w2_ref, o_ref, h_scr, acc):
    j = pl.program_id(1)

    @pl.when(j == 0)
    def _():
        h_scr[...] = _rms_modulate(y_ref[...], g2_ref[...], sh_ref[...], sc_ref[...]).astype(BF16)
        acc[...] = jnp.zeros_like(acc)

    h = h_scr[...]
    hid = _silu(_dot(h, w1_ref[...])) * _dot(h, w3_ref[...])
    acc[...] += _dot(hid.astype(BF16), w2_ref[...])

    @pl.when(j == pl.num_programs(1) - 1)
    def _():
        o_ref[...] = y_ref[...] + gm_ref[...] * acc[...]


def _dense_ffn(y, shift, scale, gmod, g2, w1, w3, w2, *, tm, layer):
    n, d = y.shape
    f = w1.shape[-1]
    tf = 512
    tiles_per_mod = (n // tm) // shift.shape[0]
    r = shift.shape[1]
    mod = pl.BlockSpec((None, r, d), lambda i, j: (i // tiles_per_mod, 0, 0))
    return pl.pallas_call(
        _ffn_kernel,
        grid=(n // tm, f // tf),
        in_specs=[pl.BlockSpec((tm, d), lambda i, j: (i, 0)), mod, mod, mod,
                  pl.BlockSpec((1, d), lambda i, j: (0, 0)),
                  pl.BlockSpec((None, d, tf), lambda i, j: (layer, 0, j)),
                  pl.BlockSpec((None, d, tf), lambda i, j: (layer, 0, j)),
                  pl.BlockSpec((None, tf, d), lambda i, j: (layer, j, 0))],
        out_specs=pl.BlockSpec((tm, d), lambda i, j: (i, 0)),
        out_shape=jax.ShapeDtypeStruct((n, d), F32),
        scratch_shapes=[pltpu.VMEM((tm, d), BF16), pltpu.VMEM((tm, d), F32)],
        compiler_params=_cparams(("arbitrary", "arbitrary")),
        name="dense_ffn",
    )(y, shift, scale, gmod, g2, w1, w3, w2)


def _top2_route(h, h_bf, rw):
    lane = lax.broadcasted_iota(jnp.int32, (1, LANES), 1)
    rw_hi = rw.astype(BF16)
    rw_lo = (rw - rw_hi.astype(F32)).astype(BF16)
    h_lo = (h - h_bf.astype(F32)).astype(BF16)
    logits = _dot_nt(h_bf, rw_hi) + _dot_nt(h_lo, rw_hi) + _dot_nt(h_bf, rw_lo)
    lowest = -3.0e38
    logits = jnp.where(lane < N_EXPERTS, logits, lowest)
    m1 = jnp.max(logits, axis=-1, keepdims=True)
    i1 = jnp.min(jnp.where(logits == m1, lane, LANES), axis=-1, keepdims=True)
    rest = jnp.where(lane == i1, lowest, logits)
    m2 = jnp.max(rest, axis=-1, keepdims=True)
    i2 = jnp.min(jnp.where(rest == m2, lane, LANES), axis=-1, keepdims=True)
    e2 = jnp.exp(m2 - m1)
    den = 1.0 + e2
    return i1, i2, 1.0 / den, e2 / den


MOE_TILE = 512
MOE_FT = 256


def _route_rows(y_ref, sh_ref, sc_ref, g2_ref, rw_ref, h_ref, route_ref):
    h = _rms_modulate(y_ref[...], g2_ref[...], sh_ref[...], sc_ref[...])
    h_ref[...] = h
    i1, i2, w1, w2 = _top2_route(h, h.astype(BF16), rw_ref[...])
    lane = lax.broadcasted_iota(jnp.int32, (1, LANES), 1)
    route_ref[...] = (jnp.where(lane == 0, i1.astype(F32), 0.0) + jnp.where(lane == 1, i2.astype(F32), 0.0)
                      + jnp.where(lane == 2, w1, 0.0) + jnp.where(lane == 3, w2, 0.0))


def _router_kernel(y_ref, sh_ref, sc_ref, g2_ref, rw_ref, h_ref, route_ref):
    real = pl.program_id(0) < pl.num_programs(0) - 1

    @pl.when(real)
    def _():
        _route_rows(y_ref, sh_ref, sc_ref, g2_ref, rw_ref, h_ref, route_ref)

    @pl.when(jnp.logical_not(real))
    def _():
        h_ref[...] = jnp.zeros_like(h_ref)
        route_ref[...] = jnp.zeros_like(route_ref)


def _router(y, shift, scale, g2, rw, *, tm):
    n, d = y.shape
    tiles = n // tm
    tiles_per_mod = tiles // shift.shape[0]
    r = shift.shape[1]
    last = tiles - 1
    mod = pl.BlockSpec((None, r, d), lambda i: (jnp.minimum(i, last) // tiles_per_mod, 0, 0))
    return pl.pallas_call(
        _router_kernel,
        grid=(tiles + 1,),
        in_specs=[pl.BlockSpec((tm, d), lambda i: (jnp.minimum(i, last), 0)), mod, mod,
                  pl.BlockSpec((1, d), lambda i: (0, 0)), pl.BlockSpec((LANES, d), lambda i: (0, 0))],
        out_specs=[pl.BlockSpec((tm, d), lambda i: (i, 0)), pl.BlockSpec((tm, LANES), lambda i: (i, 0))],
        out_shape=[jax.ShapeDtypeStruct((n + tm, d), F32), jax.ShapeDtypeStruct((n + tm, LANES), F32)],
        compiler_params=_cparams(("arbitrary",)),
        name="moe_router",
    )(y, shift, scale, g2, rw)


def _router_into_kernel(y_ref, sh_ref, sc_ref, g2_ref, rw_ref, h_in, route_in, h_ref, route_ref):
    del h_in, route_in
    _route_rows(y_ref, sh_ref, sc_ref, g2_ref, rw_ref, h_ref, route_ref)


def _router_into(y, shift, scale, g2, rw, h_all, route_all, *, row0):
    n, d = y.shape
    r = shift.shape[1]
    mod = pl.BlockSpec((None, r, d), lambda i: (0, 0, 0))
    blk = row0 // n
    return pl.pallas_call(
        _router_into_kernel,
        grid=(1,),
        in_specs=[pl.BlockSpec((n, d), lambda i: (0, 0)), mod, mod,
                  pl.BlockSpec((1, d), lambda i: (0, 0)), pl.BlockSpec((LANES, d), lambda i: (0, 0)),
                  pl.BlockSpec(memory_space=pl.ANY), pl.BlockSpec(memory_space=pl.ANY)],
        out_specs=[pl.BlockSpec((n, d), lambda i: (blk, 0)), pl.BlockSpec((n, LANES), lambda i: (blk, 0))],
        out_shape=[jax.ShapeDtypeStruct(h_all.shape, F32), jax.ShapeDtypeStruct(route_all.shape, F32)],
        input_output_aliases={5: 0, 6: 1},
        compiler_params=_cparams(("arbitrary",)),
        name="moe_router_into",
    )(y, shift, scale, g2, rw, h_all, route_all)


def _moe_plan(route, n_tiles, n_pad):
    n = route.shape[0]
    flat_e = route[:, :2].astype(jnp.int32).reshape(-1)
    onehot = (flat_e[:, None] == jnp.arange(N_EXPERTS, dtype=jnp.int32)[None, :]).astype(jnp.int32)
    csum = jnp.cumsum(onehot, axis=0)
    pos = jnp.sum((csum - 1) * onehot, axis=1)
    sizes = csum[-1]
    padded = ((sizes + MOE_TILE - 1) // MOE_TILE) * MOE_TILE
    ends = jnp.cumsum(padded)
    dest = (ends - padded)[flat_e] + pos
    rows = n_tiles * MOE_TILE
    a_of_row = jnp.full((rows,), -1, jnp.int32).at[dest].set(jnp.arange(2 * n, dtype=jnp.int32),
                                                             unique_indices=True)
    real = a_of_row >= 0
    a0 = jnp.maximum(a_of_row, 0)
    src = a0 // 2
    spare = 2 * n_pad + jnp.arange(rows, dtype=jnp.int32) % MOE_TILE
    dst = jnp.where(real, (a0 % 2) * n_pad + a0 // 2, spare)
    tile_start = jnp.arange(n_tiles, dtype=jnp.int32) * MOE_TILE
    te = jnp.sum((tile_start[:, None] >= ends[None, :]).astype(jnp.int32), axis=1)
    tv = (te < N_EXPERTS).astype(jnp.int32)
    return jnp.minimum(te, N_EXPERTS - 1), tv, src, dst


def _row_copy(src_ref, src_row, dst_ref, dst_row, sem):
    return pltpu.make_async_copy(src_ref.at[pl.ds(src_row, 1)], dst_ref.at[pl.ds(dst_row, 1)], sem)


ROW_LOOP_UNROLL = 8


def _moe_group_kernel(te_ref, tv_ref, src_ref, dst_ref, h_hbm, w1a_ref, w3a_ref, w2a_ref, w1b_ref, w3b_ref, w2b_ref,
                      out_hbm, xbuf, hb, acc, obuf, gsem, ssem, *, n_f, n_tok, n_pad):
    del te_ref
    i = pl.program_id(0)
    j = pl.program_id(1)
    n_t = pl.num_programs(0)
    n_j = pl.num_programs(1)
    valid = tv_ref[i] == 1
    slot = i % 2

    def row_loop(body):
        def step(r, c):
            body(r)
            return c
        lax.fori_loop(0, MOE_TILE, step, 0, unroll=ROW_LOOP_UNROLL)

    def gather_start(tile, s):
        row_loop(lambda r: _row_copy(h_hbm, src_ref[tile * MOE_TILE + r], xbuf.at[s], r, gsem.at[s]).start())

    def gather_wait(s):
        pltpu.make_async_copy(h_hbm.at[pl.ds(0, MOE_TILE)], xbuf.at[s], gsem.at[s]).wait()

    def scatter_start(tile, s):
        row_loop(lambda r: _row_copy(obuf.at[s], r, out_hbm, dst_ref[tile * MOE_TILE + r], ssem.at[s]).start())

    def scatter_wait(s):
        pltpu.make_async_copy(obuf.at[s], out_hbm.at[pl.ds(0, MOE_TILE)], ssem.at[s]).wait()

    @pl.when(valid & (j == 0))
    def _():
        @pl.when(i == 0)
        def _():
            gather_start(0, 0)
            obuf[1] = jnp.zeros(obuf.shape[1:], F32)
            fills = [pltpu.make_async_copy(obuf.at[1], out_hbm.at[pl.ds(2 * n_pad, MOE_TILE)], ssem.at[1])]
            if n_pad > n_tok:
                fills += [pltpu.make_async_copy(obuf.at[1, pl.ds(0, n_pad - n_tok)],
                                                out_hbm.at[pl.ds(s0 * n_pad + n_tok, n_pad - n_tok)], ssem.at[1])
                          for s0 in range(2)]
            for cp in fills:
                cp.start()
            for cp in fills:
                cp.wait()

        gather_wait(slot)
        nxt = jnp.minimum(i + 1, n_t - 1)

        @pl.when((i + 1 < n_t) & (tv_ref[nxt] == 1))
        def _():
            gather_start(i + 1, 1 - slot)

        hb[...] = xbuf[slot].astype(BF16)
        acc[...] = jnp.zeros_like(acc)

    def expert_block(w1_ref, w3_ref, w2_ref):
        h = hb[...]
        hid = _silu(_dot(h, w1_ref[...].astype(BF16))) * _dot(h, w3_ref[...].astype(BF16))
        return _dot(hid.astype(BF16), w2_ref[...].astype(BF16))

    second = 2 * j + 1 < n_f

    @pl.when(valid & second)
    def _():
        acc[...] += expert_block(w1a_ref, w3a_ref, w2a_ref) + expert_block(w1b_ref, w3b_ref, w2b_ref)

    @pl.when(valid & jnp.logical_not(second))
    def _():
        acc[...] += expert_block(w1a_ref, w3a_ref, w2a_ref)

    @pl.when(valid & (j == n_j - 1))
    def _():
        obuf[slot] = acc[...]

        @pl.when(i >= 1)
        def _():
            scatter_wait(1 - slot)

        scatter_start(i, slot)

        @pl.when(i == n_t - 1)
        def _():
            scatter_wait(slot)

    prev = jnp.maximum(i - 1, 0)

    @pl.when(jnp.logical_not(valid) & (j == 0) & (i >= 1) & (tv_ref[prev] == 1))
    def _():
        scatter_wait(1 - slot)


def _moe_grouped(h, plan, w1, w3, w2, *, layer, n_tok):
    te, tv, src, dst = plan
    n_pad, d = h.shape
    assert n_pad - n_tok <= MOE_TILE
    fe = w1.shape[-1]
    n_tiles = te.shape[0]
    n_f = fe // MOE_FT

    def blk(j, tv_i, half):
        return jnp.minimum(2 * j + half, n_f - 1) * tv_i

    def w13(half):
        return pl.BlockSpec((None, None, d, MOE_FT),
                            lambda i, j, te_r, tv_r, s_r, d_r: (layer, te_r[i], 0, blk(j, tv_r[i], half)))

    def w2m(half):
        return pl.BlockSpec((None, None, MOE_FT, d),
                            lambda i, j, te_r, tv_r, s_r, d_r: (layer, te_r[i], blk(j, tv_r[i], half), 0))

    return pl.pallas_call(
        functools.partial(_moe_group_kernel, n_f=n_f, n_tok=n_tok, n_pad=n_pad),
        grid_spec=pltpu.PrefetchScalarGridSpec(
            num_scalar_prefetch=4,
            grid=(n_tiles, (n_f + 1) // 2),
            in_specs=[pl.BlockSpec(memory_space=pl.ANY), w13(0), w13(0), w2m(0), w13(1), w13(1), w2m(1)],
            out_specs=pl.BlockSpec(memory_space=pl.ANY),
            scratch_shapes=[pltpu.VMEM((2, MOE_TILE, d), F32), pltpu.VMEM((MOE_TILE, d), BF16),
                            pltpu.VMEM((MOE_TILE, d), F32), pltpu.VMEM((2, MOE_TILE, d), F32),
                            pltpu.SemaphoreType.DMA((2,)), pltpu.SemaphoreType.DMA((2,))],
        ),
        out_shape=jax.ShapeDtypeStruct((2 * n_pad + MOE_TILE, d), F32),
        compiler_params=_cparams(("arbitrary", "arbitrary")),
        name="moe_grouped_ffn",
    )(te, tv, src, dst, h, w1, w3, w2, w1, w3, w2)


def _moe_combine_kernel(y_ref, gm_ref, route_ref, a_ref, b_ref, o_ref):
    route = route_ref[...]
    o_ref[...] = y_ref[...] + gm_ref[...] * (route[:, 2:3] * a_ref[...] + route[:, 3:4] * b_ref[...])


def _moe_combine(y, gmod, route, out01, *, tm, row0, n_pad):
    n, d = y.shape
    tiles = n // tm
    tiles_per_mod = tiles // gmod.shape[0]
    r = gmod.shape[1]
    first, second = row0 // tm, (n_pad + row0) // tm
    blk = pl.BlockSpec((tm, d), lambda i: (i, 0))
    return pl.pallas_call(
        _moe_combine_kernel,
        grid=(tiles,),
        in_specs=[blk, pl.BlockSpec((None, r, d), lambda i: (i // tiles_per_mod, 0, 0)),
                  pl.BlockSpec((tm, LANES), lambda i: (i + first, 0)),
                  pl.BlockSpec((tm, d), lambda i: (i + first, 0)),
                  pl.BlockSpec((tm, d), lambda i: (i + second, 0))],
        out_specs=blk,
        out_shape=jax.ShapeDtypeStruct((n, d), F32),
        compiler_params=_cparams(("arbitrary",)),
        name="moe_combine",
    )(y, gmod, route, out01, out01)


def _agg_matrix(n_rows, n_cmp, n_sel, width):
    a = np.zeros((n_rows, width), np.float32)
    ratio = SEL_BLOCK // CMP_STRIDE
    for s in range(n_sel):
        for tkn in range(s * ratio - 1, s * ratio + ratio):
            if 0 <= tkn < n_cmp and tkn + 1 < n_rows:
                a[tkn + 1, s] = 1.0
    return jnp.asarray(a, BF16)


def _expand_matrix(t_len):
    e = np.zeros((LANES, t_len), np.float32)
    for s in range(t_len // SEL_BLOCK):
        e[s, s * SEL_BLOCK:(s + 1) * SEL_BLOCK] = 1.0
    return jnp.asarray(e, BF16)


def _compress_weights(l, k_gain, cmp_w1, cmp_b1, cmp_w2, cmp_pe):
    w1 = cmp_w1[l]
    half = CMP_STRIDE * HEAD_DIM
    wa = w1[:, :half].reshape(2, CMP_STRIDE, HEAD_DIM, HEAD_DIM)
    wb = w1[:, half:].reshape(2, CMP_STRIDE, HEAD_DIM, HEAD_DIM)
    z = jnp.zeros_like(wa)
    top = jnp.concatenate([wa, z, wb, z], axis=-1)
    bot = jnp.concatenate([z, wa, z, wb], axis=-1)
    wbd = jnp.concatenate([top, bot], axis=-2).astype(BF16)
    w2 = cmp_w2[l]
    z2 = jnp.zeros_like(w2)
    w2bd = jnp.concatenate([jnp.concatenate([w2, z2], -1), jnp.concatenate([z2, w2], -1)], -2).astype(BF16)
    pe = cmp_pe[l].reshape(2, 2 * half, 1)
    b1 = cmp_b1[l].reshape(2, 1, HEAD_DIM)
    kg0 = jnp.tile(k_gain[l, 0], 2).reshape(1, LANES)
    return wbd, w1, pe, b1, w2bd, kg0


def kernel(x_prompt, x_sample, c_prompt, c_sample, cache_cmp, cache_sel, cache_win, state_conv, page_table, w_ada, b_ada, norm1, norm2, w_in, q_gain, k_gain, cmp_w1, cmp_b1, cmp_w2, cmp_pe, conv_w, conv_b, out_gain, w_out, ffn_w1, ffn_w3, ffn_w2, router_w, moe_w1, moe_w3, moe_w2):
    bp, t_p, d = x_prompt.shape
    bs, t_s, _ = x_sample.shape
    depth = w_in.shape[0]
    n_pool, page = cache_cmp.shape[1], cache_cmp.shape[2]
    n_pages = page_table.shape[1]
    past_len = n_pages * page
    n_p, n_s = bp * t_p, bs * t_s
    win_buf = cache_win.shape[2]
    feat = 2 * KV_DIM
    assert page == LANES and t_p % LANES == 0
    n_cmp_rows = past_len // CMP_STRIDE
    assert past_len % SEL_BLOCK == 0 and t_s < CMP_STRIDE
    n_sel_s = past_len // SEL_BLOCK + 1

    cmp_t = jnp.transpose(cache_cmp, (0, 1, 3, 4, 5, 2)).reshape(depth, n_pool, feat, page)
    sel_t = jnp.transpose(cache_sel, (0, 1, 3, 4, 5, 2)).reshape(depth, n_pool, feat, page)
    win_t = jnp.transpose(cache_win, (0, 1, 3, 4, 5, 2)).reshape(depth, bs, feat, win_buf)

    wt_all = jnp.transpose(w_in, (0, 2, 1)).astype(BF16)
    wc_all = (wt_all[:, CONV_OFF:].reshape(depth, 3, CONV_DIM // CONV_TILE, CONV_TILE, d)
              .transpose(0, 2, 1, 3, 4).reshape(depth, 3 * CONV_DIM, d))
    w_out_b = w_out.astype(BF16)
    ffn_b = (ffn_w1.astype(BF16), ffn_w3.astype(BF16), ffn_w2.astype(BF16))
    moe_w = (moe_w1, moe_w3, moe_w2)

    c_all = jnp.concatenate([c_prompt, c_sample, jnp.zeros((16 - bp - bs, d), F32)], axis=0)
    mod = _modulation(c_all, w_ada, b_ada)

    agg_p = _agg_matrix(t_p // CMP_STRIDE, t_p // CMP_STRIDE - 1, t_p // SEL_BLOCK, LANES)
    exp_p = _expand_matrix(t_p)
    sel_w = -(-n_sel_s // LANES) * LANES
    agg_s = _agg_matrix(n_cmp_rows, n_cmp_rows + 3, n_sel_s, sel_w)
    prompt_pages = jnp.broadcast_to(jnp.arange(t_p // LANES, dtype=jnp.int32), (bp, t_p // LANES))
    n_moe = n_p + n_s
    moe_tiles = (TOP_K * n_moe) // MOE_TILE + N_EXPERTS
    assert n_p % MOE_TILE == 0 and n_p % n_s == 0 and n_s <= MOE_TILE

    yp = x_prompt.reshape(n_p, d)
    ys = x_sample.reshape(n_s, d)
    outs = {k: [] for k in ("cmp_p", "sel_p", "win_p", "conv_p", "cmp_s", "sel_s", "win_s", "conv_s")}
    tm_p = 1024
    n_pick = N_SEL - 1
    for l in range(depth):
        cw = _compress_weights(l, k_gain, cmp_w1, cmp_b1, cmp_w2, cmp_pe)
        qg = jnp.tile(q_gain[l] * ATTN_SCALE, Q_TILE // HEAD_DIM).reshape(1, Q_TILE)
        kg = jnp.broadcast_to(k_gain[l][:, None, :, None], (3, N_KV, HEAD_DIM, 1))
        conv_wl, conv_bl, og = conv_w[l], conv_b[l].reshape(1, CONV_DIM), out_gain[l].reshape(1, d)
        mp = [mod[l, :bp, k * d:(k + 1) * d].reshape(bp, 1, d) for k in range(6)]
        ms = [jnp.repeat(mod[l, bp:bp + bs, k * d:(k + 1) * d], t_s, axis=0).reshape(1, n_s, d) for k in range(6)]
        g1 = norm1[l].reshape(1, d)
        g2 = norm2[l].reshape(1, d)

        qn, kvt, gate, u, bg = _inproj(yp, mp[0], mp[1], g1, wt_all, wc_all, qg, kg, tm=tm_p, n_seq=bp, layer=l)
        kc, vc = _compress(
            kvt, prompt_pages,
            lambda k, p: pl.BlockSpec((None, feat, LANES), lambda b, g, pt: (b, 0, pt[b, g * p + k])),
            *cw)
        gate_hm = gate[:, :3 * N_HEADS].reshape(n_p, N_KV, 3 * GQA).transpose(1, 0, 2)
        attn = _prompt_attention(qn, gate_hm, kc, vc, kvt, agg_p, exp_p, tq=256)
        yp = _outproj(attn, u, bg, conv_wl, conv_bl, og, w_out_b, yp, mp[2], tm=tm_p, n_seq=bp, layer=l)
        kv6 = kvt.reshape(bp, 3, 2, N_KV, HEAD_DIM, t_p)
        outs["cmp_p"].append(kv6[:, 0])
        outs["sel_p"].append(kv6[:, 1])
        outs["win_p"].append(kv6[:, 2, ..., t_p - min(WINDOW, t_p):])
        outs["conv_p"].append(u.reshape(bp, t_p, CONV_DIM)[:, t_p - 2:])

        qn_s, kvt_s, gate_s, u_s, bg_s = _inproj(ys, ms[0], ms[1], g1, wt_all, wc_all, qg, kg, tm=n_s, n_seq=1,
                                                 layer=l)
        layer = l
        kc_s, vc_s = _compress(
            cmp_t, page_table,
            lambda k, p: pl.BlockSpec((None, None, feat, LANES),
                                      lambda b, g, pt: (layer, pt[b, g * p + k], 0, 0)),
            *cw)
        q5 = qn_s.astype(F32).reshape(bs, t_s, N_KV, GQA, HEAD_DIM)
        qs_sel = q5.transpose(0, 2, 3, 1, 4).reshape(bs, N_KV, GQA * t_s, HEAD_DIM)
        oc, picked = _sample_select(qs_sel, kc_s, vc_s, agg_s, q0=past_len, n_tok=t_s, n_sel=n_sel_s)
        blocks = picked[:, :, :n_pick].reshape(bs, N_KV, t_s, n_pick)
        page_of = page_table[jnp.arange(bs)[:, None, None, None], blocks // 2].reshape(-1)
        half_of = (blocks % 2).reshape(-1)
        pad8 = ((0, 0), (0, 0), (0, 0), (0, 8 - GQA), (0, 0))
        q8 = jnp.pad(q5.transpose(0, 2, 1, 3, 4), pad8)
        oc8 = jnp.pad(oc.reshape(bs, N_KV, GQA, t_s, HEAD_DIM).transpose(0, 1, 3, 2, 4), pad8)
        gate8 = jnp.pad(gate_s[:, :3 * N_HEADS].reshape(bs, t_s, N_KV, GQA, 3).transpose(0, 2, 1, 3, 4), pad8)
        attn8 = _sample_attend(page_of, half_of, q8, oc8, gate8, kvt_s, sel_t, win_t, n_tok=t_s, n_pick=n_pick,
                               layer=l)
        attn_s = attn8[:, :, :, :GQA].transpose(0, 2, 1, 3, 4).reshape(n_s, ATTN_DIM)
        st = state_conv[l]
        zero = jnp.zeros((bs, t_s - 1, CONV_DIM), F32)
        e1 = jnp.concatenate([st[:, 1:2], zero], axis=1).reshape(n_s, CONV_DIM)
        e2 = jnp.concatenate([st[:, 0:1], st[:, 1:2], zero[:, 1:]], axis=1).reshape(n_s, CONV_DIM)
        ys = _outproj(attn_s, u_s, bg_s, conv_wl, conv_bl, og, w_out_b, ys, ms[2], tm=n_s, n_seq=1, layer=l,
                      ext=(e1, e2), n_tok=t_s)
        rows_s = kvt_s[0].T.reshape(bs, t_s, 3, 2, N_KV, HEAD_DIM)
        outs["cmp_s"].append(rows_s[:, :, 0])
        outs["sel_s"].append(rows_s[:, :, 1])
        new_win_t = kvt_s[0, 4 * KV_DIM:].reshape(feat, bs, t_s).transpose(1, 0, 2)
        outs["win_s"].append(jnp.concatenate([win_t[l], new_win_t], axis=-1)[..., t_s:])
        u_ext = jnp.concatenate([st, u_s.reshape(bs, t_s, CONV_DIM)], axis=1)
        outs["conv_s"].append(u_ext[:, t_s:])

        if l % 2 == 0:
            yp = _dense_ffn(yp, mp[3], mp[4], mp[5], g2, *ffn_b, tm=512, layer=l // 2)
            ys = _dense_ffn(ys, ms[3], ms[4], ms[5], g2, *ffn_b, tm=n_s, layer=l // 2)
        else:
            rw = jnp.zeros((LANES, d), F32).at[:N_EXPERTS].set(jnp.transpose(router_w[l // 2]))
            h_all, route_all = _router(yp, mp[3], mp[4], g2, rw, tm=MOE_TILE)
            h_all, route_all = _router_into(ys, ms[3], ms[4], g2, rw, h_all, route_all, row0=n_p)
            n_pad = h_all.shape[0]
            plan = _moe_plan(route_all[:n_moe], moe_tiles, n_pad)
            out01 = _moe_grouped(h_all, plan, *moe_w, layer=l // 2, n_tok=n_moe)
            yp = _moe_combine(yp, mp[5], route_all, out01, tm=MOE_TILE, row0=0, n_pad=n_pad)
            ys = _moe_combine(ys, ms[5], route_all, out01, tm=n_s, row0=n_p, n_pad=n_pad)

    def rows_major(x):
        return jnp.transpose(x, (0, 1, 5, 2, 3, 4))

    new_win_s = jnp.stack(outs["win_s"]).reshape(depth, bs, 2, N_KV, HEAD_DIM, win_buf)
    return (yp.reshape(bp, t_p, d), ys.reshape(bs, t_s, d),
            rows_major(jnp.stack(outs["cmp_p"])), rows_major(jnp.stack(outs["sel_p"])),
            rows_major(jnp.stack(outs["win_p"])), jnp.stack(outs["conv_p"]),
            jnp.stack(outs["cmp_s"]), jnp.stack(outs["sel_s"]),
            rows_major(new_win_s), jnp.stack(outs["conv_s"]))
```

```python
import functools

import numpy as np
import jax
import jax.numpy as jnp
from jax import lax
from jax.experimental import pallas as pl
from jax.experimental.pallas import tpu as pltpu

F32 = jnp.float32
BF16 = jnp.bfloat16

D_MODEL = 2048
HEAD_DIM = 64
ATTN_DIM = 1024
N_HEADS = 16
N_KV = 4
GQA = 4
KV_DIM = 256
CONV_DIM = 1024
CMP_STRIDE = 16
SEL_BLOCK = 64
N_SEL = 16
WINDOW = 512
N_EXPERTS = 8
TOP_K = 2
KV_OFF = ATTN_DIM
GATE_OFF = KV_OFF + 6 * KV_DIM
CONV_OFF = GATE_OFF + 3 * N_HEADS
IN_COLS = CONV_OFF + 3 * CONV_DIM
EPS = 1e-6
NEG_INF = -1e30
FORCE_BONUS = 1e4
ATTN_SCALE = HEAD_DIM ** -0.5
LANES = 128
VMEM_LIMIT = 56 * 1024 * 1024
MAX_PAGES_PER_STEP = 32
NT_DIMS = (((1,), (1,)), ((), ()))


def _cparams(sem):
    return pltpu.CompilerParams(dimension_semantics=sem, vmem_limit_bytes=VMEM_LIMIT)


def _dot(a, b):
    return jnp.dot(a, b, preferred_element_type=F32)


def _dot_nt(a, b):
    return lax.dot_general(a, b, NT_DIMS, preferred_element_type=F32)


def _silu(x):
    return x * jax.nn.sigmoid(x)


def _split3_dot(x, w_hi):
    x1 = x.astype(BF16)
    r1 = x - x1.astype(F32)
    x2 = r1.astype(BF16)
    x3 = (r1 - x2.astype(F32)).astype(BF16)
    return _dot(x1, w_hi) + _dot(x2, w_hi) + _dot(x3, w_hi)


def _headnorm_lanes(x, gain):
    lane = lax.broadcasted_iota(jnp.int32, (1, LANES), 1)
    lo = lane < HEAD_DIM
    outs = []
    for c in range(x.shape[-1] // LANES):
        xc = x[:, c * LANES:(c + 1) * LANES]
        x2 = xc * xc
        s_lo = jnp.sum(jnp.where(lo, x2, 0.0), axis=-1, keepdims=True)
        s_hi = jnp.sum(jnp.where(lo, 0.0, x2), axis=-1, keepdims=True)
        r = jnp.where(lo, lax.rsqrt(s_lo * (1.0 / HEAD_DIM) + EPS), lax.rsqrt(s_hi * (1.0 / HEAD_DIM) + EPS))
        outs.append(xc * r)
    return jnp.concatenate(outs, axis=-1) * gain


def _masked_softmax(s, mask):
    s = jnp.where(mask, s, NEG_INF)
    m = jnp.max(s, axis=-1, keepdims=True)
    e = jnp.where(mask, jnp.exp(s - m), 0.0)
    return e / jnp.maximum(jnp.sum(e, axis=-1, keepdims=True), 1e-30)


def _rms_modulate(x, g, shift, scale):
    r = lax.rsqrt(jnp.mean(x * x, axis=-1, keepdims=True) + EPS)
    return (x * r * g) * (1.0 + scale) + shift


def _mod_kernel(c_ref, w_ref, b_ref, o_ref):
    s = _silu(c_ref[...])
    o_ref[...] = _dot(s.astype(BF16), w_ref[...].astype(BF16)) + b_ref[...]


def _modulation(c_all, w_ada, b_ada):
    depth, d, cols = w_ada.shape
    rows = c_all.shape[0]
    tn = 1024
    return pl.pallas_call(
        _mod_kernel,
        grid=(depth, cols // tn),
        in_specs=[pl.BlockSpec((rows, d), lambda l, j: (0, 0)),
                  pl.BlockSpec((None, d, tn), lambda l, j: (l, 0, j)),
                  pl.BlockSpec((None, 1, tn), lambda l, j: (l, 0, j))],
        out_specs=pl.BlockSpec((None, rows, tn), lambda l, j: (l, 0, j)),
        out_shape=jax.ShapeDtypeStruct((depth, rows, cols), F32),
        compiler_params=_cparams(("arbitrary", "arbitrary")),
        name="adaln_modulation",
    )(c_all, w_ada, b_ada.reshape(depth, 1, cols))


Q_TILE = 512
KV_TILE = 512
CONV_TILE = 256
N_Q_STEPS = ATTN_DIM // Q_TILE
N_KV_STEPS = 3
N_CONV_STEPS = CONV_DIM // CONV_TILE
N_IN_STEPS = N_Q_STEPS + N_KV_STEPS + N_CONV_STEPS


def _inproj_kernel(y_ref, sh_ref, sc_ref, g1_ref, wq_ref, wkv_ref, wg_ref, wc_ref, qg_ref, kg_ref,
                   qn_ref, kvt_ref, gate_ref, u_ref, bg_ref, h_scr):
    j = pl.program_id(1)

    @pl.when(j == 0)
    def _():
        h = _rms_modulate(y_ref[...], g1_ref[...], sh_ref[...], sc_ref[...])
        h_scr[...] = h.astype(BF16)
        gate_ref[...] = jax.nn.sigmoid(_dot_nt(h_scr[...], wg_ref[...]))

    @pl.when(j < N_Q_STEPS)
    def _():
        p = _dot_nt(h_scr[...], wq_ref[...])
        qn_ref[...] = _headnorm_lanes(p, qg_ref[...]).astype(BF16)

    @pl.when((j >= N_Q_STEPS) & (j < N_Q_STEPS + N_KV_STEPS))
    def _():
        pt = _dot_nt(wkv_ref[...], h_scr[...])
        tok = pt.shape[-1]
        k3 = pt[:KV_DIM].reshape(N_KV, HEAD_DIM, tok)
        ms = jnp.mean(k3 * k3, axis=1, keepdims=True)
        kn = k3 * lax.rsqrt(ms + EPS) * kg_ref[...]
        k3 = jnp.where(j == N_Q_STEPS, k3, kn)
        kvt_ref[:KV_DIM, :] = k3.reshape(KV_DIM, tok)
        kvt_ref[KV_DIM:, :] = pt[KV_DIM:]

    @pl.when(j >= N_Q_STEPS + N_KV_STEPS)
    def _():
        p = _dot_nt(h_scr[...], wc_ref[...])
        u_ref[...] = p[:, 2 * CONV_TILE:] * p[:, :CONV_TILE]
        bg_ref[...] = p[:, CONV_TILE:2 * CONV_TILE]


def _inproj(y, shift, scale, g1, wt_all, wc_all, qg, kg, *, tm, n_seq, layer):
    n, d = y.shape
    t = n // n_seq
    tiles_per_seq = t // tm
    tiles_per_mod = (n // tm) // shift.shape[0]
    r = shift.shape[1]

    def clamp(j, lo, cnt):
        return jnp.clip(j - lo, 0, cnt - 1)

    q_lo, kv_lo, c_lo = 0, N_Q_STEPS, N_Q_STEPS + N_KV_STEPS
    in_specs = [
        pl.BlockSpec((tm, d), lambda i, j: (i, 0)),
        pl.BlockSpec((None, r, d), lambda i, j: (i // tiles_per_mod, 0, 0)),
        pl.BlockSpec((None, r, d), lambda i, j: (i // tiles_per_mod, 0, 0)),
        pl.BlockSpec((1, d), lambda i, j: (0, 0)),
        pl.BlockSpec((None, Q_TILE, d), lambda i, j: (layer, clamp(j, q_lo, N_Q_STEPS), 0)),
        pl.BlockSpec((None, KV_TILE, d), lambda i, j: (layer, KV_OFF // KV_TILE + clamp(j, kv_lo, N_KV_STEPS), 0)),
        pl.BlockSpec((None, LANES, d), lambda i, j: (layer, GATE_OFF // LANES, 0)),
        pl.BlockSpec((None, 3 * CONV_TILE, d), lambda i, j: (layer, clamp(j, c_lo, N_CONV_STEPS), 0)),
        pl.BlockSpec((1, Q_TILE), lambda i, j: (0, 0)),
        pl.BlockSpec((None, N_KV, HEAD_DIM, 1), lambda i, j: (clamp(j, kv_lo, N_KV_STEPS), 0, 0, 0)),
    ]
    out_specs = [
        pl.BlockSpec((tm, Q_TILE), lambda i, j: (i, clamp(j, q_lo, N_Q_STEPS))),
        pl.BlockSpec((None, KV_TILE, tm),
                     lambda i, j: (i // tiles_per_seq, clamp(j, kv_lo, N_KV_STEPS), i % tiles_per_seq)),
        pl.BlockSpec((tm, LANES), lambda i, j: (i, 0)),
        pl.BlockSpec((tm, CONV_TILE), lambda i, j: (i, clamp(j, c_lo, N_CONV_STEPS))),
        pl.BlockSpec((tm, CONV_TILE), lambda i, j: (i, clamp(j, c_lo, N_CONV_STEPS))),
    ]
    out_shape = [
        jax.ShapeDtypeStruct((n, ATTN_DIM), BF16),
        jax.ShapeDtypeStruct((n_seq, 6 * KV_DIM, t), F32),
        jax.ShapeDtypeStruct((n, LANES), F32),
        jax.ShapeDtypeStruct((n, CONV_DIM), F32),
        jax.ShapeDtypeStruct((n, CONV_DIM), F32),
    ]
    return pl.pallas_call(
        _inproj_kernel,
        grid=(n // tm, N_IN_STEPS),
        in_specs=in_specs,
        out_specs=out_specs,
        out_shape=out_shape,
        scratch_shapes=[pltpu.VMEM((tm, d), BF16)],
        compiler_params=_cparams(("arbitrary", "arbitrary")),
        name="input_projection",
    )(y, shift, scale, g1, wt_all, wt_all, wt_all, wc_all, qg, kg)


def _compress_kernel(pt_ref, *refs, p):
    del pt_ref
    pages = refs[:p]
    wbd_ref, w1_ref, pe_ref, b1_ref, w2bd_ref, kg_ref, kc_ref, vc_ref, xs, carry, cconst = refs[p:]
    g = pl.program_id(1)
    cpp = LANES // CMP_STRIDE
    m = p * cpp

    @pl.when(g == 0)
    def _():
        carry[...] = jnp.zeros_like(carry)

    @pl.when((g == 0) & (pl.program_id(0) == 0))
    def _():
        for kv in range(2):
            c64 = jnp.sum(pe_ref[kv] * w1_ref[kv], axis=0, keepdims=True) + b1_ref[kv]
            cconst[kv] = jnp.broadcast_to(jnp.concatenate([c64, c64], axis=-1), (8, LANES))

    for k in range(p):
        for jj in range(4):
            tile = pages[k][jj * LANES:(jj + 1) * LANES, :]
            xs[jj, k * LANES:(k + 1) * LANES, :] = tile.T

    row = lax.broadcasted_iota(jnp.int32, (m, 1), 0)
    for jj in range(4):
        kv = jj // 2
        acc = jnp.zeros((m, 2 * LANES), F32)
        for t in range(CMP_STRIDE):
            acc = acc + _dot(xs[jj, pl.ds(t, m, stride=CMP_STRIDE), :].astype(BF16), wbd_ref[kv, t])
        a_part = acc[:, :LANES]
        b_part = acc[:, LANES:]
        prev = carry[jj]
        a_shift = jnp.where(row == 0, prev[7:8, :], pltpu.roll(a_part, 1, axis=0))
        carry[jj] = a_part[m - 8:, :]
        pre = a_shift + b_part + cconst[kv][0:1, :]
        out = _dot(_silu(pre).astype(BF16), w2bd_ref[kv])
        if kv == 0:
            out = _headnorm_lanes(out, kg_ref[...])
        dst = kc_ref if kv == 0 else vc_ref
        hb = (jj % 2) * 2
        dst[hb] = out[:, :HEAD_DIM].astype(BF16)
        dst[hb + 1] = out[:, HEAD_DIM:].astype(BF16)


def _compress(src, page_ids, page_spec_fn, wbd, w1, pe, b1, w2bd, kg0):
    n_seq, n_pages = page_ids.shape
    p = min(MAX_PAGES_PER_STEP, n_pages)
    assert n_pages % p == 0
    m = p * (LANES // CMP_STRIDE)
    n_groups = n_pages // p
    n_rows = n_pages * (LANES // CMP_STRIDE)
    page_specs = [page_spec_fn(k, p) for k in range(p)]

    def const(shape):
        return pl.BlockSpec(shape, lambda b, g, pt: (0,) * len(shape))

    in_specs = page_specs + [const(wbd.shape), const(w1.shape), const(pe.shape), const(b1.shape),
                             const(w2bd.shape), const(kg0.shape)]
    out_spec = pl.BlockSpec((None, N_KV, m, HEAD_DIM), lambda b, g, pt: (b, 0, g, 0))
    return pl.pallas_call(
        functools.partial(_compress_kernel, p=p),
        grid_spec=pltpu.PrefetchScalarGridSpec(
            num_scalar_prefetch=1,
            grid=(n_seq, n_groups),
            in_specs=in_specs,
            out_specs=[out_spec, out_spec],
            scratch_shapes=[pltpu.VMEM((4, p * LANES, LANES), F32), pltpu.VMEM((4, 8, LANES), F32),
                            pltpu.VMEM((2, 8, LANES), F32)],
        ),
        out_shape=[jax.ShapeDtypeStruct((n_seq, N_KV, n_rows, HEAD_DIM), BF16)] * 2,
        compiler_params=_cparams(("arbitrary", "arbitrary")),
        name="compress_mlp",
    )(page_ids, *([src] * p), wbd, w1, pe, b1, w2bd, kg0)


SEL_CHUNK = 512


def _attn_kernel(q_ref, gate_ref, kc_ref, vc_ref, kst_ref, vst_ref, kwt_ref, vwt_ref, agg_ref, exp_ref,
                 o_ref, bias_scr, *, tq, t_len):
    i = pl.program_id(2)
    q0 = i * tq
    m_rows = GQA * tq
    n_sel = t_len // SEL_BLOCK
    win_keys = WINDOW + tq
    q = q_ref[...]
    qs = jnp.concatenate([q[:, g * HEAD_DIM:(g + 1) * HEAD_DIM] for g in range(GQA)], axis=0)
    row = lax.broadcasted_iota(jnp.int32, (m_rows, 1), 0)
    qpos = q0 + (row % tq)
    qp = q0 + lax.broadcasted_iota(jnp.int32, (tq, 1), 0)

    s_c = _dot_nt(qs, kc_ref[...])
    r_i = lax.broadcasted_iota(jnp.int32, (1, s_c.shape[-1]), 1)
    mask_c = (r_i >= 1) & (CMP_STRIDE * r_i + (CMP_STRIDE - 1) <= qpos)
    p_c = _masked_softmax(s_c, mask_c)
    o_c = _dot(p_c.astype(BF16), vc_ref[...])
    imp = p_c[0:tq] + p_c[tq:2 * tq] + p_c[2 * tq:3 * tq] + p_c[3 * tq:4 * tq]
    imp_s = _split3_dot(imp, agg_ref[...])

    sidx = lax.broadcasted_iota(jnp.int32, (1, LANES), 1)
    cur = qp // SEL_BLOCK
    forced = (sidx == 0) | (sidx == cur) | (sidx == cur - 1)
    causal = sidx * SEL_BLOCK <= qp
    score = jnp.where(causal, imp_s + jnp.where(forced, FORCE_BONUS, 0.0), NEG_INF)
    sv = score.T[:n_sel]
    srow = lax.broadcasted_iota(jnp.int32, (n_sel, 1), 0)
    cnt = jnp.zeros((n_sel, tq), jnp.int32)
    for t in range(n_sel):
        other = sv[t:t + 1, :]
        beats = (other > sv) | ((other == sv) & (t < srow))
        cnt = cnt + beats.astype(jnp.int32)
    sel_t = jnp.where((cnt < min(N_SEL, n_sel)) & (sv > NEG_INF / 2), 1.0, 0.0)
    sel = jnp.concatenate([sel_t, jnp.zeros((LANES - n_sel, tq), F32)], axis=0).T
    selm = _dot(sel.astype(BF16), exp_ref[...])
    kpos_all = lax.broadcasted_iota(jnp.int32, (1, t_len), 1)
    bias_scr[...] = jnp.where((selm > 0.5) & (kpos_all <= qp), 0.0, NEG_INF)

    n_chunks = (q0 + tq + SEL_CHUNK - 1) // SEL_CHUNK

    def body(c, carry):
        m_old, l_old, acc = carry
        off = pl.multiple_of(c * SEL_CHUNK, SEL_CHUNK)
        kt = kst_ref[:, pl.ds(off, SEL_CHUNK)].astype(BF16)
        s = _dot(qs, kt).reshape(GQA, tq, SEL_CHUNK) + bias_scr[:, pl.ds(off, SEL_CHUNK)][None]
        s = s.reshape(m_rows, SEL_CHUNK)
        m_new = jnp.maximum(m_old, jnp.max(s, axis=-1, keepdims=True))
        alpha = jnp.exp(m_old - m_new)
        pr = jnp.exp(s - m_new)
        l_new = alpha * l_old + jnp.sum(pr, axis=-1, keepdims=True)
        vt = vst_ref[:, pl.ds(off, SEL_CHUNK)].astype(BF16)
        acc = alpha * acc + _dot_nt(pr.astype(BF16), vt)
        return m_new, l_new, acc

    init = (jnp.full((m_rows, 1), NEG_INF, F32), jnp.zeros((m_rows, 1), F32), jnp.zeros((m_rows, HEAD_DIM), F32))
    _, l_s, acc_s = lax.fori_loop(0, n_chunks, body, init)
    o_s = acc_s / jnp.maximum(l_s, 1e-30)

    start = pl.multiple_of(jnp.clip(q0 - WINDOW, 0, t_len - win_keys), LANES)
    dist = qp - (start + lax.broadcasted_iota(jnp.int32, (1, win_keys), 1))
    bias_w = jnp.where((dist >= 0) & (dist < WINDOW), 0.0, NEG_INF)
    s_w = _dot(qs, kwt_ref[:, pl.ds(start, win_keys)].astype(BF16)).reshape(GQA, tq, win_keys) + bias_w[None]
    s_w = s_w.reshape(m_rows, win_keys)
    e_w = jnp.exp(s_w - jnp.max(s_w, axis=-1, keepdims=True))
    p_w = e_w / jnp.maximum(jnp.sum(e_w, axis=-1, keepdims=True), 1e-30)
    o_w = _dot_nt(p_w.astype(BF16), vwt_ref[:, pl.ds(start, win_keys)].astype(BF16))

    gate = gate_ref[...]

    def gcol(jb):
        return jnp.concatenate([jnp.broadcast_to(gate[:, g * 3 + jb:g * 3 + jb + 1], (tq, HEAD_DIM))
                                for g in range(GQA)], axis=0)

    o = gcol(0) * o_c + gcol(1) * o_s + gcol(2) * o_w
    o_ref[...] = jnp.concatenate([o[g * tq:(g + 1) * tq] for g in range(GQA)], axis=-1)


def _prompt_attention(qn, gate_hm, kc, vc, kvt, agg, expand, *, tq):
    n = qn.shape[0]
    n_seq, _, t_len = kvt.shape
    tiles = t_len // tq
    kern = functools.partial(_attn_kernel, tq=tq, t_len=t_len)

    def kv_spec(block_row):
        return pl.BlockSpec((None, HEAD_DIM, t_len), lambda b, h, i: (b, block_row + h, 0))

    sel_k, sel_v = (2 * KV_DIM) // HEAD_DIM, (3 * KV_DIM) // HEAD_DIM
    win_k, win_v = (4 * KV_DIM) // HEAD_DIM, (5 * KV_DIM) // HEAD_DIM
    return pl.pallas_call(
        kern,
        grid=(n_seq, N_KV, tiles),
        in_specs=[
            pl.BlockSpec((tq, GQA * HEAD_DIM), lambda b, h, i: (b * tiles + i, h)),
            pl.BlockSpec((None, tq, 3 * GQA), lambda b, h, i: (h, b * tiles + i, 0)),
            pl.BlockSpec((None, None, kc.shape[2], HEAD_DIM), lambda b, h, i: (b, h, 0, 0)),
            pl.BlockSpec((None, None, vc.shape[2], HEAD_DIM), lambda b, h, i: (b, h, 0, 0)),
            kv_spec(sel_k), kv_spec(sel_v), kv_spec(win_k), kv_spec(win_v),
            pl.BlockSpec(agg.shape, lambda b, h, i: (0, 0)),
            pl.BlockSpec(expand.shape, lambda b, h, i: (0, 0)),
        ],
        out_specs=pl.BlockSpec((tq, GQA * HEAD_DIM), lambda b, h, i: (b * tiles + i, h)),
        out_shape=jax.ShapeDtypeStruct((n, ATTN_DIM), F32),
        scratch_shapes=[pltpu.VMEM((tq, t_len), F32)],
        compiler_params=_cparams(("arbitrary", "arbitrary", "arbitrary")),
        name="prompt_attention",
    )(qn, gate_hm, kc, vc, kvt, kvt, kvt, kvt, agg, expand)


def _sample_select_kernel(qs_ref, kc_ref, vc_ref, agg_ref, oc_ref, idx_ref, *, q0, n_tok, n_sel, n_pick):
    rows = GQA * n_tok
    row = lax.broadcasted_iota(jnp.int32, (rows, 1), 0)
    qpos = q0 + (row % n_tok)
    imps = []
    for h in range(N_KV):
        s_c = _dot_nt(qs_ref[h].astype(BF16), kc_ref[h])
        r_i = lax.broadcasted_iota(jnp.int32, (1, s_c.shape[-1]), 1)
        mask_c = (r_i >= 1) & (CMP_STRIDE * r_i + (CMP_STRIDE - 1) <= qpos)
        p_c = _masked_softmax(s_c, mask_c)
        oc_ref[h] = _dot(p_c.astype(BF16), vc_ref[h])
        imp = p_c[0:n_tok]
        for g in range(1, GQA):
            imp = imp + p_c[g * n_tok:(g + 1) * n_tok]
        imps.append(imp)
    imp_all = jnp.concatenate(imps, axis=0)
    imp_s = _split3_dot(imp_all, agg_ref[...])
    width = imp_s.shape[-1]
    sidx = lax.broadcasted_iota(jnp.int32, (1, width), 1)
    r2 = lax.broadcasted_iota(jnp.int32, (N_KV * n_tok, 1), 0)
    qp = q0 + (r2 % n_tok)
    cur = qp // SEL_BLOCK
    forced = (sidx == 0) | (sidx == cur) | (sidx == cur - 1)
    causal = sidx * SEL_BLOCK <= qp
    score = jnp.where(causal, imp_s + jnp.where(forced, FORCE_BONUS, 0.0), NEG_INF)
    lowest = -3.0e38
    score = jnp.where((sidx >= n_sel - 1), lowest, score)
    out_lane = lax.broadcasted_iota(jnp.int32, (1, LANES), 1)
    picked = jnp.zeros((N_KV * n_tok, LANES), jnp.int32)
    for it in range(n_pick):
        mx = jnp.max(score, axis=-1, keepdims=True)
        first = jnp.min(jnp.where(score == mx, sidx, width), axis=-1, keepdims=True)
        picked = jnp.where(out_lane == it, first, picked)
        score = jnp.where(sidx == first, lowest, score)
    idx_ref[...] = picked


def _sample_select(qs, kc, vc, agg, *, q0, n_tok, n_sel):
    n_seq = qs.shape[0]
    n_pick = N_SEL - 1
    kern = functools.partial(_sample_select_kernel, q0=q0, n_tok=n_tok, n_sel=n_sel, n_pick=n_pick)
    return pl.pallas_call(
        kern,
        grid=(n_seq,),
        in_specs=[
            pl.BlockSpec((None, N_KV, GQA * n_tok, HEAD_DIM), lambda b: (b, 0, 0, 0)),
            pl.BlockSpec((None, N_KV, kc.shape[2], HEAD_DIM), lambda b: (b, 0, 0, 0)),
            pl.BlockSpec((None, N_KV, vc.shape[2], HEAD_DIM), lambda b: (b, 0, 0, 0)),
            pl.BlockSpec(agg.shape, lambda b: (0, 0)),
        ],
        out_specs=[
            pl.BlockSpec((None, N_KV, GQA * n_tok, HEAD_DIM), lambda b: (b, 0, 0, 0)),
            pl.BlockSpec((None, N_KV * n_tok, LANES), lambda b: (b, 0, 0)),
        ],
        out_shape=[
            jax.ShapeDtypeStruct((n_seq, N_KV, GQA * n_tok, HEAD_DIM), F32),
            jax.ShapeDtypeStruct((n_seq, N_KV * n_tok, LANES), jnp.int32),
        ],
        compiler_params=_cparams(("arbitrary",)),
        name="sample_select",
    )(qs, kc, vc, agg)


def _sample_attend_kernel(pg_ref, half_ref, q_ref, oc_ref, gate_ref, knew_ref, vnew_ref, kwp_ref, vwp_ref,
                          kwn_ref, vwn_ref, sel_hbm, o_ref, kbuf, vbuf, sem, *, n_tok, n_pick, layer):
    n_pages = n_tok * n_pick
    b = pl.program_id(0)
    h = pl.program_id(1)
    step = b * N_KV + h
    n_steps = pl.num_programs(0) * N_KV
    slot = step % 2

    def gather_start(stp, s):
        row0 = pl.multiple_of((stp % N_KV) * HEAD_DIM, HEAD_DIM)
        for tk in range(n_pages):
            page = pg_ref[stp * n_pages + tk]
            pltpu.make_async_copy(sel_hbm.at[layer, page, pl.ds(row0, HEAD_DIM)], kbuf.at[s, tk], sem.at[s]).start()
            pltpu.make_async_copy(sel_hbm.at[layer, page, pl.ds(KV_DIM + row0, HEAD_DIM)], vbuf.at[s, tk],
                                  sem.at[s]).start()

    @pl.when(step == 0)
    def _():
        gather_start(0, 0)

    @pl.when(step + 1 < n_steps)
    def _():
        gather_start(step + 1, 1 - slot)

    for buf in (kbuf, vbuf):
        pltpu.make_async_copy(sel_hbm.at[layer, pl.ds(0, n_pages), pl.ds(0, HEAD_DIM)], buf.at[slot], sem.at[slot]).wait()
    lane = lax.broadcasted_iota(jnp.int32, (1, LANES), 1)
    n_new = knew_ref.shape[-1]
    new_lane = lax.broadcasted_iota(jnp.int32, (1, n_new), 1)
    n_win = kwp_ref.shape[-1]
    wl = lax.broadcasted_iota(jnp.int32, (1, n_win), 1)
    k_new, v_new = knew_ref[...].astype(BF16), vnew_ref[...].astype(BF16)
    kw_past, vw_past = kwp_ref[...].astype(BF16), vwp_ref[...].astype(BF16)
    kw_new, vw_new = kwn_ref[...].astype(BF16), vwn_ref[...].astype(BF16)

    for t in range(n_tok):
        q = q_ref[t].astype(BF16)
        new_mask = (new_lane // n_tok == b) & (new_lane % n_tok <= t)

        flat = ((b * N_KV + h) * n_tok + t) * n_pick
        scores, masks = [], []
        for k in range(n_pick):
            half = half_ref[flat + k]
            scores.append(_dot(q, kbuf[slot, t * n_pick + k].astype(BF16)))
            masks.append(lane // SEL_BLOCK == half)
        s_new = _dot(q, k_new)
        mx = jnp.max(jnp.where(new_mask, s_new, NEG_INF), axis=-1, keepdims=True)
        for s, mk in zip(scores, masks):
            mx = jnp.maximum(mx, jnp.max(jnp.where(mk, s, NEG_INF), axis=-1, keepdims=True))
        e_new = jnp.exp(jnp.where(new_mask, s_new, NEG_INF) - mx) * new_mask.astype(F32)
        den = jnp.sum(e_new, axis=-1, keepdims=True)
        acc = _dot_nt(e_new.astype(BF16), v_new)
        for k in range(n_pick):
            e = jnp.exp(jnp.where(masks[k], scores[k], NEG_INF) - mx) * masks[k].astype(F32)
            den = den + jnp.sum(e, axis=-1, keepdims=True)
            acc = acc + _dot_nt(e.astype(BF16), vbuf[slot, t * n_pick + k].astype(BF16))
        o_s = acc / jnp.maximum(den, 1e-30)

        past_mask = (n_win - wl + t) < WINDOW
        s_p = jnp.where(past_mask, _dot(q, kw_past), NEG_INF)
        s_n = jnp.where(new_mask, _dot(q, kw_new), NEG_INF)
        mw = jnp.maximum(jnp.max(s_p, axis=-1, keepdims=True), jnp.max(s_n, axis=-1, keepdims=True))
        e_p = jnp.exp(s_p - mw) * past_mask.astype(F32)
        e_n = jnp.exp(s_n - mw) * new_mask.astype(F32)
        den_w = jnp.sum(e_p, axis=-1, keepdims=True) + jnp.sum(e_n, axis=-1, keepdims=True)
        acc_w = _dot_nt(e_p.astype(BF16), vw_past) + _dot_nt(e_n.astype(BF16), vw_new)
        o_w = acc_w / jnp.maximum(den_w, 1e-30)

        gate = gate_ref[t]
        o_ref[t] = gate[:, 0:1] * oc_ref[t] + gate[:, 1:2] * o_s + gate[:, 2:3] * o_w


def _sample_attend(page_of, half_of, q8, oc8, gate8, kvt_new, cache_sel_t, cache_win_t, *, n_tok, n_pick, layer):
    n_seq = q8.shape[0]
    kern = functools.partial(_sample_attend_kernel, n_tok=n_tok, n_pick=n_pick, layer=layer)
    n_new = kvt_new.shape[-1]
    n_win = cache_win_t.shape[-1]
    n_pages = n_tok * n_pick
    assert cache_sel_t.shape[1] >= n_pages

    def qspec(last):
        return pl.BlockSpec((None, None, n_tok, 8, last), lambda b, h, pg, hf: (b, h, 0, 0, 0))

    def new_spec(block_row):
        return pl.BlockSpec((None, HEAD_DIM, n_new), lambda b, h, pg, hf: (0, block_row + h, 0))

    def win_spec(block_row):
        return pl.BlockSpec((None, None, HEAD_DIM, n_win), lambda b, h, pg, hf: (layer, b, block_row + h, 0))

    sel_k, sel_v = (2 * KV_DIM) // HEAD_DIM, (3 * KV_DIM) // HEAD_DIM
    win_k, win_v = (4 * KV_DIM) // HEAD_DIM, (5 * KV_DIM) // HEAD_DIM
    in_specs = [qspec(HEAD_DIM), qspec(HEAD_DIM), qspec(3), new_spec(sel_k), new_spec(sel_v),
                win_spec(0), win_spec(N_KV), new_spec(win_k), new_spec(win_v), pl.BlockSpec(memory_space=pl.ANY)]
    return pl.pallas_call(
        kern,
        grid_spec=pltpu.PrefetchScalarGridSpec(
            num_scalar_prefetch=2,
            grid=(n_seq, N_KV),
            in_specs=in_specs,
            out_specs=qspec(HEAD_DIM),
            scratch_shapes=[pltpu.VMEM((2, n_pages, HEAD_DIM, LANES), F32), pltpu.VMEM((2, n_pages, HEAD_DIM, LANES), F32),
                            pltpu.SemaphoreType.DMA((2,))],
        ),
        out_shape=jax.ShapeDtypeStruct(q8.shape, F32),
        compiler_params=_cparams(("arbitrary", "arbitrary")),
        name="sample_attend",
    )(page_of, half_of, q8, oc8, gate8, kvt_new, kvt_new, cache_win_t, cache_win_t, kvt_new, kvt_new, cache_sel_t)


def _mix_prologue(attn, u, u1, u2, bg, cw_ref, cb_ref, og_ref, o_scr):
    og = og_ref[...]
    an = attn * lax.rsqrt(jnp.mean(attn * attn, axis=-1, keepdims=True) + EPS) * og[:, :ATTN_DIM]
    yv = cb_ref[...] + cw_ref[0:1, :] * u2
    yv = yv + cw_ref[1:2, :] * u1
    yv = yv + cw_ref[2:3, :] * u
    cv = bg * yv
    cn = cv * lax.rsqrt(jnp.mean(cv * cv, axis=-1, keepdims=True) + EPS) * og[:, ATTN_DIM:]
    o_scr[:, :ATTN_DIM] = an.astype(BF16)
    o_scr[:, ATTN_DIM:] = cn.astype(BF16)


def _outproj_seq_kernel(attn_ref, u_ref, halo_ref, bg_ref, cw_ref, cb_ref, og_ref, w_ref, y_ref, gm_ref,
                        o_ref, o_scr, *, tiles_per_seq):
    i = pl.program_id(0)

    @pl.when(pl.program_id(1) == 0)
    def _():
        u = u_ref[...]
        tm = u.shape[0]
        row = lax.broadcasted_iota(jnp.int32, (tm, 1), 0)
        halo = jnp.where(i % tiles_per_seq == 0, 0.0, halo_ref[...])
        p1 = halo[7:8, :]
        p2 = halo[6:7, :]
        u1 = jnp.where(row == 0, p1, pltpu.roll(u, 1, axis=0))
        u2 = jnp.where(row == 0, p2, jnp.where(row == 1, p1, pltpu.roll(u, 2, axis=0)))
        _mix_prologue(attn_ref[...], u, u1, u2, bg_ref[...], cw_ref, cb_ref, og_ref, o_scr)

    o_ref[...] = y_ref[...] + gm_ref[...] * _dot(o_scr[...], w_ref[...])


def _outproj_state_kernel(attn_ref, u_ref, e1_ref, e2_ref, bg_ref, cw_ref, cb_ref, og_ref, w_ref, y_ref, gm_ref,
                          o_ref, o_scr, *, n_tok):
    @pl.when(pl.program_id(1) == 0)
    def _():
        u = u_ref[...]
        tm = u.shape[0]
        k = lax.broadcasted_iota(jnp.int32, (tm, 1), 0) % n_tok
        u1 = jnp.where(k >= 1, pltpu.roll(u, 1, axis=0), e1_ref[...])
        u2 = jnp.where(k >= 2, pltpu.roll(u, 2, axis=0), e2_ref[...])
        _mix_prologue(attn_ref[...], u, u1, u2, bg_ref[...], cw_ref, cb_ref, og_ref, o_scr)

    o_ref[...] = y_ref[...] + gm_ref[...] * _dot(o_scr[...], w_ref[...])


def _outproj(attn, u, bg, cw, cb, og, w, y, gmod, *, tm, n_seq, layer, ext=None, n_tok=None):
    n, d = y.shape
    tn = 512
    tiles_per_seq = (n // n_seq) // tm
    tiles_per_mod = (n // tm) // gmod.shape[0]
    r = gmod.shape[1]
    full = lambda i, j: (i, 0)
    c0 = lambda i, j: (0, 0)
    common = [pl.BlockSpec((tm, CONV_DIM), full), pl.BlockSpec((3, CONV_DIM), c0), pl.BlockSpec((1, CONV_DIM), c0),
              pl.BlockSpec((1, d), c0), pl.BlockSpec((None, d, tn), lambda i, j: (layer, 0, j)),
              pl.BlockSpec((tm, tn), lambda i, j: (i, j)),
              pl.BlockSpec((None, r, tn), lambda i, j: (i // tiles_per_mod, 0, j))]
    head = [pl.BlockSpec((tm, ATTN_DIM), full), pl.BlockSpec((tm, CONV_DIM), full)]
    if ext is None:
        kern = functools.partial(_outproj_seq_kernel, tiles_per_seq=tiles_per_seq)
        extra = [pl.BlockSpec((8, CONV_DIM), lambda i, j: (jnp.maximum(i * (tm // 8) - 1, 0), 0))]
        args = (attn, u, u, bg, cw, cb, og, w, y, gmod)
    else:
        kern = functools.partial(_outproj_state_kernel, n_tok=n_tok)
        extra = [pl.BlockSpec((tm, CONV_DIM), full), pl.BlockSpec((tm, CONV_DIM), full)]
        args = (attn, u, ext[0], ext[1], bg, cw, cb, og, w, y, gmod)
    return pl.pallas_call(
        kern,
        grid=(n // tm, d // tn),
        in_specs=head + extra + common,
        out_specs=pl.BlockSpec((tm, tn), lambda i, j: (i, j)),
        out_shape=jax.ShapeDtypeStruct((n, d), F32),
        scratch_shapes=[pltpu.VMEM((tm, d), BF16)],
        compiler_params=_cparams(("arbitrary", "arbitrary")),
        name="output_projection",
    )(*args)


def _ffn_kernel(y_ref, sh_ref, sc_ref, gm_ref, g2_ref, w1a_ref, w3a_ref, w2a_ref, w1b_ref, w3b_ref, w2b_ref,
                o_ref, h_scr, acc, *, n_f):
    j = pl.program_id(1)

    @pl.when(j == 0)
    def _():
        h_scr[...] = _rms_modulate(y_ref[...], g2_ref[...], sh_ref[...], sc_ref[...]).astype(BF16)
        acc[...] = jnp.zeros_like(acc)

    def hidden_block(w1_ref, w3_ref, w2_ref):
        h = h_scr[...]
        hid = _silu(_dot(h, w1_ref[...])) * _dot(h, w3_ref[...])
        return _dot(hid.astype(BF16), w2_ref[...])

    second = 2 * j + 1 < n_f

    @pl.when(second)
    def _():
        acc[...] += hidden_block(w1a_ref, w3a_ref, w2a_ref) + hidden_block(w1b_ref, w3b_ref, w2b_ref)

    @pl.when(jnp.logical_not(second))
    def _():
        acc[...] += hidden_block(w1a_ref, w3a_ref, w2a_ref)

    @pl.when(j == pl.num_programs(1) - 1)
    def _():
        o_ref[...] = y_ref[...] + gm_ref[...] * acc[...]


def _dense_ffn(y, shift, scale, gmod, g2, w1, w3, w2, *, tm, layer):
    n, d = y.shape
    f = w1.shape[-1]
    tf = 512
    n_f = f // tf
    tiles_per_mod = (n // tm) // shift.shape[0]
    r = shift.shape[1]
    mod = pl.BlockSpec((None, r, d), lambda i, j: (i // tiles_per_mod, 0, 0))

    def blk(j, half):
        return jnp.minimum(2 * j + half, n_f - 1)

    def w13(half):
        return pl.BlockSpec((None, d, tf), lambda i, j: (layer, 0, blk(j, half)))

    def w2s(half):
        return pl.BlockSpec((None, tf, d), lambda i, j: (layer, blk(j, half), 0))

    return pl.pallas_call(
        functools.partial(_ffn_kernel, n_f=n_f),
        grid=(n // tm, (n_f + 1) // 2),
        in_specs=[pl.BlockSpec((tm, d), lambda i, j: (i, 0)), mod, mod, mod,
                  pl.BlockSpec((1, d), lambda i, j: (0, 0)),
                  w13(0), w13(0), w2s(0), w13(1), w13(1), w2s(1)],
        out_specs=pl.BlockSpec((tm, d), lambda i, j: (i, 0)),
        out_shape=jax.ShapeDtypeStruct((n, d), F32),
        scratch_shapes=[pltpu.VMEM((tm, d), BF16), pltpu.VMEM((tm, d), F32)],
        compiler_params=_cparams(("arbitrary", "arbitrary")),
        name="dense_ffn",
    )(y, shift, scale, gmod, g2, w1, w3, w2, w1, w3, w2)


def _top2_route(h, h_bf, rw):
    lane = lax.broadcasted_iota(jnp.int32, (1, LANES), 1)
    rw_hi = rw.astype(BF16)
    rw_lo = (rw - rw_hi.astype(F32)).astype(BF16)
    h_lo = (h - h_bf.astype(F32)).astype(BF16)
    logits = _dot_nt(h_bf, rw_hi) + _dot_nt(h_lo, rw_hi) + _dot_nt(h_bf, rw_lo)
    lowest = -3.0e38
    logits = jnp.where(lane < N_EXPERTS, logits, lowest)
    m1 = jnp.max(logits, axis=-1, keepdims=True)
    i1 = jnp.min(jnp.where(logits == m1, lane, LANES), axis=-1, keepdims=True)
    rest = jnp.where(lane == i1, lowest, logits)
    m2 = jnp.max(rest, axis=-1, keepdims=True)
    i2 = jnp.min(jnp.where(rest == m2, lane, LANES), axis=-1, keepdims=True)
    e2 = jnp.exp(m2 - m1)
    den = 1.0 + e2
    return i1, i2, 1.0 / den, e2 / den


MOE_TILE = 512
MOE_FT = 256


def _route_rows(y_ref, sh_ref, sc_ref, g2_ref, rw_ref, h_ref, route_ref):
    h = _rms_modulate(y_ref[...], g2_ref[...], sh_ref[...], sc_ref[...])
    h_ref[...] = h
    i1, i2, w1, w2 = _top2_route(h, h.astype(BF16), rw_ref[...])
    lane = lax.broadcasted_iota(jnp.int32, (1, LANES), 1)
    route_ref[...] = (jnp.where(lane == 0, i1.astype(F32), 0.0) + jnp.where(lane == 1, i2.astype(F32), 0.0)
                      + jnp.where(lane == 2, w1, 0.0) + jnp.where(lane == 3, w2, 0.0))


def _router_kernel(y_ref, sh_ref, sc_ref, y2_ref, sh2_ref, sc2_ref, g2_ref, rw_ref, h_ref, route_ref):
    real = pl.program_id(0) < pl.num_programs(0) - 1

    @pl.when(real)
    def _():
        _route_rows(y_ref, sh_ref, sc_ref, g2_ref, rw_ref, h_ref, route_ref)

    @pl.when(jnp.logical_not(real))
    def _():
        n2 = y2_ref.shape[0]
        h_ref[...] = jnp.zeros_like(h_ref)
        route_ref[...] = jnp.zeros_like(route_ref)
        _route_rows(y2_ref, sh2_ref, sc2_ref, g2_ref, rw_ref, h_ref.at[pl.ds(0, n2)], route_ref.at[pl.ds(0, n2)])


def _router(y, shift, scale, y2, shift2, scale2, g2, rw, *, tm):
    n, d = y.shape
    n2 = y2.shape[0]
    assert n2 <= tm and shift2.shape[:2] == (1, n2)
    tiles = n // tm
    tiles_per_mod = tiles // shift.shape[0]
    r = shift.shape[1]
    last = tiles - 1
    mod = pl.BlockSpec((None, r, d), lambda i: (jnp.minimum(i, last) // tiles_per_mod, 0, 0))
    mod2 = pl.BlockSpec((None, n2, d), lambda i: (0, 0, 0))
    return pl.pallas_call(
        _router_kernel,
        grid=(tiles + 1,),
        in_specs=[pl.BlockSpec((tm, d), lambda i: (jnp.minimum(i, last), 0)), mod, mod,
                  pl.BlockSpec((n2, d), lambda i: (0, 0)), mod2, mod2,
                  pl.BlockSpec((1, d), lambda i: (0, 0)), pl.BlockSpec((LANES, d), lambda i: (0, 0))],
        out_specs=[pl.BlockSpec((tm, d), lambda i: (i, 0)), pl.BlockSpec((tm, LANES), lambda i: (i, 0))],
        out_shape=[jax.ShapeDtypeStruct((n + tm, d), F32), jax.ShapeDtypeStruct((n + tm, LANES), F32)],
        compiler_params=_cparams(("arbitrary",)),
        name="moe_router",
    )(y, shift, scale, y2, shift2, scale2, g2, rw)


def _moe_plan(route, n_tiles, n_pad):
    n = route.shape[0]
    flat_e = route[:, :2].astype(jnp.int32).reshape(-1)
    onehot = (flat_e[:, None] == jnp.arange(N_EXPERTS, dtype=jnp.int32)[None, :]).astype(jnp.int32)
    csum = jnp.cumsum(onehot, axis=0)
    pos = jnp.sum((csum - 1) * onehot, axis=1)
    sizes = csum[-1]
    padded = ((sizes + MOE_TILE - 1) // MOE_TILE) * MOE_TILE
    ends = jnp.cumsum(padded)
    dest = (ends - padded)[flat_e] + pos
    rows = n_tiles * MOE_TILE
    a_of_row = jnp.full((rows,), -1, jnp.int32).at[dest].set(jnp.arange(2 * n, dtype=jnp.int32),
                                                             unique_indices=True)
    real = a_of_row >= 0
    a0 = jnp.maximum(a_of_row, 0)
    src = a0 // 2
    spare = 2 * n_pad + jnp.arange(rows, dtype=jnp.int32) % MOE_TILE
    dst = jnp.where(real, (a0 % 2) * n_pad + a0 // 2, spare)
    tile_start = jnp.arange(n_tiles, dtype=jnp.int32) * MOE_TILE
    te = jnp.sum((tile_start[:, None] >= ends[None, :]).astype(jnp.int32), axis=1)
    tv = (te < N_EXPERTS).astype(jnp.int32)
    return jnp.minimum(te, N_EXPERTS - 1), tv, src, dst


def _row_copy(src_ref, src_row, dst_ref, dst_row, sem):
    return pltpu.make_async_copy(src_ref.at[pl.ds(src_row, 1)], dst_ref.at[pl.ds(dst_row, 1)], sem)


ROW_LOOP_UNROLL = 8


def _moe_group_kernel(te_ref, tv_ref, src_ref, dst_ref, h_hbm, w1a_ref, w3a_ref, w2a_ref, w1b_ref, w3b_ref, w2b_ref,
                      out_hbm, xbuf, hb, acc, obuf, gsem, ssem, *, n_f, n_tok, n_pad):
    del te_ref
    i = pl.program_id(0)
    j = pl.program_id(1)
    n_t = pl.num_programs(0)
    n_j = pl.num_programs(1)
    valid = tv_ref[i] == 1
    slot = i % 2

    def row_loop(body):
        def step(r, c):
            body(r)
            return c
        lax.fori_loop(0, MOE_TILE, step, 0, unroll=ROW_LOOP_UNROLL)

    def gather_start(tile, s):
        row_loop(lambda r: _row_copy(h_hbm, src_ref[tile * MOE_TILE + r], xbuf.at[s], r, gsem.at[s]).start())

    def gather_wait(s):
        pltpu.make_async_copy(h_hbm.at[pl.ds(0, MOE_TILE)], xbuf.at[s], gsem.at[s]).wait()

    def scatter_start(tile, s):
        row_loop(lambda r: _row_copy(obuf.at[s], r, out_hbm, dst_ref[tile * MOE_TILE + r], ssem.at[s]).start())

    def scatter_wait(s):
        pltpu.make_async_copy(obuf.at[s], out_hbm.at[pl.ds(0, MOE_TILE)], ssem.at[s]).wait()

    @pl.when(valid & (j == 0))
    def _():
        @pl.when(i == 0)
        def _():
            gather_start(0, 0)
            obuf[1] = jnp.zeros(obuf.shape[1:], F32)
            fills = [pltpu.make_async_copy(obuf.at[1], out_hbm.at[pl.ds(2 * n_pad, MOE_TILE)], ssem.at[1])]
            if n_pad > n_tok:
                fills += [pltpu.make_async_copy(obuf.at[1, pl.ds(0, n_pad - n_tok)],
                                                out_hbm.at[pl.ds(s0 * n_pad + n_tok, n_pad - n_tok)], ssem.at[1])
                          for s0 in range(2)]
            for cp in fills:
                cp.start()
            for cp in fills:
                cp.wait()

        gather_wait(slot)
        nxt = jnp.minimum(i + 1, n_t - 1)

        @pl.when((i + 1 < n_t) & (tv_ref[nxt] == 1))
        def _():
            gather_start(i + 1, 1 - slot)

        hb[...] = xbuf[slot].astype(BF16)
        acc[...] = jnp.zeros_like(acc)

    def expert_block(w1_ref, w3_ref, w2_ref):
        h = hb[...]
        hid = _silu(_dot(h, w1_ref[...].astype(BF16))) * _dot(h, w3_ref[...].astype(BF16))
        return _dot(hid.astype(BF16), w2_ref[...].astype(BF16))

    second = 2 * j + 1 < n_f

    @pl.when(valid & second)
    def _():
        acc[...] += expert_block(w1a_ref, w3a_ref, w2a_ref) + expert_block(w1b_ref, w3b_ref, w2b_ref)

    @pl.when(valid & jnp.logical_not(second))
    def _():
        acc[...] += expert_block(w1a_ref, w3a_ref, w2a_ref)

    @pl.when(valid & (j == n_j - 1))
    def _():
        obuf[slot] = acc[...]

        @pl.when(i >= 1)
        def _():
            scatter_wait(1 - slot)

        scatter_start(i, slot)

        @pl.when(i == n_t - 1)
        def _():
            scatter_wait(slot)

    prev = jnp.maximum(i - 1, 0)

    @pl.when(jnp.logical_not(valid) & (j == 0) & (i >= 1) & (tv_ref[prev] == 1))
    def _():
        scatter_wait(1 - slot)


def _moe_grouped(h, plan, w1, w3, w2, *, layer, n_tok):
    te, tv, src, dst = plan
    n_pad, d = h.shape
    assert n_pad - n_tok <= MOE_TILE
    fe = w1.shape[-1]
    n_tiles = te.shape[0]
    n_f = fe // MOE_FT

    def blk(j, tv_i, half):
        return jnp.minimum(2 * j + half, n_f - 1) * tv_i

    def w13(half):
        return pl.BlockSpec((None, None, d, MOE_FT),
                            lambda i, j, te_r, tv_r, s_r, d_r: (layer, te_r[i], 0, blk(j, tv_r[i], half)))

    def w2m(half):
        return pl.BlockSpec((None, None, MOE_FT, d),
                            lambda i, j, te_r, tv_r, s_r, d_r: (layer, te_r[i], blk(j, tv_r[i], half), 0))

    return pl.pallas_call(
        functools.partial(_moe_group_kernel, n_f=n_f, n_tok=n_tok, n_pad=n_pad),
        grid_spec=pltpu.PrefetchScalarGridSpec(
            num_scalar_prefetch=4,
            grid=(n_tiles, (n_f + 1) // 2),
            in_specs=[pl.BlockSpec(memory_space=pl.ANY), w13(0), w13(0), w2m(0), w13(1), w13(1), w2m(1)],
            out_specs=pl.BlockSpec(memory_space=pl.ANY),
            scratch_shapes=[pltpu.VMEM((2, MOE_TILE, d), F32), pltpu.VMEM((MOE_TILE, d), BF16),
                            pltpu.VMEM((MOE_TILE, d), F32), pltpu.VMEM((2, MOE_TILE, d), F32),
                            pltpu.SemaphoreType.DMA((2,)), pltpu.SemaphoreType.DMA((2,))],
        ),
        out_shape=jax.ShapeDtypeStruct((2 * n_pad + MOE_TILE, d), F32),
        compiler_params=_cparams(("arbitrary", "arbitrary")),
        name="moe_grouped_ffn",
    )(te, tv, src, dst, h, w1, w3, w2, w1, w3, w2)


def _moe_combine_kernel(y_ref, gm_ref, route_ref, a_ref, b_ref, o_ref):
    route = route_ref[...]
    o_ref[...] = y_ref[...] + gm_ref[...] * (route[:, 2:3] * a_ref[...] + route[:, 3:4] * b_ref[...])


def _moe_combine(y, gmod, route, out01, *, tm, row0, n_pad):
    n, d = y.shape
    tiles = n // tm
    tiles_per_mod = tiles // gmod.shape[0]
    r = gmod.shape[1]
    first, second = row0 // tm, (n_pad + row0) // tm
    blk = pl.BlockSpec((tm, d), lambda i: (i, 0))
    return pl.pallas_call(
        _moe_combine_kernel,
        grid=(tiles,),
        in_specs=[blk, pl.BlockSpec((None, r, d), lambda i: (i // tiles_per_mod, 0, 0)),
                  pl.BlockSpec((tm, LANES), lambda i: (i + first, 0)),
                  pl.BlockSpec((tm, d), lambda i: (i + first, 0)),
                  pl.BlockSpec((tm, d), lambda i: (i + second, 0))],
        out_specs=blk,
        out_shape=jax.ShapeDtypeStruct((n, d), F32),
        compiler_params=_cparams(("arbitrary",)),
        name="moe_combine",
    )(y, gmod, route, out01, out01)


def _agg_matrix(n_rows, n_cmp, n_sel, width):
    a = np.zeros((n_rows, width), np.float32)
    ratio = SEL_BLOCK // CMP_STRIDE
    for s in range(n_sel):
        for tkn in range(s * ratio - 1, s * ratio + ratio):
            if 0 <= tkn < n_cmp and tkn + 1 < n_rows:
                a[tkn + 1, s] = 1.0
    return jnp.asarray(a, BF16)


def _expand_matrix(t_len):
    e = np.zeros((LANES, t_len), np.float32)
    for s in range(t_len // SEL_BLOCK):
        e[s, s * SEL_BLOCK:(s + 1) * SEL_BLOCK] = 1.0
    return jnp.asarray(e, BF16)


def _compress_weights(l, k_gain, cmp_w1, cmp_b1, cmp_w2, cmp_pe):
    w1 = cmp_w1[l]
    half = CMP_STRIDE * HEAD_DIM
    wa = w1[:, :half].reshape(2, CMP_STRIDE, HEAD_DIM, HEAD_DIM)
    wb = w1[:, half:].reshape(2, CMP_STRIDE, HEAD_DIM, HEAD_DIM)
    z = jnp.zeros_like(wa)
    top = jnp.concatenate([wa, z, wb, z], axis=-1)
    bot = jnp.concatenate([z, wa, z, wb], axis=-1)
    wbd = jnp.concatenate([top, bot], axis=-2).astype(BF16)
    w2 = cmp_w2[l]
    z2 = jnp.zeros_like(w2)
    w2bd = jnp.concatenate([jnp.concatenate([w2, z2], -1), jnp.concatenate([z2, w2], -1)], -2).astype(BF16)
    pe = cmp_pe[l].reshape(2, 2 * half, 1)
    b1 = cmp_b1[l].reshape(2, 1, HEAD_DIM)
    kg0 = jnp.tile(k_gain[l, 0], 2).reshape(1, LANES)
    return wbd, w1, pe, b1, w2bd, kg0


def kernel(x_prompt, x_sample, c_prompt, c_sample, cache_cmp, cache_sel, cache_win, state_conv, page_table, w_ada, b_ada, norm1, norm2, w_in, q_gain, k_gain, cmp_w1, cmp_b1, cmp_w2, cmp_pe, conv_w, conv_b, out_gain, w_out, ffn_w1, ffn_w3, ffn_w2, router_w, moe_w1, moe_w3, moe_w2):
    bp, t_p, d = x_prompt.shape
    bs, t_s, _ = x_sample.shape
    depth = w_in.shape[0]
    n_pool, page = cache_cmp.shape[1], cache_cmp.shape[2]
    n_pages = page_table.shape[1]
    past_len = n_pages * page
    n_p, n_s = bp * t_p, bs * t_s
    win_buf = cache_win.shape[2]
    feat = 2 * KV_DIM
    assert page == LANES and t_p % LANES == 0
    n_cmp_rows = past_len // CMP_STRIDE
    assert past_len % SEL_BLOCK == 0 and t_s < CMP_STRIDE
    n_sel_s = past_len // SEL_BLOCK + 1

    cmp_t = jnp.transpose(cache_cmp, (0, 1, 3, 4, 5, 2)).reshape(depth, n_pool, feat, page)
    sel_t = jnp.transpose(cache_sel, (0, 1, 3, 4, 5, 2)).reshape(depth, n_pool, feat, page)
    win_t = jnp.transpose(cache_win, (0, 1, 3, 4, 5, 2)).reshape(depth, bs, feat, win_buf)

    wt_all = jnp.transpose(w_in, (0, 2, 1)).astype(BF16)
    wc_all = (wt_all[:, CONV_OFF:].reshape(depth, 3, CONV_DIM // CONV_TILE, CONV_TILE, d)
              .transpose(0, 2, 1, 3, 4).reshape(depth, 3 * CONV_DIM, d))
    w_out_b = w_out.astype(BF16)
    ffn_b = (ffn_w1.astype(BF16), ffn_w3.astype(BF16), ffn_w2.astype(BF16))
    moe_w = (moe_w1, moe_w3, moe_w2)

    c_all = jnp.concatenate([c_prompt, c_sample, jnp.zeros((16 - bp - bs, d), F32)], axis=0)
    mod = _modulation(c_all, w_ada, b_ada)

    agg_p = _agg_matrix(t_p // CMP_STRIDE, t_p // CMP_STRIDE - 1, t_p // SEL_BLOCK, LANES)
    exp_p = _expand_matrix(t_p)
    sel_w = -(-n_sel_s // LANES) * LANES
    agg_s = _agg_matrix(n_cmp_rows, n_cmp_rows + 3, n_sel_s, sel_w)
    prompt_pages = jnp.broadcast_to(jnp.arange(t_p // LANES, dtype=jnp.int32), (bp, t_p // LANES))
    n_moe = n_p + n_s
    moe_tiles = (TOP_K * n_moe) // MOE_TILE + N_EXPERTS
    assert n_p % MOE_TILE == 0 and n_p % n_s == 0 and n_s <= MOE_TILE

    yp = x_prompt.reshape(n_p, d)
    ys = x_sample.reshape(n_s, d)
    outs = {k: [] for k in ("cmp_p", "sel_p", "win_p", "conv_p", "cmp_s", "sel_s", "win_s", "conv_s")}
    tm_p = 1024
    n_pick = N_SEL - 1
    for l in range(depth):
        cw = _compress_weights(l, k_gain, cmp_w1, cmp_b1, cmp_w2, cmp_pe)
        qg = jnp.tile(q_gain[l] * ATTN_SCALE, Q_TILE // HEAD_DIM).reshape(1, Q_TILE)
        kg = jnp.broadcast_to(k_gain[l][:, None, :, None], (3, N_KV, HEAD_DIM, 1))
        conv_wl, conv_bl, og = conv_w[l], conv_b[l].reshape(1, CONV_DIM), out_gain[l].reshape(1, d)
        mp = [mod[l, :bp, k * d:(k + 1) * d].reshape(bp, 1, d) for k in range(6)]
        ms = [jnp.repeat(mod[l, bp:bp + bs, k * d:(k + 1) * d], t_s, axis=0).reshape(1, n_s, d) for k in range(6)]
        g1 = norm1[l].reshape(1, d)
        g2 = norm2[l].reshape(1, d)

        qn, kvt, gate, u, bg = _inproj(yp, mp[0], mp[1], g1, wt_all, wc_all, qg, kg, tm=tm_p, n_seq=bp, layer=l)
        kc, vc = _compress(
            kvt, prompt_pages,
            lambda k, p: pl.BlockSpec((None, feat, LANES), lambda b, g, pt: (b, 0, pt[b, g * p + k])),
            *cw)
        gate_hm = gate[:, :3 * N_HEADS].reshape(n_p, N_KV, 3 * GQA).transpose(1, 0, 2)
        attn = _prompt_attention(qn, gate_hm, kc, vc, kvt, agg_p, exp_p, tq=256)
        yp = _outproj(attn, u, bg, conv_wl, conv_bl, og, w_out_b, yp, mp[2], tm=tm_p, n_seq=bp, layer=l)
        kv6 = kvt.reshape(bp, 3, 2, N_KV, HEAD_DIM, t_p)
        outs["cmp_p"].append(kv6[:, 0])
        outs["sel_p"].append(kv6[:, 1])
        outs["win_p"].append(kv6[:, 2, ..., t_p - min(WINDOW, t_p):])
        outs["conv_p"].append(u.reshape(bp, t_p, CONV_DIM)[:, t_p - 2:])

        qn_s, kvt_s, gate_s, u_s, bg_s = _inproj(ys, ms[0], ms[1], g1, wt_all, wc_all, qg, kg, tm=n_s, n_seq=1,
                                                 layer=l)
        layer = l
        kc_s, vc_s = _compress(
            cmp_t, page_table,
            lambda k, p: pl.BlockSpec((None, None, feat, LANES),
                                      lambda b, g, pt: (layer, pt[b, g * p + k], 0, 0)),
            *cw)
        q5 = qn_s.astype(F32).reshape(bs, t_s, N_KV, GQA, HEAD_DIM)
        qs_sel = q5.transpose(0, 2, 3, 1, 4).reshape(bs, N_KV, GQA * t_s, HEAD_DIM)
        oc, picked = _sample_select(qs_sel, kc_s, vc_s, agg_s, q0=past_len, n_tok=t_s, n_sel=n_sel_s)
        blocks = picked[:, :, :n_pick].reshape(bs, N_KV, t_s, n_pick)
        page_of = page_table[jnp.arange(bs)[:, None, None, None], blocks // 2].reshape(-1)
        half_of = (blocks % 2).reshape(-1)
        pad8 = ((0, 0), (0, 0), (0, 0), (0, 8 - GQA), (0, 0))
        q8 = jnp.pad(q5.transpose(0, 2, 1, 3, 4), pad8)
        oc8 = jnp.pad(oc.reshape(bs, N_KV, GQA, t_s, HEAD_DIM).transpose(0, 1, 3, 2, 4), pad8)
        gate8 = jnp.pad(gate_s[:, :3 * N_HEADS].reshape(bs, t_s, N_KV, GQA, 3).transpose(0, 2, 1, 3, 4), pad8)
        attn8 = _sample_attend(page_of, half_of, q8, oc8, gate8, kvt_s, sel_t, win_t, n_tok=t_s, n_pick=n_pick,
                               layer=l)
        attn_s = attn8[:, :, :, :GQA].transpose(0, 2, 1, 3, 4).reshape(n_s, ATTN_DIM)
        st = state_conv[l]
        zero = jnp.zeros((bs, t_s - 1, CONV_DIM), F32)
        e1 = jnp.concatenate([st[:, 1:2], zero], axis=1).reshape(n_s, CONV_DIM)
        e2 = jnp.concatenate([st[:, 0:1], st[:, 1:2], zero[:, 1:]], axis=1).reshape(n_s, CONV_DIM)
        ys = _outproj(attn_s, u_s, bg_s, conv_wl, conv_bl, og, w_out_b, ys, ms[2], tm=n_s, n_seq=1, layer=l,
                      ext=(e1, e2), n_tok=t_s)
        rows_s = kvt_s[0].T.reshape(bs, t_s, 3, 2, N_KV, HEAD_DIM)
        outs["cmp_s"].append(rows_s[:, :, 0])
        outs["sel_s"].append(rows_s[:, :, 1])
        new_win_t = kvt_s[0, 4 * KV_DIM:].reshape(feat, bs, t_s).transpose(1, 0, 2)
        outs["win_s"].append(jnp.concatenate([win_t[l], new_win_t], axis=-1)[..., t_s:])
        u_ext = jnp.concatenate([st, u_s.reshape(bs, t_s, CONV_DIM)], axis=1)
        outs["conv_s"].append(u_ext[:, t_s:])

        if l % 2 == 0:
            yp = _dense_ffn(yp, mp[3], mp[4], mp[5], g2, *ffn_b, tm=512, layer=l // 2)
            ys = _dense_ffn(ys, ms[3], ms[4], ms[5], g2, *ffn_b, tm=n_s, layer=l // 2)
        else:
            rw = jnp.zeros((LANES, d), F32).at[:N_EXPERTS].set(jnp.transpose(router_w[l // 2]))
            h_all, route_all = _router(yp, mp[3], mp[4], ys, ms[3], ms[4], g2, rw, tm=MOE_TILE)
            n_pad = h_all.shape[0]
            plan = _moe_plan(route_all[:n_moe], moe_tiles, n_pad)
            out01 = _moe_grouped(h_all, plan, *moe_w, layer=l // 2, n_tok=n_moe)
            yp = _moe_combine(yp, mp[5], route_all, out01, tm=MOE_TILE, row0=0, n_pad=n_pad)
            ys = _moe_combine(ys, ms[5], route_all, out01, tm=n_s, row0=n_p, n_pad=n_pad)

    def rows_major(x):
        return jnp.transpose(x, (0, 1, 5, 2, 3, 4))

    new_win_s = jnp.stack(outs["win_s"]).reshape(depth, bs, 2, N_KV, HEAD_DIM, win_buf)
    return (yp.reshape(bp, t_p, d), ys.reshape(bs, t_s, d),
            rows_major(jnp.stack(outs["cmp_p"])), rows_major(jnp.stack(outs["sel_p"])),
            rows_major(jnp.stack(outs["win_p"])), jnp.stack(outs["conv_p"]),
            jnp.stack(outs["cmp_s"]), jnp.stack(outs["sel_s"]),
            rows_major(new_win_s), jnp.stack(outs["conv_s"]))
```

```python
import functools

import numpy as np
import jax
import jax.numpy as jnp
from jax import lax
from jax.experimental import pallas as pl
from jax.experimental.pallas import tpu as pltpu

F32 = jnp.float32
BF16 = jnp.bfloat16

D_MODEL = 2048
HEAD_DIM = 64
ATTN_DIM = 1024
N_HEADS = 16
N_KV = 4
GQA = 4
KV_DIM = 256
CONV_DIM = 1024
CMP_STRIDE = 16
SEL_BLOCK = 64
N_SEL = 16
WINDOW = 512
N_EXPERTS = 8
TOP_K = 2
KV_OFF = ATTN_DIM
GATE_OFF = KV_OFF + 6 * KV_DIM
CONV_OFF = GATE_OFF + 3 * N_HEADS
IN_COLS = CONV_OFF + 3 * CONV_DIM
EPS = 1e-6
NEG_INF = -1e30
FORCE_BONUS = 1e4
ATTN_SCALE = HEAD_DIM ** -0.5
LANES = 128
VMEM_LIMIT = 56 * 1024 * 1024
MAX_PAGES_PER_STEP = 32
NT_DIMS = (((1,), (1,)), ((), ()))


def _cparams(sem):
    return pltpu.CompilerParams(dimension_semantics=sem, vmem_limit_bytes=VMEM_LIMIT)


def _dot(a, b):
    return jnp.dot(a, b, preferred_element_type=F32)


def _dot_nt(a, b):
    return lax.dot_general(a, b, NT_DIMS, preferred_element_type=F32)


def _silu(x):
    return x * jax.nn.sigmoid(x)


def _split3_dot(x, w_hi):
    x1 = x.astype(BF16)
    r1 = x - x1.astype(F32)
    x2 = r1.astype(BF16)
    x3 = (r1 - x2.astype(F32)).astype(BF16)
    return _dot(x1, w_hi) + _dot(x2, w_hi) + _dot(x3, w_hi)


def _headnorm_lanes(x, gain):
    lane = lax.broadcasted_iota(jnp.int32, (1, LANES), 1)
    lo = lane < HEAD_DIM
    outs = []
    for c in range(x.shape[-1] // LANES):
        xc = x[:, c * LANES:(c + 1) * LANES]
        x2 = xc * xc
        s_lo = jnp.sum(jnp.where(lo, x2, 0.0), axis=-1, keepdims=True)
        s_hi = jnp.sum(jnp.where(lo, 0.0, x2), axis=-1, keepdims=True)
        r = jnp.where(lo, lax.rsqrt(s_lo * (1.0 / HEAD_DIM) + EPS), lax.rsqrt(s_hi * (1.0 / HEAD_DIM) + EPS))
        outs.append(xc * r)
    return jnp.concatenate(outs, axis=-1) * gain


def _masked_softmax(s, mask):
    s = jnp.where(mask, s, NEG_INF)
    m = jnp.max(s, axis=-1, keepdims=True)
    e = jnp.where(mask, jnp.exp(s - m), 0.0)
    return e / jnp.maximum(jnp.sum(e, axis=-1, keepdims=True), 1e-30)


def _rms_modulate(x, g, shift, scale):
    r = lax.rsqrt(jnp.mean(x * x, axis=-1, keepdims=True) + EPS)
    return (x * r * g) * (1.0 + scale) + shift


def _mod_kernel(c_ref, w_ref, b_ref, o_ref):
    s = _silu(c_ref[...])
    o_ref[...] = _dot(s.astype(BF16), w_ref[...].astype(BF16)) + b_ref[...]


def _modulation(c_all, w_ada, b_ada):
    depth, d, cols = w_ada.shape
    rows = c_all.shape[0]
    tn = 1024
    return pl.pallas_call(
        _mod_kernel,
        grid=(depth, cols // tn),
        in_specs=[pl.BlockSpec((rows, d), lambda l, j: (0, 0)),
                  pl.BlockSpec((None, d, tn), lambda l, j: (l, 0, j)),
                  pl.BlockSpec((None, 1, tn), lambda l, j: (l, 0, j))],
        out_specs=pl.BlockSpec((None, rows, tn), lambda l, j: (l, 0, j)),
        out_shape=jax.ShapeDtypeStruct((depth, rows, cols), F32),
        compiler_params=_cparams(("arbitrary", "arbitrary")),
        name="adaln_modulation",
    )(c_all, w_ada, b_ada.reshape(depth, 1, cols))


Q_TILE = 512
KV_TILE = 512
CONV_TILE = 256
N_Q_STEPS = ATTN_DIM // Q_TILE
N_KV_STEPS = 3
N_CONV_STEPS = CONV_DIM // CONV_TILE
N_IN_STEPS = N_Q_STEPS + N_KV_STEPS + N_CONV_STEPS


def _inproj_kernel(y_ref, sh_ref, sc_ref, g1_ref, wq_ref, wkv_ref, wg_ref, wc_ref, qg_ref, kg_ref,
                   qn_ref, kvt_ref, gate_ref, u_ref, bg_ref, h_scr):
    j = pl.program_id(1)

    @pl.when(j == 0)
    def _():
        h = _rms_modulate(y_ref[...], g1_ref[...], sh_ref[...], sc_ref[...])
        h_scr[...] = h.astype(BF16)
        gate_ref[...] = jax.nn.sigmoid(_dot_nt(h_scr[...], wg_ref[...]))

    @pl.when(j < N_Q_STEPS)
    def _():
        p = _dot_nt(h_scr[...], wq_ref[...])
        qn_ref[...] = _headnorm_lanes(p, qg_ref[...]).astype(BF16)

    @pl.when((j >= N_Q_STEPS) & (j < N_Q_STEPS + N_KV_STEPS))
    def _():
        pt = _dot_nt(wkv_ref[...], h_scr[...])
        tok = pt.shape[-1]
        k3 = pt[:KV_DIM].reshape(N_KV, HEAD_DIM, tok)
        ms = jnp.mean(k3 * k3, axis=1, keepdims=True)
        kn = k3 * lax.rsqrt(ms + EPS) * kg_ref[...]
        k3 = jnp.where(j == N_Q_STEPS, k3, kn)
        kvt_ref[:KV_DIM, :] = k3.reshape(KV_DIM, tok)
        kvt_ref[KV_DIM:, :] = pt[KV_DIM:]

    @pl.when(j >= N_Q_STEPS + N_KV_STEPS)
    def _():
        p = _dot_nt(h_scr[...], wc_ref[...])
        u_ref[...] = p[:, 2 * CONV_TILE:] * p[:, :CONV_TILE]
        bg_ref[...] = p[:, CONV_TILE:2 * CONV_TILE]


def _inproj(y, shift, scale, g1, wt_all, wc_all, qg, kg, *, tm, n_seq, layer):
    n, d = y.shape
    t = n // n_seq
    tiles_per_seq = t // tm
    tiles_per_mod = (n // tm) // shift.shape[0]
    r = shift.shape[1]

    def clamp(j, lo, cnt):
        return jnp.clip(j - lo, 0, cnt - 1)

    q_lo, kv_lo, c_lo = 0, N_Q_STEPS, N_Q_STEPS + N_KV_STEPS
    in_specs = [
        pl.BlockSpec((tm, d), lambda i, j: (i, 0)),
        pl.BlockSpec((None, r, d), lambda i, j: (i // tiles_per_mod, 0, 0)),
        pl.BlockSpec((None, r, d), lambda i, j: (i // tiles_per_mod, 0, 0)),
        pl.BlockSpec((1, d), lambda i, j: (0, 0)),
        pl.BlockSpec((None, Q_TILE, d), lambda i, j: (layer, clamp(j, q_lo, N_Q_STEPS), 0)),
        pl.BlockSpec((None, KV_TILE, d), lambda i, j: (layer, KV_OFF // KV_TILE + clamp(j, kv_lo, N_KV_STEPS), 0)),
        pl.BlockSpec((None, LANES, d), lambda i, j: (layer, GATE_OFF // LANES, 0)),
        pl.BlockSpec((None, 3 * CONV_TILE, d), lambda i, j: (layer, clamp(j, c_lo, N_CONV_STEPS), 0)),
        pl.BlockSpec((1, Q_TILE), lambda i, j: (0, 0)),
        pl.BlockSpec((None, N_KV, HEAD_DIM, 1), lambda i, j: (clamp(j, kv_lo, N_KV_STEPS), 0, 0, 0)),
    ]
    out_specs = [
        pl.BlockSpec((tm, Q_TILE), lambda i, j: (i, clamp(j, q_lo, N_Q_STEPS))),
        pl.BlockSpec((None, KV_TILE, tm),
                     lambda i, j: (i // tiles_per_seq, clamp(j, kv_lo, N_KV_STEPS), i % tiles_per_seq)),
        pl.BlockSpec((tm, LANES), lambda i, j: (i, 0)),
        pl.BlockSpec((tm, CONV_TILE), lambda i, j: (i, clamp(j, c_lo, N_CONV_STEPS))),
        pl.BlockSpec((tm, CONV_TILE), lambda i, j: (i, clamp(j, c_lo, N_CONV_STEPS))),
    ]
    out_shape = [
        jax.ShapeDtypeStruct((n, ATTN_DIM), BF16),
        jax.ShapeDtypeStruct((n_seq, 6 * KV_DIM, t), F32),
        jax.ShapeDtypeStruct((n, LANES), F32),
        jax.ShapeDtypeStruct((n, CONV_DIM), F32),
        jax.ShapeDtypeStruct((n, CONV_DIM), F32),
    ]
    return pl.pallas_call(
        _inproj_kernel,
        grid=(n // tm, N_IN_STEPS),
        in_specs=in_specs,
        out_specs=out_specs,
        out_shape=out_shape,
        scratch_shapes=[pltpu.VMEM((tm, d), BF16)],
        compiler_params=_cparams(("arbitrary", "arbitrary")),
        name="input_projection",
    )(y, shift, scale, g1, wt_all, wt_all, wt_all, wc_all, qg, kg)


def _compress_kernel(pt_ref, *refs, p):
    del pt_ref
    pages = refs[:p]
    wbd_ref, w1_ref, pe_ref, b1_ref, w2bd_ref, kg_ref, kc_ref, vc_ref, xs, carry, cconst = refs[p:]
    g = pl.program_id(1)
    cpp = LANES // CMP_STRIDE
    m = p * cpp

    @pl.when(g == 0)
    def _():
        carry[...] = jnp.zeros_like(carry)

    @pl.when((g == 0) & (pl.program_id(0) == 0))
    def _():
        for kv in range(2):
            c64 = jnp.sum(pe_ref[kv] * w1_ref[kv], axis=0, keepdims=True) + b1_ref[kv]
            cconst[kv] = jnp.broadcast_to(jnp.concatenate([c64, c64], axis=-1), (8, LANES))

    for k in range(p):
        for jj in range(4):
            tile = pages[k][jj * LANES:(jj + 1) * LANES, :]
            xs[jj, k * LANES:(k + 1) * LANES, :] = tile.T

    row = lax.broadcasted_iota(jnp.int32, (m, 1), 0)
    for jj in range(4):
        kv = jj // 2
        acc = jnp.zeros((m, 2 * LANES), F32)
        for t in range(CMP_STRIDE):
            acc = acc + _dot(xs[jj, pl.ds(t, m, stride=CMP_STRIDE), :].astype(BF16), wbd_ref[kv, t])
        a_part = acc[:, :LANES]
        b_part = acc[:, LANES:]
        prev = carry[jj]
        a_shift = jnp.where(row == 0, prev[7:8, :], pltpu.roll(a_part, 1, axis=0))
        carry[jj] = a_part[m - 8:, :]
        pre = a_shift + b_part + cconst[kv][0:1, :]
        out = _dot(_silu(pre).astype(BF16), w2bd_ref[kv])
        if kv == 0:
            out = _headnorm_lanes(out, kg_ref[...])
        dst = kc_ref if kv == 0 else vc_ref
        hb = (jj % 2) * 2
        dst[hb] = out[:, :HEAD_DIM].astype(BF16)
        dst[hb + 1] = out[:, HEAD_DIM:].astype(BF16)


def _compress(src, page_ids, page_spec_fn, wbd, w1, pe, b1, w2bd, kg0):
    n_seq, n_pages = page_ids.shape
    p = min(MAX_PAGES_PER_STEP, n_pages)
    assert n_pages % p == 0
    m = p * (LANES // CMP_STRIDE)
    n_groups = n_pages // p
    n_rows = n_pages * (LANES // CMP_STRIDE)
    page_specs = [page_spec_fn(k, p) for k in range(p)]

    def const(shape):
        return pl.BlockSpec(shape, lambda b, g, pt: (0,) * len(shape))

    in_specs = page_specs + [const(wbd.shape), const(w1.shape), const(pe.shape), const(b1.shape),
                             const(w2bd.shape), const(kg0.shape)]
    out_spec = pl.BlockSpec((None, N_KV, m, HEAD_DIM), lambda b, g, pt: (b, 0, g, 0))
    return pl.pallas_call(
        functools.partial(_compress_kernel, p=p),
        grid_spec=pltpu.PrefetchScalarGridSpec(
            num_scalar_prefetch=1,
            grid=(n_seq, n_groups),
            in_specs=in_specs,
            out_specs=[out_spec, out_spec],
            scratch_shapes=[pltpu.VMEM((4, p * LANES, LANES), F32), pltpu.VMEM((4, 8, LANES), F32),
                            pltpu.VMEM((2, 8, LANES), F32)],
        ),
        out_shape=[jax.ShapeDtypeStruct((n_seq, N_KV, n_rows, HEAD_DIM), BF16)] * 2,
        compiler_params=_cparams(("arbitrary", "arbitrary")),
        name="compress_mlp",
    )(page_ids, *([src] * p), wbd, w1, pe, b1, w2bd, kg0)


SEL_CHUNK = 512


def _attn_kernel(q_ref, gate_ref, kc_ref, vc_ref, kst_ref, vst_ref, kwt_ref, vwt_ref, agg_ref, exp_ref,
                 o_ref, bias_scr, *, tq, t_len):
    i = pl.program_id(2)
    q0 = i * tq
    m_rows = GQA * tq
    n_sel = t_len // SEL_BLOCK
    win_keys = WINDOW + tq
    q = q_ref[...]
    qs = jnp.concatenate([q[:, g * HEAD_DIM:(g + 1) * HEAD_DIM] for g in range(GQA)], axis=0)
    row = lax.broadcasted_iota(jnp.int32, (m_rows, 1), 0)
    qpos = q0 + (row % tq)
    qp = q0 + lax.broadcasted_iota(jnp.int32, (tq, 1), 0)

    s_c = _dot_nt(qs, kc_ref[...])
    r_i = lax.broadcasted_iota(jnp.int32, (1, s_c.shape[-1]), 1)
    mask_c = (r_i >= 1) & (CMP_STRIDE * r_i + (CMP_STRIDE - 1) <= qpos)
    p_c = _masked_softmax(s_c, mask_c)
    o_c = _dot(p_c.astype(BF16), vc_ref[...])
    imp = p_c[0:tq] + p_c[tq:2 * tq] + p_c[2 * tq:3 * tq] + p_c[3 * tq:4 * tq]
    imp_s = _split3_dot(imp, agg_ref[...])

    sidx = lax.broadcasted_iota(jnp.int32, (1, LANES), 1)
    cur = qp // SEL_BLOCK
    forced = (sidx == 0) | (sidx == cur) | (sidx == cur - 1)
    causal = sidx * SEL_BLOCK <= qp
    score = jnp.where(causal, imp_s + jnp.where(forced, FORCE_BONUS, 0.0), NEG_INF)
    sv = score.T[:n_sel]
    srow = lax.broadcasted_iota(jnp.int32, (n_sel, 1), 0)
    cnt = jnp.zeros((n_sel, tq), jnp.int32)
    for t in range(n_sel):
        other = sv[t:t + 1, :]
        beats = (other > sv) | ((other == sv) & (t < srow))
        cnt = cnt + beats.astype(jnp.int32)
    sel_t = jnp.where((cnt < min(N_SEL, n_sel)) & (sv > NEG_INF / 2), 1.0, 0.0)
    sel = jnp.concatenate([sel_t, jnp.zeros((LANES - n_sel, tq), F32)], axis=0).T
    selm = _dot(sel.astype(BF16), exp_ref[...])
    kpos_all = lax.broadcasted_iota(jnp.int32, (1, t_len), 1)
    bias_scr[...] = jnp.where((selm > 0.5) & (kpos_all <= qp), 0.0, NEG_INF)

    n_chunks = (q0 + tq + SEL_CHUNK - 1) // SEL_CHUNK

    def body(c, carry):
        m_old, l_old, acc = carry
        off = pl.multiple_of(c * SEL_CHUNK, SEL_CHUNK)
        kt = kst_ref[:, pl.ds(off, SEL_CHUNK)].astype(BF16)
        s = _dot(qs, kt).reshape(GQA, tq, SEL_CHUNK) + bias_scr[:, pl.ds(off, SEL_CHUNK)][None]
        s = s.reshape(m_rows, SEL_CHUNK)
        m_new = jnp.maximum(m_old, jnp.max(s, axis=-1, keepdims=True))
        alpha = jnp.exp(m_old - m_new)
        pr = jnp.exp(s - m_new)
        l_new = alpha * l_old + jnp.sum(pr, axis=-1, keepdims=True)
        vt = vst_ref[:, pl.ds(off, SEL_CHUNK)].astype(BF16)
        acc = alpha * acc + _dot_nt(pr.astype(BF16), vt)
        return m_new, l_new, acc

    init = (jnp.full((m_rows, 1), NEG_INF, F32), jnp.zeros((m_rows, 1), F32), jnp.zeros((m_rows, HEAD_DIM), F32))
    _, l_s, acc_s = lax.fori_loop(0, n_chunks, body, init)
    o_s = acc_s / jnp.maximum(l_s, 1e-30)

    start = pl.multiple_of(jnp.clip(q0 - WINDOW, 0, t_len - win_keys), LANES)
    dist = qp - (start + lax.broadcasted_iota(jnp.int32, (1, win_keys), 1))
    bias_w = jnp.where((dist >= 0) & (dist < WINDOW), 0.0, NEG_INF)
    s_w = _dot(qs, kwt_ref[:, pl.ds(start, win_keys)].astype(BF16)).reshape(GQA, tq, win_keys) + bias_w[None]
    s_w = s_w.reshape(m_rows, win_keys)
    e_w = jnp.exp(s_w - jnp.max(s_w, axis=-1, keepdims=True))
    p_w = e_w / jnp.maximum(jnp.sum(e_w, axis=-1, keepdims=True), 1e-30)
    o_w = _dot_nt(p_w.astype(BF16), vwt_ref[:, pl.ds(start, win_keys)].astype(BF16))

    gate = gate_ref[...]

    def gcol(jb):
        return jnp.concatenate([jnp.broadcast_to(gate[:, g * 3 + jb:g * 3 + jb + 1], (tq, HEAD_DIM))
                                for g in range(GQA)], axis=0)

    o = gcol(0) * o_c + gcol(1) * o_s + gcol(2) * o_w
    o_ref[...] = jnp.concatenate([o[g * tq:(g + 1) * tq] for g in range(GQA)], axis=-1)


def _prompt_attention(qn, gate_hm, kc, vc, kvt, agg, expand, *, tq):
    n = qn.shape[0]
    n_seq, _, t_len = kvt.shape
    tiles = t_len // tq
    kern = functools.partial(_attn_kernel, tq=tq, t_len=t_len)

    def kv_spec(block_row):
        return pl.BlockSpec((None, HEAD_DIM, t_len), lambda b, h, i: (b, block_row + h, 0))

    sel_k, sel_v = (2 * KV_DIM) // HEAD_DIM, (3 * KV_DIM) // HEAD_DIM
    win_k, win_v = (4 * KV_DIM) // HEAD_DIM, (5 * KV_DIM) // HEAD_DIM
    return pl.pallas_call(
        kern,
        grid=(n_seq, N_KV, tiles),
        in_specs=[
            pl.BlockSpec((tq, GQA * HEAD_DIM), lambda b, h, i: (b * tiles + i, h)),
            pl.BlockSpec((None, tq, 3 * GQA), lambda b, h, i: (h, b * tiles + i, 0)),
            pl.BlockSpec((None, None, kc.shape[2], HEAD_DIM), lambda b, h, i: (b, h, 0, 0)),
            pl.BlockSpec((None, None, vc.shape[2], HEAD_DIM), lambda b, h, i: (b, h, 0, 0)),
            kv_spec(sel_k), kv_spec(sel_v), kv_spec(win_k), kv_spec(win_v),
            pl.BlockSpec(agg.shape, lambda b, h, i: (0, 0)),
            pl.BlockSpec(expand.shape, lambda b, h, i: (0, 0)),
        ],
        out_specs=pl.BlockSpec((tq, GQA * HEAD_DIM), lambda b, h, i: (b * tiles + i, h)),
        out_shape=jax.ShapeDtypeStruct((n, ATTN_DIM), F32),
        scratch_shapes=[pltpu.VMEM((tq, t_len), F32)],
        compiler_params=_cparams(("arbitrary", "arbitrary", "arbitrary")),
        name="prompt_attention",
    )(qn, gate_hm, kc, vc, kvt, kvt, kvt, kvt, agg, expand)


def _sample_select_kernel(qs_ref, kc_ref, vc_ref, agg_ref, oc_ref, idx_ref, *, q0, n_tok, n_sel, n_pick):
    rows = GQA * n_tok
    row = lax.broadcasted_iota(jnp.int32, (rows, 1), 0)
    qpos = q0 + (row % n_tok)
    imps = []
    for h in range(N_KV):
        s_c = _dot_nt(qs_ref[h].astype(BF16), kc_ref[h])
        r_i = lax.broadcasted_iota(jnp.int32, (1, s_c.shape[-1]), 1)
        mask_c = (r_i >= 1) & (CMP_STRIDE * r_i + (CMP_STRIDE - 1) <= qpos)
        p_c = _masked_softmax(s_c, mask_c)
        oc_ref[h] = _dot(p_c.astype(BF16), vc_ref[h])
        imp = p_c[0:n_tok]
        for g in range(1, GQA):
            imp = imp + p_c[g * n_tok:(g + 1) * n_tok]
        imps.append(imp)
    imp_all = jnp.concatenate(imps, axis=0)
    imp_s = _split3_dot(imp_all, agg_ref[...])
    width = imp_s.shape[-1]
    sidx = lax.broadcasted_iota(jnp.int32, (1, width), 1)
    r2 = lax.broadcasted_iota(jnp.int32, (N_KV * n_tok, 1), 0)
    qp = q0 + (r2 % n_tok)
    cur = qp // SEL_BLOCK
    forced = (sidx == 0) | (sidx == cur) | (sidx == cur - 1)
    causal = sidx * SEL_BLOCK <= qp
    score = jnp.where(causal, imp_s + jnp.where(forced, FORCE_BONUS, 0.0), NEG_INF)
    lowest = -3.0e38
    score = jnp.where((sidx >= n_sel - 1), lowest, score)
    out_lane = lax.broadcasted_iota(jnp.int32, (1, LANES), 1)
    picked = jnp.zeros((N_KV * n_tok, LANES), jnp.int32)
    for it in range(n_pick):
        mx = jnp.max(score, axis=-1, keepdims=True)
        first = jnp.min(jnp.where(score == mx, sidx, width), axis=-1, keepdims=True)
        picked = jnp.where(out_lane == it, first, picked)
        score = jnp.where(sidx == first, lowest, score)
    idx_ref[...] = picked


def _sample_select(qs, kc, vc, agg, *, q0, n_tok, n_sel):
    n_seq = qs.shape[0]
    n_pick = N_SEL - 1
    kern = functools.partial(_sample_select_kernel, q0=q0, n_tok=n_tok, n_sel=n_sel, n_pick=n_pick)
    return pl.pallas_call(
        kern,
        grid=(n_seq,),
        in_specs=[
            pl.BlockSpec((None, N_KV, GQA * n_tok, HEAD_DIM), lambda b: (b, 0, 0, 0)),
            pl.BlockSpec((None, N_KV, kc.shape[2], HEAD_DIM), lambda b: (b, 0, 0, 0)),
            pl.BlockSpec((None, N_KV, vc.shape[2], HEAD_DIM), lambda b: (b, 0, 0, 0)),
            pl.BlockSpec(agg.shape, lambda b: (0, 0)),
        ],
        out_specs=[
            pl.BlockSpec((None, N_KV, GQA * n_tok, HEAD_DIM), lambda b: (b, 0, 0, 0)),
            pl.BlockSpec((None, N_KV * n_tok, LANES), lambda b: (b, 0, 0)),
        ],
        out_shape=[
            jax.ShapeDtypeStruct((n_seq, N_KV, GQA * n_tok, HEAD_DIM), F32),
            jax.ShapeDtypeStruct((n_seq, N_KV * n_tok, LANES), jnp.int32),
        ],
        compiler_params=_cparams(("arbitrary",)),
        name="sample_select",
    )(qs, kc, vc, agg)


def _sample_attend_kernel(pg_ref, half_ref, q_ref, oc_ref, gate_ref, knew_ref, vnew_ref, kwp_ref, vwp_ref,
                          kwn_ref, vwn_ref, sel_hbm, o_ref, kbuf, vbuf, sem, *, n_tok, n_pick, layer):
    n_pages = n_tok * n_pick
    b = pl.program_id(0)
    h = pl.program_id(1)
    step = b * N_KV + h
    n_steps = pl.num_programs(0) * N_KV
    slot = step % 2

    def gather_start(stp, s):
        row0 = pl.multiple_of((stp % N_KV) * HEAD_DIM, HEAD_DIM)
        for tk in range(n_pages):
            page = pg_ref[stp * n_pages + tk]
            pltpu.make_async_copy(sel_hbm.at[layer, page, pl.ds(row0, HEAD_DIM)], kbuf.at[s, tk], sem.at[s]).start()
            pltpu.make_async_copy(sel_hbm.at[layer, page, pl.ds(KV_DIM + row0, HEAD_DIM)], vbuf.at[s, tk],
                                  sem.at[s]).start()

    @pl.when(step == 0)
    def _():
        gather_start(0, 0)

    @pl.when(step + 1 < n_steps)
    def _():
        gather_start(step + 1, 1 - slot)

    for buf in (kbuf, vbuf):
        pltpu.make_async_copy(sel_hbm.at[layer, pl.ds(0, n_pages), pl.ds(0, HEAD_DIM)], buf.at[slot], sem.at[slot]).wait()
    lane = lax.broadcasted_iota(jnp.int32, (1, LANES), 1)
    n_new = knew_ref.shape[-1]
    new_lane = lax.broadcasted_iota(jnp.int32, (1, n_new), 1)
    n_win = kwp_ref.shape[-1]
    wl = lax.broadcasted_iota(jnp.int32, (1, n_win), 1)
    k_new, v_new = knew_ref[...].astype(BF16), vnew_ref[...].astype(BF16)
    kw_past, vw_past = kwp_ref[...].astype(BF16), vwp_ref[...].astype(BF16)
    kw_new, vw_new = kwn_ref[...].astype(BF16), vwn_ref[...].astype(BF16)

    for t in range(n_tok):
        q = q_ref[t].astype(BF16)
        new_mask = (new_lane // n_tok == b) & (new_lane % n_tok <= t)

        flat = ((b * N_KV + h) * n_tok + t) * n_pick
        scores, masks = [], []
        for k in range(n_pick):
            half = half_ref[flat + k]
            scores.append(_dot(q, kbuf[slot, t * n_pick + k].astype(BF16)))
            masks.append(lane // SEL_BLOCK == half)
        s_new = _dot(q, k_new)
        mx = jnp.max(jnp.where(new_mask, s_new, NEG_INF), axis=-1, keepdims=True)
        for s, mk in zip(scores, masks):
            mx = jnp.maximum(mx, jnp.max(jnp.where(mk, s, NEG_INF), axis=-1, keepdims=True))
        e_new = jnp.exp(jnp.where(new_mask, s_new, NEG_INF) - mx) * new_mask.astype(F32)
        den = jnp.sum(e_new, axis=-1, keepdims=True)
        acc = _dot_nt(e_new.astype(BF16), v_new)
        for k in range(n_pick):
            e = jnp.exp(jnp.where(masks[k], scores[k], NEG_INF) - mx) * masks[k].astype(F32)
            den = den + jnp.sum(e, axis=-1, keepdims=True)
            acc = acc + _dot_nt(e.astype(BF16), vbuf[slot, t * n_pick + k].astype(BF16))
        o_s = acc / jnp.maximum(den, 1e-30)

        past_mask = (n_win - wl + t) < WINDOW
        s_p = jnp.where(past_mask, _dot(q, kw_past), NEG_INF)
        s_n = jnp.where(new_mask, _dot(q, kw_new), NEG_INF)
        mw = jnp.maximum(jnp.max(s_p, axis=-1, keepdims=True), jnp.max(s_n, axis=-1, keepdims=True))
        e_p = jnp.exp(s_p - mw) * past_mask.astype(F32)
        e_n = jnp.exp(s_n - mw) * new_mask.astype(F32)
        den_w = jnp.sum(e_p, axis=-1, keepdims=True) + jnp.sum(e_n, axis=-1, keepdims=True)
        acc_w = _dot_nt(e_p.astype(BF16), vw_past) + _dot_nt(e_n.astype(BF16), vw_new)
        o_w = acc_w / jnp.maximum(den_w, 1e-30)

        gate = gate_ref[t]
        o_ref[t] = gate[:, 0:1] * oc_ref[t] + gate[:, 1:2] * o_s + gate[:, 2:3] * o_w


def _sample_attend(page_of, half_of, q8, oc8, gate8, kvt_new, cache_sel_t, cache_win_t, *, n_tok, n_pick, layer):
    n_seq = q8.shape[0]
    kern = functools.partial(_sample_attend_kernel, n_tok=n_tok, n_pick=n_pick, layer=layer)
    n_new = kvt_new.shape[-1]
    n_win = cache_win_t.shape[-1]
    n_pages = n_tok * n_pick
    assert cache_sel_t.shape[1] >= n_pages

    def qspec(last):
        return pl.BlockSpec((None, None, n_tok, 8, last), lambda b, h, pg, hf: (b, h, 0, 0, 0))

    def new_spec(block_row):
        return pl.BlockSpec((None, HEAD_DIM, n_new), lambda b, h, pg, hf: (0, block_row + h, 0))

    def win_spec(block_row):
        return pl.BlockSpec((None, None, HEAD_DIM, n_win), lambda b, h, pg, hf: (layer, b, block_row + h, 0))

    sel_k, sel_v = (2 * KV_DIM) // HEAD_DIM, (3 * KV_DIM) // HEAD_DIM
    win_k, win_v = (4 * KV_DIM) // HEAD_DIM, (5 * KV_DIM) // HEAD_DIM
    in_specs = [qspec(HEAD_DIM), qspec(HEAD_DIM), qspec(3), new_spec(sel_k), new_spec(sel_v),
                win_spec(0), win_spec(N_KV), new_spec(win_k), new_spec(win_v), pl.BlockSpec(memory_space=pl.ANY)]
    return pl.pallas_call(
        kern,
        grid_spec=pltpu.PrefetchScalarGridSpec(
            num_scalar_prefetch=2,
            grid=(n_seq, N_KV),
            in_specs=in_specs,
            out_specs=qspec(HEAD_DIM),
            scratch_shapes=[pltpu.VMEM((2, n_pages, HEAD_DIM, LANES), F32), pltpu.VMEM((2, n_pages, HEAD_DIM, LANES), F32),
                            pltpu.SemaphoreType.DMA((2,))],
        ),
        out_shape=jax.ShapeDtypeStruct(q8.shape, F32),
        compiler_params=_cparams(("arbitrary", "arbitrary")),
        name="sample_attend",
    )(page_of, half_of, q8, oc8, gate8, kvt_new, kvt_new, cache_win_t, cache_win_t, kvt_new, kvt_new, cache_sel_t)


def _mix_prologue(attn, u, u1, u2, bg, cw_ref, cb_ref, og_ref, o_scr):
    og = og_ref[...]
    an = attn * lax.rsqrt(jnp.mean(attn * attn, axis=-1, keepdims=True) + EPS) * og[:, :ATTN_DIM]
    yv = cb_ref[...] + cw_ref[0:1, :] * u2
    yv = yv + cw_ref[1:2, :] * u1
    yv = yv + cw_ref[2:3, :] * u
    cv = bg * yv
    cn = cv * lax.rsqrt(jnp.mean(cv * cv, axis=-1, keepdims=True) + EPS) * og[:, ATTN_DIM:]
    o_scr[:, :ATTN_DIM] = an.astype(BF16)
    o_scr[:, ATTN_DIM:] = cn.astype(BF16)


def _outproj_seq_kernel(attn_ref, u_ref, halo_ref, bg_ref, cw_ref, cb_ref, og_ref, w_ref, y_ref, gm_ref,
                        o_ref, o_scr, *, tiles_per_seq):
    i = pl.program_id(0)

    @pl.when(pl.program_id(1) == 0)
    def _():
        u = u_ref[...]
        tm = u.shape[0]
        row = lax.broadcasted_iota(jnp.int32, (tm, 1), 0)
        halo = jnp.where(i % tiles_per_seq == 0, 0.0, halo_ref[...])
        p1 = halo[7:8, :]
        p2 = halo[6:7, :]
        u1 = jnp.where(row == 0, p1, pltpu.roll(u, 1, axis=0))
        u2 = jnp.where(row == 0, p2, jnp.where(row == 1, p1, pltpu.roll(u, 2, axis=0)))
        _mix_prologue(attn_ref[...], u, u1, u2, bg_ref[...], cw_ref, cb_ref, og_ref, o_scr)

    o_ref[...] = y_ref[...] + gm_ref[...] * _dot(o_scr[...], w_ref[...])


def _outproj_state_kernel(attn_ref, u_ref, e1_ref, e2_ref, bg_ref, cw_ref, cb_ref, og_ref, w_ref, y_ref, gm_ref,
                          o_ref, o_scr, *, n_tok):
    @pl.when(pl.program_id(1) == 0)
    def _():
        u = u_ref[...]
        tm = u.shape[0]
        k = lax.broadcasted_iota(jnp.int32, (tm, 1), 0) % n_tok
        u1 = jnp.where(k >= 1, pltpu.roll(u, 1, axis=0), e1_ref[...])
        u2 = jnp.where(k >= 2, pltpu.roll(u, 2, axis=0), e2_ref[...])
        _mix_prologue(attn_ref[...], u, u1, u2, bg_ref[...], cw_ref, cb_ref, og_ref, o_scr)

    o_ref[...] = y_ref[...] + gm_ref[...] * _dot(o_scr[...], w_ref[...])


def _outproj(attn, u, bg, cw, cb, og, w, y, gmod, *, tm, n_seq, layer, ext=None, n_tok=None):
    n, d = y.shape
    tn = 512
    tiles_per_seq = (n // n_seq) // tm
    tiles_per_mod = (n // tm) // gmod.shape[0]
    r = gmod.shape[1]
    full = lambda i, j: (i, 0)
    c0 = lambda i, j: (0, 0)
    common = [pl.BlockSpec((tm, CONV_DIM), full), pl.BlockSpec((3, CONV_DIM), c0), pl.BlockSpec((1, CONV_DIM), c0),
              pl.BlockSpec((1, d), c0), pl.BlockSpec((None, d, tn), lambda i, j: (layer, 0, j)),
              pl.BlockSpec((tm, tn), lambda i, j: (i, j)),
              pl.BlockSpec((None, r, tn), lambda i, j: (i // tiles_per_mod, 0, j))]
    head = [pl.BlockSpec((tm, ATTN_DIM), full), pl.BlockSpec((tm, CONV_DIM), full)]
    if ext is None:
        kern = functools.partial(_outproj_seq_kernel, tiles_per_seq=tiles_per_seq)
        extra = [pl.BlockSpec((8, CONV_DIM), lambda i, j: (jnp.maximum(i * (tm // 8) - 1, 0), 0))]
        args = (attn, u, u, bg, cw, cb, og, w, y, gmod)
    else:
        kern = functools.partial(_outproj_state_kernel, n_tok=n_tok)
        extra = [pl.BlockSpec((tm, CONV_DIM), full), pl.BlockSpec((tm, CONV_DIM), full)]
        args = (attn, u, ext[0], ext[1], bg, cw, cb, og, w, y, gmod)
    return pl.pallas_call(
        kern,
        grid=(n // tm, d // tn),
        in_specs=head + extra + common,
        out_specs=pl.BlockSpec((tm, tn), lambda i, j: (i, j)),
        out_shape=jax.ShapeDtypeStruct((n, d), F32),
        scratch_shapes=[pltpu.VMEM((tm, d), BF16)],
        compiler_params=_cparams(("arbitrary", "arbitrary")),
        name="output_projection",
    )(*args)


def _ffn_kernel(y_ref, sh_ref, sc_ref, gm_ref, g2_ref, w1_ref, w3_ref, w2_ref, o_ref, h_scr, acc):
    j = pl.program_id(1)

    @pl.when(j == 0)
    def _():
        h_scr[...] = _rms_modulate(y_ref[...], g2_ref[...], sh_ref[...], sc_ref[...]).astype(BF16)
        acc[...] = jnp.zeros_like(acc)

    h = h_scr[...]
    hid = _silu(_dot(h, w1_ref[...])) * _dot(h, w3_ref[...])
    acc[...] += _dot(hid.astype(BF16), w2_ref[...])

    @pl.when(j == pl.num_programs(1) - 1)
    def _():
        o_ref[...] = y_ref[...] + gm_ref[...] * acc[...]


def _dense_ffn(y, shift, scale, gmod, g2, w1, w3, w2, *, tm, layer):
    n, d = y.shape
    f = w1.shape[-1]
    tf = 512
    tiles_per_mod = (n // tm) // shift.shape[0]
    r = shift.shape[1]
    mod = pl.BlockSpec((None, r, d), lambda i, j: (i // tiles_per_mod, 0, 0))
    return pl.pallas_call(
        _ffn_kernel,
        grid=(n // tm, f // tf),
        in_specs=[pl.BlockSpec((tm, d), lambda i, j: (i, 0)), mod, mod, mod,
                  pl.BlockSpec((1, d), lambda i, j: (0, 0)),
                  pl.BlockSpec((None, d, tf), lambda i, j: (layer, 0, j)),
                  pl.BlockSpec((None, d, tf), lambda i, j: (layer, 0, j)),
                  pl.BlockSpec((None, tf, d), lambda i, j: (layer, j, 0))],
        out_specs=pl.BlockSpec((tm, d), lambda i, j: (i, 0)),
        out_shape=jax.ShapeDtypeStruct((n, d), F32),
        scratch_shapes=[pltpu.VMEM((tm, d), BF16), pltpu.VMEM((tm, d), F32)],
        compiler_params=_cparams(("arbitrary", "arbitrary")),
        name="dense_ffn",
    )(y, shift, scale, gmod, g2, w1, w3, w2)


def _top2_route(h, h_bf, rw):
    lane = lax.broadcasted_iota(jnp.int32, (1, LANES), 1)
    rw_hi = rw.astype(BF16)
    rw_lo = (rw - rw_hi.astype(F32)).astype(BF16)
    h_lo = (h - h_bf.astype(F32)).astype(BF16)
    logits = _dot_nt(h_bf, rw_hi) + _dot_nt(h_lo, rw_hi) + _dot_nt(h_bf, rw_lo)
    lowest = -3.0e38
    logits = jnp.where(lane < N_EXPERTS, logits, lowest)
    m1 = jnp.max(logits, axis=-1, keepdims=True)
    i1 = jnp.min(jnp.where(logits == m1, lane, LANES), axis=-1, keepdims=True)
    rest = jnp.where(lane == i1, lowest, logits)
    m2 = jnp.max(rest, axis=-1, keepdims=True)
    i2 = jnp.min(jnp.where(rest == m2, lane, LANES), axis=-1, keepdims=True)
    e2 = jnp.exp(m2 - m1)
    den = 1.0 + e2
    return i1, i2, 1.0 / den, e2 / den


MOE_TILE = 512
MOE_FT = 256


def _route_rows(y_ref, sh_ref, sc_ref, g2_ref, rw_ref, h_ref, route_ref):
    h = _rms_modulate(y_ref[...], g2_ref[...], sh_ref[...], sc_ref[...])
    h_ref[...] = h
    i1, i2, w1, w2 = _top2_route(h, h.astype(BF16), rw_ref[...])
    lane = lax.broadcasted_iota(jnp.int32, (1, LANES), 1)
    route_ref[...] = (jnp.where(lane == 0, i1.astype(F32), 0.0) + jnp.where(lane == 1, i2.astype(F32), 0.0)
                      + jnp.where(lane == 2, w1, 0.0) + jnp.where(lane == 3, w2, 0.0))


def _router_kernel(y_ref, sh_ref, sc_ref, y2_ref, sh2_ref, sc2_ref, g2_ref, rw_ref, h_ref, route_ref):
    real = pl.program_id(0) < pl.num_programs(0) - 1

    @pl.when(real)
    def _():
        _route_rows(y_ref, sh_ref, sc_ref, g2_ref, rw_ref, h_ref, route_ref)

    @pl.when(jnp.logical_not(real))
    def _():
        n2 = y2_ref.shape[0]
        h_ref[...] = jnp.zeros_like(h_ref)
        route_ref[...] = jnp.zeros_like(route_ref)
        _route_rows(y2_ref, sh2_ref, sc2_ref, g2_ref, rw_ref, h_ref.at[pl.ds(0, n2)], route_ref.at[pl.ds(0, n2)])


def _router(y, shift, scale, y2, shift2, scale2, g2, rw, *, tm):
    n, d = y.shape
    n2 = y2.shape[0]
    assert n2 <= tm and shift2.shape[:2] == (1, n2)
    tiles = n // tm
    tiles_per_mod = tiles // shift.shape[0]
    r = shift.shape[1]
    last = tiles - 1
    mod = pl.BlockSpec((None, r, d), lambda i: (jnp.minimum(i, last) // tiles_per_mod, 0, 0))
    mod2 = pl.BlockSpec((None, n2, d), lambda i: (0, 0, 0))
    return pl.pallas_call(
        _router_kernel,
        grid=(tiles + 1,),
        in_specs=[pl.BlockSpec((tm, d), lambda i: (jnp.minimum(i, last), 0)), mod, mod,
                  pl.BlockSpec((n2, d), lambda i: (0, 0)), mod2, mod2,
                  pl.BlockSpec((1, d), lambda i: (0, 0)), pl.BlockSpec((LANES, d), lambda i: (0, 0))],
        out_specs=[pl.BlockSpec((tm, d), lambda i: (i, 0)), pl.BlockSpec((tm, LANES), lambda i: (i, 0))],
        out_shape=[jax.ShapeDtypeStruct((n + tm, d), F32), jax.ShapeDtypeStruct((n + tm, LANES), F32)],
        compiler_params=_cparams(("arbitrary",)),
        name="moe_router",
    )(y, shift, scale, y2, shift2, scale2, g2, rw)


def _moe_plan(route, n_tiles, n_pad):
    n = route.shape[0]
    flat_e = route[:, :2].astype(jnp.int32).reshape(-1)
    onehot = (flat_e[:, None] == jnp.arange(N_EXPERTS, dtype=jnp.int32)[None, :]).astype(jnp.int32)
    csum = jnp.cumsum(onehot, axis=0)
    pos = jnp.sum((csum - 1) * onehot, axis=1)
    sizes = csum[-1]
    padded = ((sizes + MOE_TILE - 1) // MOE_TILE) * MOE_TILE
    ends = jnp.cumsum(padded)
    dest = (ends - padded)[flat_e] + pos
    rows = n_tiles * MOE_TILE
    a_of_row = jnp.full((rows,), -1, jnp.int32).at[dest].set(jnp.arange(2 * n, dtype=jnp.int32),
                                                             unique_indices=True)
    real = a_of_row >= 0
    a0 = jnp.maximum(a_of_row, 0)
    src = a0 // 2
    spare = 2 * n_pad + jnp.arange(rows, dtype=jnp.int32) % MOE_TILE
    dst = jnp.where(real, (a0 % 2) * n_pad + a0 // 2, spare)
    tile_start = jnp.arange(n_tiles, dtype=jnp.int32) * MOE_TILE
    te = jnp.sum((tile_start[:, None] >= ends[None, :]).astype(jnp.int32), axis=1)
    tv = (te < N_EXPERTS).astype(jnp.int32)
    return jnp.minimum(te, N_EXPERTS - 1), tv, src, dst


def _row_copy(src_ref, src_row, dst_ref, dst_row, sem):
    return pltpu.make_async_copy(src_ref.at[pl.ds(src_row, 1)], dst_ref.at[pl.ds(dst_row, 1)], sem)


ROW_LOOP_UNROLL = 8


def _moe_group_kernel(te_ref, tv_ref, src_ref, dst_ref, h_hbm, w1a_ref, w3a_ref, w2a_ref, w1b_ref, w3b_ref, w2b_ref,
                      out_hbm, xbuf, hb, acc, obuf, gsem, ssem, *, n_f, n_tok, n_pad):
    del te_ref
    i = pl.program_id(0)
    j = pl.program_id(1)
    n_t = pl.num_programs(0)
    n_j = pl.num_programs(1)
    valid = tv_ref[i] == 1
    slot = i % 2

    def row_loop(body):
        def step(r, c):
            body(r)
            return c
        lax.fori_loop(0, MOE_TILE, step, 0, unroll=ROW_LOOP_UNROLL)

    def gather_start(tile, s):
        row_loop(lambda r: _row_copy(h_hbm, src_ref[tile * MOE_TILE + r], xbuf.at[s], r, gsem.at[s]).start())

    def gather_wait(s):
        pltpu.make_async_copy(h_hbm.at[pl.ds(0, MOE_TILE)], xbuf.at[s], gsem.at[s]).wait()

    def scatter_start(tile, s):
        row_loop(lambda r: _row_copy(obuf.at[s], r, out_hbm, dst_ref[tile * MOE_TILE + r], ssem.at[s]).start())

    def scatter_wait(s):
        pltpu.make_async_copy(obuf.at[s], out_hbm.at[pl.ds(0, MOE_TILE)], ssem.at[s]).wait()

    @pl.when(valid & (j == 0))
    def _():
        @pl.when(i == 0)
        def _():
            gather_start(0, 0)
            obuf[1] = jnp.zeros(obuf.shape[1:], F32)
            fills = [pltpu.make_async_copy(obuf.at[1], out_hbm.at[pl.ds(2 * n_pad, MOE_TILE)], ssem.at[1])]
            if n_pad > n_tok:
                fills += [pltpu.make_async_copy(obuf.at[1, pl.ds(0, n_pad - n_tok)],
                                                out_hbm.at[pl.ds(s0 * n_pad + n_tok, n_pad - n_tok)], ssem.at[1])
                          for s0 in range(2)]
            for cp in fills:
                cp.start()
            for cp in fills:
                cp.wait()

        gather_wait(slot)
        nxt = jnp.minimum(i + 1, n_t - 1)

        @pl.when((i + 1 < n_t) & (tv_ref[nxt] == 1))
        def _():
            gather_start(i + 1, 1 - slot)

        hb[...] = xbuf[slot].astype(BF16)
        acc[...] = jnp.zeros_like(acc)

    def expert_block(w1_ref, w3_ref, w2_ref):
        h = hb[...]
        hid = _silu(_dot(h, w1_ref[...].astype(BF16))) * _dot(h, w3_ref[...].astype(BF16))
        return _dot(hid.astype(BF16), w2_ref[...].astype(BF16))

    second = 2 * j + 1 < n_f

    @pl.when(valid & second)
    def _():
        acc[...] += expert_block(w1a_ref, w3a_ref, w2a_ref) + expert_block(w1b_ref, w3b_ref, w2b_ref)

    @pl.when(valid & jnp.logical_not(second))
    def _():
        acc[...] += expert_block(w1a_ref, w3a_ref, w2a_ref)

    @pl.when(valid & (j == n_j - 1))
    def _():
        obuf[slot] = acc[...]

        @pl.when(i >= 1)
        def _():
            scatter_wait(1 - slot)

        scatter_start(i, slot)

        @pl.when(i == n_t - 1)
        def _():
            scatter_wait(slot)

    prev = jnp.maximum(i - 1, 0)

    @pl.when(jnp.logical_not(valid) & (j == 0) & (i >= 1) & (tv_ref[prev] == 1))
    def _():
        scatter_wait(1 - slot)


def _moe_grouped(h, plan, w1, w3, w2, *, layer, n_tok):
    te, tv, src, dst = plan
    n_pad, d = h.shape
    assert n_pad - n_tok <= MOE_TILE
    fe = w1.shape[-1]
    n_tiles = te.shape[0]
    n_f = fe // MOE_FT

    def blk(j, tv_i, half):
        return jnp.minimum(2 * j + half, n_f - 1) * tv_i

    def w13(half):
        return pl.BlockSpec((None, None, d, MOE_FT),
                            lambda i, j, te_r, tv_r, s_r, d_r: (layer, te_r[i], 0, blk(j, tv_r[i], half)))

    def w2m(half):
        return pl.BlockSpec((None, None, MOE_FT, d),
                            lambda i, j, te_r, tv_r, s_r, d_r: (layer, te_r[i], blk(j, tv_r[i], half), 0))

    return pl.pallas_call(
        functools.partial(_moe_group_kernel, n_f=n_f, n_tok=n_tok, n_pad=n_pad),
        grid_spec=pltpu.PrefetchScalarGridSpec(
            num_scalar_prefetch=4,
            grid=(n_tiles, (n_f + 1) // 2),
            in_specs=[pl.BlockSpec(memory_space=pl.ANY), w13(0), w13(0), w2m(0), w13(1), w13(1), w2m(1)],
            out_specs=pl.BlockSpec(memory_space=pl.ANY),
            scratch_shapes=[pltpu.VMEM((2, MOE_TILE, d), F32), pltpu.VMEM((MOE_TILE, d), BF16),
                            pltpu.VMEM((MOE_TILE, d), F32), pltpu.VMEM((2, MOE_TILE, d), F32),
                            pltpu.SemaphoreType.DMA((2,)), pltpu.SemaphoreType.DMA((2,))],
        ),
        out_shape=jax.ShapeDtypeStruct((2 * n_pad + MOE_TILE, d), F32),
        compiler_params=_cparams(("arbitrary", "arbitrary")),
        name="moe_grouped_ffn",
    )(te, tv, src, dst, h, w1, w3, w2, w1, w3, w2)


def _moe_combine_kernel(y_ref, gm_ref, route_ref, a_ref, b_ref, o_ref):
    route = route_ref[...]
    o_ref[...] = y_ref[...] + gm_ref[...] * (route[:, 2:3] * a_ref[...] + route[:, 3:4] * b_ref[...])


def _moe_combine(y, gmod, route, out01, *, tm, row0, n_pad):
    n, d = y.shape
    tiles = n // tm
    tiles_per_mod = tiles // gmod.shape[0]
    r = gmod.shape[1]
    first, second = row0 // tm, (n_pad + row0) // tm
    blk = pl.BlockSpec((tm, d), lambda i: (i, 0))
    return pl.pallas_call(
        _moe_combine_kernel,
        grid=(tiles,),
        in_specs=[blk, pl.BlockSpec((None, r, d), lambda i: (i // tiles_per_mod, 0, 0)),
                  pl.BlockSpec((tm, LANES), lambda i: (i + first, 0)),
                  pl.BlockSpec((tm, d), lambda i: (i + first, 0)),
                  pl.BlockSpec((tm, d), lambda i: (i + second, 0))],
        out_specs=blk,
        out_shape=jax.ShapeDtypeStruct((n, d), F32),
        compiler_params=_cparams(("arbitrary",)),
        name="moe_combine",
    )(y, gmod, route, out01, out01)


def _agg_matrix(n_rows, n_cmp, n_sel, width):
    a = np.zeros((n_rows, width), np.float32)
    ratio = SEL_BLOCK // CMP_STRIDE
    for s in range(n_sel):
        for tkn in range(s * ratio - 1, s * ratio + ratio):
            if 0 <= tkn < n_cmp and tkn + 1 < n_rows:
                a[tkn + 1, s] = 1.0
    return jnp.asarray(a, BF16)


def _expand_matrix(t_len):
    e = np.zeros((LANES, t_len), np.float32)
    for s in range(t_len // SEL_BLOCK):
        e[s, s * SEL_BLOCK:(s + 1) * SEL_BLOCK] = 1.0
    return jnp.asarray(e, BF16)


def _compress_weights(l, k_gain, cmp_w1, cmp_b1, cmp_w2, cmp_pe):
    w1 = cmp_w1[l]
    half = CMP_STRIDE * HEAD_DIM
    wa = w1[:, :half].reshape(2, CMP_STRIDE, HEAD_DIM, HEAD_DIM)
    wb = w1[:, half:].reshape(2, CMP_STRIDE, HEAD_DIM, HEAD_DIM)
    z = jnp.zeros_like(wa)
    top = jnp.concatenate([wa, z, wb, z], axis=-1)
    bot = jnp.concatenate([z, wa, z, wb], axis=-1)
    wbd = jnp.concatenate([top, bot], axis=-2).astype(BF16)
    w2 = cmp_w2[l]
    z2 = jnp.zeros_like(w2)
    w2bd = jnp.concatenate([jnp.concatenate([w2, z2], -1), jnp.concatenate([z2, w2], -1)], -2).astype(BF16)
    pe = cmp_pe[l].reshape(2, 2 * half, 1)
    b1 = cmp_b1[l].reshape(2, 1, HEAD_DIM)
    kg0 = jnp.tile(k_gain[l, 0], 2).reshape(1, LANES)
    return wbd, w1, pe, b1, w2bd, kg0


def kernel(x_prompt, x_sample, c_prompt, c_sample, cache_cmp, cache_sel, cache_win, state_conv, page_table, w_ada, b_ada, norm1, norm2, w_in, q_gain, k_gain, cmp_w1, cmp_b1, cmp_w2, cmp_pe, conv_w, conv_b, out_gain, w_out, ffn_w1, ffn_w3, ffn_w2, router_w, moe_w1, moe_w3, moe_w2):
    bp, t_p, d = x_prompt.shape
    bs, t_s, _ = x_sample.shape
    depth = w_in.shape[0]
    n_pool, page = cache_cmp.shape[1], cache_cmp.shape[2]
    n_pages = page_table.shape[1]
    past_len = n_pages * page
    n_p, n_s = bp * t_p, bs * t_s
    win_buf = cache_win.shape[2]
    feat = 2 * KV_DIM
    assert page == LANES and t_p % LANES == 0
    n_cmp_rows = past_len // CMP_STRIDE
    assert past_len % SEL_BLOCK == 0 and t_s < CMP_STRIDE
    n_sel_s = past_len // SEL_BLOCK + 1

    cmp_t = jnp.transpose(cache_cmp, (0, 1, 3, 4, 5, 2)).reshape(depth, n_pool, feat, page)
    sel_t = jnp.transpose(cache_sel, (0, 1, 3, 4, 5, 2)).reshape(depth, n_pool, feat, page)
    win_t = jnp.transpose(cache_win, (0, 1, 3, 4, 5, 2)).reshape(depth, bs, feat, win_buf)

    wt_all = jnp.transpose(w_in, (0, 2, 1)).astype(BF16)
    wc_all = (wt_all[:, CONV_OFF:].reshape(depth, 3, CONV_DIM // CONV_TILE, CONV_TILE, d)
              .transpose(0, 2, 1, 3, 4).reshape(depth, 3 * CONV_DIM, d))
    w_out_b = w_out.astype(BF16)
    ffn_b = (ffn_w1.astype(BF16), ffn_w3.astype(BF16), ffn_w2.astype(BF16))
    moe_w = (moe_w1, moe_w3, moe_w2)

    c_all = jnp.concatenate([c_prompt, c_sample, jnp.zeros((16 - bp - bs, d), F32)], axis=0)
    mod = _modulation(c_all, w_ada, b_ada)

    agg_p = _agg_matrix(t_p // CMP_STRIDE, t_p // CMP_STRIDE - 1, t_p // SEL_BLOCK, LANES)
    exp_p = _expand_matrix(t_p)
    sel_w = -(-n_sel_s // LANES) * LANES
    agg_s = _agg_matrix(n_cmp_rows, n_cmp_rows + 3, n_sel_s, sel_w)
    prompt_pages = jnp.broadcast_to(jnp.arange(t_p // LANES, dtype=jnp.int32), (bp, t_p // LANES))
    n_moe = n_p + n_s
    moe_tiles = (TOP_K * n_moe) // MOE_TILE + N_EXPERTS
    assert n_p % MOE_TILE == 0 and n_p % n_s == 0 and n_s <= MOE_TILE

    yp = x_prompt.reshape(n_p, d)
    ys = x_sample.reshape(n_s, d)
    outs = {k: [] for k in ("cmp_p", "sel_p", "win_p", "conv_p", "cmp_s", "sel_s", "win_s", "conv_s")}
    tm_p = 1024
    n_pick = N_SEL - 1
    for l in range(depth):
        cw = _compress_weights(l, k_gain, cmp_w1, cmp_b1, cmp_w2, cmp_pe)
        qg = jnp.tile(q_gain[l] * ATTN_SCALE, Q_TILE // HEAD_DIM).reshape(1, Q_TILE)
        kg = jnp.broadcast_to(k_gain[l][:, None, :, None], (3, N_KV, HEAD_DIM, 1))
        conv_wl, conv_bl, og = conv_w[l], conv_b[l].reshape(1, CONV_DIM), out_gain[l].reshape(1, d)
        mp = [mod[l, :bp, k * d:(k + 1) * d].reshape(bp, 1, d) for k in range(6)]
        ms = [jnp.repeat(mod[l, bp:bp + bs, k * d:(k + 1) * d], t_s, axis=0).reshape(1, n_s, d) for k in range(6)]
        g1 = norm1[l].reshape(1, d)
        g2 = norm2[l].reshape(1, d)

        qn, kvt, gate, u, bg = _inproj(yp, mp[0], mp[1], g1, wt_all, wc_all, qg, kg, tm=tm_p, n_seq=bp, layer=l)
        kc, vc = _compress(
            kvt, prompt_pages,
            lambda k, p: pl.BlockSpec((None, feat, LANES), lambda b, g, pt: (b, 0, pt[b, g * p + k])),
            *cw)
        gate_hm = gate[:, :3 * N_HEADS].reshape(n_p, N_KV, 3 * GQA).transpose(1, 0, 2)
        attn = _prompt_attention(qn, gate_hm, kc, vc, kvt, agg_p, exp_p, tq=256)
        yp = _outproj(attn, u, bg, conv_wl, conv_bl, og, w_out_b, yp, mp[2], tm=tm_p, n_seq=bp, layer=l)
        kv6 = kvt.reshape(bp, 3, 2, N_KV, HEAD_DIM, t_p)
        outs["cmp_p"].append(kv6[:, 0])
        outs["sel_p"].append(kv6[:, 1])
        outs["win_p"].append(kv6[:, 2, ..., t_p - min(WINDOW, t_p):])
        outs["conv_p"].append(u.reshape(bp, t_p, CONV_DIM)[:, t_p - 2:])

        qn_s, kvt_s, gate_s, u_s, bg_s = _inproj(ys, ms[0], ms[1], g1, wt_all, wc_all, qg, kg, tm=n_s, n_seq=1,
                                                 layer=l)
        layer = l
        kc_s, vc_s = _compress(
            cmp_t, page_table,
            lambda k, p: pl.BlockSpec((None, None, feat, LANES),
                                      lambda b, g, pt: (layer, pt[b, g * p + k], 0, 0)),
            *cw)
        q5 = qn_s.astype(F32).reshape(bs, t_s, N_KV, GQA, HEAD_DIM)
        qs_sel = q5.transpose(0, 2, 3, 1, 4).reshape(bs, N_KV, GQA * t_s, HEAD_DIM)
        oc, picked = _sample_select(qs_sel, kc_s, vc_s, agg_s, q0=past_len, n_tok=t_s, n_sel=n_sel_s)
        blocks = picked[:, :, :n_pick].reshape(bs, N_KV, t_s, n_pick)
        page_of = page_table[jnp.arange(bs)[:, None, None, None], blocks // 2].reshape(-1)
        half_of = (blocks % 2).reshape(-1)
        pad8 = ((0, 0), (0, 0), (0, 0), (0, 8 - GQA), (0, 0))
        q8 = jnp.pad(q5.transpose(0, 2, 1, 3, 4), pad8)
        oc8 = jnp.pad(oc.reshape(bs, N_KV, GQA, t_s, HEAD_DIM).transpose(0, 1, 3, 2, 4), pad8)
        gate8 = jnp.pad(gate_s[:, :3 * N_HEADS].reshape(bs, t_s, N_KV, GQA, 3).transpose(0, 2, 1, 3, 4), pad8)
        attn8 = _sample_attend(page_of, half_of, q8, oc8, gate8, kvt_s, sel_t, win_t, n_tok=t_s, n_pick=n_pick,
                               layer=l)
        attn_s = attn8[:, :, :, :GQA].transpose(0, 2, 1, 3, 4).reshape(n_s, ATTN_DIM)
        st = state_conv[l]
        zero = jnp.zeros((bs, t_s - 1, CONV_DIM), F32)
        e1 = jnp.concatenate([st[:, 1:2], zero], axis=1).reshape(n_s, CONV_DIM)
        e2 = jnp.concatenate([st[:, 0:1], st[:, 1:2], zero[:, 1:]], axis=1).reshape(n_s, CONV_DIM)
        ys = _outproj(attn_s, u_s, bg_s, conv_wl, conv_bl, og, w_out_b, ys, ms[2], tm=n_s, n_seq=1, layer=l,
                      ext=(e1, e2), n_tok=t_s)
        rows_s = kvt_s[0].T.reshape(bs, t_s, 3, 2, N_KV, HEAD_DIM)
        outs["cmp_s"].append(rows_s[:, :, 0])
        outs["sel_s"].append(rows_s[:, :, 1])
        new_win_t = kvt_s[0, 4 * KV_DIM:].reshape(feat, bs, t_s).transpose(1, 0, 2)
        outs["win_s"].append(jnp.concatenate([win_t[l], new_win_t], axis=-1)[..., t_s:])
        u_ext = jnp.concatenate([st, u_s.reshape(bs, t_s, CONV_DIM)], axis=1)
        outs["conv_s"].append(u_ext[:, t_s:])

        if l % 2 == 0:
            yp = _dense_ffn(yp, mp[3], mp[4], mp[5], g2, *ffn_b, tm=512, layer=l // 2)
            ys = _dense_ffn(ys, ms[3], ms[4], ms[5], g2, *ffn_b, tm=n_s, layer=l // 2)
        else:
            rw = jnp.zeros((LANES, d), F32).at[:N_EXPERTS].set(jnp.transpose(router_w[l // 2]))
            h_all, route_all = _router(yp, mp[3], mp[4], ys, ms[3], ms[4], g2, rw, tm=MOE_TILE)
            n_pad = h_all.shape[0]
            plan = _moe_plan(route_all[:n_moe], moe_tiles, n_pad)
            out01 = _moe_grouped(h_all, plan, *moe_w, layer=l // 2, n_tok=n_moe)
            yp = _moe_combine(yp, mp[5], route_all, out01, tm=MOE_TILE, row0=0, n_pad=n_pad)
            ys = _moe_combine(ys, ms[5], route_all, out01, tm=n_s, row0=n_p, n_pad=n_pad)

    def rows_major(x):
        return jnp.transpose(x, (0, 1, 5, 2, 3, 4))

    new_win_s = jnp.stack(outs["win_s"]).reshape(depth, bs, 2, N_KV, HEAD_DIM, win_buf)
    return (yp.reshape(bp, t_p, d), ys.reshape(bs, t_s, d),
            rows_major(jnp.stack(outs["cmp_p"])), rows_major(jnp.stack(outs["sel_p"])),
            rows_major(jnp.stack(outs["win_p"])), jnp.stack(outs["conv_p"]),
            jnp.stack(outs["cmp_s"]), jnp.stack(outs["sel_s"]),
            rows_major(new_win_s), jnp.stack(outs["conv_s"]))
```

```python
import functools

import numpy as np
import jax
import jax.numpy as jnp
from jax import lax
from jax.experimental import pallas as pl
from jax.experimental.pallas import tpu as pltpu

F32 = jnp.float32
BF16 = jnp.bfloat16

D_MODEL = 2048
HEAD_DIM = 64
ATTN_DIM = 1024
N_HEADS = 16
N_KV = 4
GQA = 4
KV_DIM = 256
CONV_DIM = 1024
CMP_STRIDE = 16
SEL_BLOCK = 64
N_SEL = 16
WINDOW = 512
N_EXPERTS = 8
TOP_K = 2
KV_OFF = ATTN_DIM
GATE_OFF = KV_OFF + 6 * KV_DIM
CONV_OFF = GATE_OFF + 3 * N_HEADS
IN_COLS = CONV_OFF + 3 * CONV_DIM
EPS = 1e-6
NEG_INF = -1e30
FORCE_BONUS = 1e4
ATTN_SCALE = HEAD_DIM ** -0.5
LANES = 128
VMEM_LIMIT = 56 * 1024 * 1024
MAX_PAGES_PER_STEP = 32
NT_DIMS = (((1,), (1,)), ((), ()))


def _cparams(sem):
    return pltpu.CompilerParams(dimension_semantics=sem, vmem_limit_bytes=VMEM_LIMIT)


def _dot(a, b):
    return jnp.dot(a, b, preferred_element_type=F32)


def _dot_nt(a, b):
    return lax.dot_general(a, b, NT_DIMS, preferred_element_type=F32)


def _silu(x):
    return x * jax.nn.sigmoid(x)


def _split3_dot(x, w_hi):
    x1 = x.astype(BF16)
    r1 = x - x1.astype(F32)
    x2 = r1.astype(BF16)
    x3 = (r1 - x2.astype(F32)).astype(BF16)
    return _dot(x1, w_hi) + _dot(x2, w_hi) + _dot(x3, w_hi)


def _headnorm_lanes(x, gain):
    lane = lax.broadcasted_iota(jnp.int32, (1, LANES), 1)
    lo = lane < HEAD_DIM
    outs = []
    for c in range(x.shape[-1] // LANES):
        xc = x[:, c * LANES:(c + 1) * LANES]
        x2 = xc * xc
        s_lo = jnp.sum(jnp.where(lo, x2, 0.0), axis=-1, keepdims=True)
        s_hi = jnp.sum(jnp.where(lo, 0.0, x2), axis=-1, keepdims=True)
        r = jnp.where(lo, lax.rsqrt(s_lo * (1.0 / HEAD_DIM) + EPS), lax.rsqrt(s_hi * (1.0 / HEAD_DIM) + EPS))
        outs.append(xc * r)
    return jnp.concatenate(outs, axis=-1) * gain


def _masked_softmax(s, mask):
    s = jnp.where(mask, s, NEG_INF)
    m = jnp.max(s, axis=-1, keepdims=True)
    e = jnp.where(mask, jnp.exp(s - m), 0.0)
    return e / jnp.maximum(jnp.sum(e, axis=-1, keepdims=True), 1e-30)


def _rms_modulate(x, g, shift, scale):
    r = lax.rsqrt(jnp.mean(x * x, axis=-1, keepdims=True) + EPS)
    return (x * r * g) * (1.0 + scale) + shift


def _mod_kernel(c_ref, w_ref, b_ref, o_ref):
    s = _silu(c_ref[...])
    o_ref[...] = _dot(s.astype(BF16), w_ref[...].astype(BF16)) + b_ref[...]


def _modulation(c_all, w_ada, b_ada):
    depth, d, cols = w_ada.shape
    rows = c_all.shape[0]
    tn = 1024
    return pl.pallas_call(
        _mod_kernel,
        grid=(depth, cols // tn),
        in_specs=[pl.BlockSpec((rows, d), lambda l, j: (0, 0)),
                  pl.BlockSpec((None, d, tn), lambda l, j: (l, 0, j)),
                  pl.BlockSpec((None, 1, tn), lambda l, j: (l, 0, j))],
        out_specs=pl.BlockSpec((None, rows, tn), lambda l, j: (l, 0, j)),
        out_shape=jax.ShapeDtypeStruct((depth, rows, cols), F32),
        compiler_params=_cparams(("arbitrary", "arbitrary")),
        name="adaln_modulation",
    )(c_all, w_ada, b_ada.reshape(depth, 1, cols))


Q_TILE = 512
KV_TILE = 512
CONV_TILE = 256
N_Q_STEPS = ATTN_DIM // Q_TILE
N_KV_STEPS = 3
N_CONV_STEPS = CONV_DIM // CONV_TILE
N_IN_STEPS = N_Q_STEPS + N_KV_STEPS + N_CONV_STEPS


def _inproj_kernel(y_ref, sh_ref, sc_ref, g1_ref, wq_ref, wkv_ref, wg_ref, wc_ref, qg_ref, kg_ref,
                   qn_ref, kvt_ref, gate_ref, u_ref, bg_ref, h_scr):
    j = pl.program_id(1)

    @pl.when(j == 0)
    def _():
        h = _rms_modulate(y_ref[...], g1_ref[...], sh_ref[...], sc_ref[...])
        h_scr[...] = h.astype(BF16)
        gate_ref[...] = jax.nn.sigmoid(_dot_nt(h_scr[...], wg_ref[...]))

    @pl.when(j < N_Q_STEPS)
    def _():
        p = _dot_nt(h_scr[...], wq_ref[...])
        qn_ref[...] = _headnorm_lanes(p, qg_ref[...]).astype(BF16)

    @pl.when((j >= N_Q_STEPS) & (j < N_Q_STEPS + N_KV_STEPS))
    def _():
        pt = _dot_nt(wkv_ref[...], h_scr[...])
        tok = pt.shape[-1]
        k3 = pt[:KV_DIM].reshape(N_KV, HEAD_DIM, tok)
        ms = jnp.mean(k3 * k3, axis=1, keepdims=True)
        kn = k3 * lax.rsqrt(ms + EPS) * kg_ref[...]
        k3 = jnp.where(j == N_Q_STEPS, k3, kn)
        kvt_ref[:KV_DIM, :] = k3.reshape(KV_DIM, tok)
        kvt_ref[KV_DIM:, :] = pt[KV_DIM:]

    @pl.when(j >= N_Q_STEPS + N_KV_STEPS)
    def _():
        p = _dot_nt(h_scr[...], wc_ref[...])
        u_ref[...] = p[:, 2 * CONV_TILE:] * p[:, :CONV_TILE]
        bg_ref[...] = p[:, CONV_TILE:2 * CONV_TILE]


def _inproj(y, shift, scale, g1, wt_all, wc_all, qg, kg, *, tm, n_seq, layer):
    n, d = y.shape
    t = n // n_seq
    tiles_per_seq = t // tm
    tiles_per_mod = (n // tm) // shift.shape[0]
    r = shift.shape[1]

    def clamp(j, lo, cnt):
        return jnp.clip(j - lo, 0, cnt - 1)

    q_lo, kv_lo, c_lo = 0, N_Q_STEPS, N_Q_STEPS + N_KV_STEPS
    in_specs = [
        pl.BlockSpec((tm, d), lambda i, j: (i, 0)),
        pl.BlockSpec((None, r, d), lambda i, j: (i // tiles_per_mod, 0, 0)),
        pl.BlockSpec((None, r, d), lambda i, j: (i // tiles_per_mod, 0, 0)),
        pl.BlockSpec((1, d), lambda i, j: (0, 0)),
        pl.BlockSpec((None, Q_TILE, d), lambda i, j: (layer, clamp(j, q_lo, N_Q_STEPS), 0)),
        pl.BlockSpec((None, KV_TILE, d), lambda i, j: (layer, KV_OFF // KV_TILE + clamp(j, kv_lo, N_KV_STEPS), 0)),
        pl.BlockSpec((None, LANES, d), lambda i, j: (layer, GATE_OFF // LANES, 0)),
        pl.BlockSpec((None, 3 * CONV_TILE, d), lambda i, j: (layer, clamp(j, c_lo, N_CONV_STEPS), 0)),
        pl.BlockSpec((1, Q_TILE), lambda i, j: (0, 0)),
        pl.BlockSpec((None, N_KV, HEAD_DIM, 1), lambda i, j: (clamp(j, kv_lo, N_KV_STEPS), 0, 0, 0)),
    ]
    out_specs = [
        pl.BlockSpec((tm, Q_TILE), lambda i, j: (i, clamp(j, q_lo, N_Q_STEPS))),
        pl.BlockSpec((None, KV_TILE, tm),
                     lambda i, j: (i // tiles_per_seq, clamp(j, kv_lo, N_KV_STEPS), i % tiles_per_seq)),
        pl.BlockSpec((tm, LANES), lambda i, j: (i, 0)),
        pl.BlockSpec((tm, CONV_TILE), lambda i, j: (i, clamp(j, c_lo, N_CONV_STEPS))),
        pl.BlockSpec((tm, CONV_TILE), lambda i, j: (i, clamp(j, c_lo, N_CONV_STEPS))),
    ]
    out_shape = [
        jax.ShapeDtypeStruct((n, ATTN_DIM), BF16),
        jax.ShapeDtypeStruct((n_seq, 6 * KV_DIM, t), F32),
        jax.ShapeDtypeStruct((n, LANES), F32),
        jax.ShapeDtypeStruct((n, CONV_DIM), F32),
        jax.ShapeDtypeStruct((n, CONV_DIM), F32),
    ]
    return pl.pallas_call(
        _inproj_kernel,
        grid=(n // tm, N_IN_STEPS),
        in_specs=in_specs,
        out_specs=out_specs,
        out_shape=out_shape,
        scratch_shapes=[pltpu.VMEM((tm, d), BF16)],
        compiler_params=_cparams(("arbitrary", "arbitrary")),
        name="input_projection",
    )(y, shift, scale, g1, wt_all, wt_all, wt_all, wc_all, qg, kg)


def _compress_kernel(pt_ref, *refs, p):
    del pt_ref
    pages = refs[:p]
    wbd_ref, w1_ref, pe_ref, b1_ref, w2bd_ref, kg_ref, kc_ref, vc_ref, xs, carry, cconst = refs[p:]
    g = pl.program_id(1)
    cpp = LANES // CMP_STRIDE
    m = p * cpp

    @pl.when(g == 0)
    def _():
        carry[...] = jnp.zeros_like(carry)

    @pl.when((g == 0) & (pl.program_id(0) == 0))
    def _():
        for kv in range(2):
            c64 = jnp.sum(pe_ref[kv] * w1_ref[kv], axis=0, keepdims=True) + b1_ref[kv]
            cconst[kv] = jnp.broadcast_to(jnp.concatenate([c64, c64], axis=-1), (8, LANES))

    for k in range(p):
        for jj in range(4):
            tile = pages[k][jj * LANES:(jj + 1) * LANES, :]
            xs[jj, k * LANES:(k + 1) * LANES, :] = tile.T

    row = lax.broadcasted_iota(jnp.int32, (m, 1), 0)
    for jj in range(4):
        kv = jj // 2
        acc = jnp.zeros((m, 2 * LANES), F32)
        for t in range(CMP_STRIDE):
            acc = acc + _dot(xs[jj, pl.ds(t, m, stride=CMP_STRIDE), :].astype(BF16), wbd_ref[kv, t])
        a_part = acc[:, :LANES]
        b_part = acc[:, LANES:]
        prev = carry[jj]
        a_shift = jnp.where(row == 0, prev[7:8, :], pltpu.roll(a_part, 1, axis=0))
        carry[jj] = a_part[m - 8:, :]
        pre = a_shift + b_part + cconst[kv][0:1, :]
        out = _dot(_silu(pre).astype(BF16), w2bd_ref[kv])
        if kv == 0:
            out = _headnorm_lanes(out, kg_ref[...])
        dst = kc_ref if kv == 0 else vc_ref
        hb = (jj % 2) * 2
        dst[hb] = out[:, :HEAD_DIM].astype(BF16)
        dst[hb + 1] = out[:, HEAD_DIM:].astype(BF16)


def _compress(src, page_ids, page_spec_fn, wbd, w1, pe, b1, w2bd, kg0):
    n_seq, n_pages = page_ids.shape
    p = min(MAX_PAGES_PER_STEP, n_pages)
    assert n_pages % p == 0
    m = p * (LANES // CMP_STRIDE)
    n_groups = n_pages // p
    n_rows = n_pages * (LANES // CMP_STRIDE)
    page_specs = [page_spec_fn(k, p) for k in range(p)]

    def const(shape):
        return pl.BlockSpec(shape, lambda b, g, pt: (0,) * len(shape))

    in_specs = page_specs + [const(wbd.shape), const(w1.shape), const(pe.shape), const(b1.shape),
                             const(w2bd.shape), const(kg0.shape)]
    out_spec = pl.BlockSpec((None, N_KV, m, HEAD_DIM), lambda b, g, pt: (b, 0, g, 0))
    return pl.pallas_call(
        functools.partial(_compress_kernel, p=p),
        grid_spec=pltpu.PrefetchScalarGridSpec(
            num_scalar_prefetch=1,
            grid=(n_seq, n_groups),
            in_specs=in_specs,
            out_specs=[out_spec, out_spec],
            scratch_shapes=[pltpu.VMEM((4, p * LANES, LANES), F32), pltpu.VMEM((4, 8, LANES), F32),
                            pltpu.VMEM((2, 8, LANES), F32)],
        ),
        out_shape=[jax.ShapeDtypeStruct((n_seq, N_KV, n_rows, HEAD_DIM), BF16)] * 2,
        compiler_params=_cparams(("arbitrary", "arbitrary")),
        name="compress_mlp",
    )(page_ids, *([src] * p), wbd, w1, pe, b1, w2bd, kg0)


SEL_CHUNK = 512


def _attn_kernel(q_ref, gate_ref, kc_ref, vc_ref, kst_ref, vst_ref, kwt_ref, vwt_ref, agg_ref, exp_ref,
                 o_ref, bias_scr, *, tq, t_len):
    i = pl.program_id(2)
    q0 = i * tq
    m_rows = GQA * tq
    n_sel = t_len // SEL_BLOCK
    win_keys = WINDOW + tq
    q = q_ref[...]
    qs = jnp.concatenate([q[:, g * HEAD_DIM:(g + 1) * HEAD_DIM] for g in range(GQA)], axis=0)
    row = lax.broadcasted_iota(jnp.int32, (m_rows, 1), 0)
    qpos = q0 + (row % tq)
    qp = q0 + lax.broadcasted_iota(jnp.int32, (tq, 1), 0)

    s_c = _dot_nt(qs, kc_ref[...])
    r_i = lax.broadcasted_iota(jnp.int32, (1, s_c.shape[-1]), 1)
    mask_c = (r_i >= 1) & (CMP_STRIDE * r_i + (CMP_STRIDE - 1) <= qpos)
    p_c = _masked_softmax(s_c, mask_c)
    o_c = _dot(p_c.astype(BF16), vc_ref[...])
    imp = p_c[0:tq] + p_c[tq:2 * tq] + p_c[2 * tq:3 * tq] + p_c[3 * tq:4 * tq]
    imp_s = _split3_dot(imp, agg_ref[...])

    sidx = lax.broadcasted_iota(jnp.int32, (1, LANES), 1)
    cur = qp // SEL_BLOCK
    forced = (sidx == 0) | (sidx == cur) | (sidx == cur - 1)
    causal = sidx * SEL_BLOCK <= qp
    score = jnp.where(causal, imp_s + jnp.where(forced, FORCE_BONUS, 0.0), NEG_INF)
    sv = score.T[:n_sel]
    srow = lax.broadcasted_iota(jnp.int32, (n_sel, 1), 0)
    cnt = jnp.zeros((n_sel, tq), jnp.int32)
    for t in range(n_sel):
        other = sv[t:t + 1, :]
        beats = (other > sv) | ((other == sv) & (t < srow))
        cnt = cnt + beats.astype(jnp.int32)
    sel_t = jnp.where((cnt < min(N_SEL, n_sel)) & (sv > NEG_INF / 2), 1.0, 0.0)
    sel = jnp.concatenate([sel_t, jnp.zeros((LANES - n_sel, tq), F32)], axis=0).T
    selm = _dot(sel.astype(BF16), exp_ref[...])
    kpos_all = lax.broadcasted_iota(jnp.int32, (1, t_len), 1)
    bias_scr[...] = jnp.where((selm > 0.5) & (kpos_all <= qp), 0.0, NEG_INF)

    n_chunks = (q0 + tq + SEL_CHUNK - 1) // SEL_CHUNK

    def body(c, carry):
        m_old, l_old, acc = carry
        off = pl.multiple_of(c * SEL_CHUNK, SEL_CHUNK)
        kt = kst_ref[:, pl.ds(off, SEL_CHUNK)].astype(BF16)
        s = _dot(qs, kt).reshape(GQA, tq, SEL_CHUNK) + bias_scr[:, pl.ds(off, SEL_CHUNK)][None]
        s = s.reshape(m_rows, SEL_CHUNK)
        m_new = jnp.maximum(m_old, jnp.max(s, axis=-1, keepdims=True))
        alpha = jnp.exp(m_old - m_new)
        pr = jnp.exp(s - m_new)
        l_new = alpha * l_old + jnp.sum(pr, axis=-1, keepdims=True)
        vt = vst_ref[:, pl.ds(off, SEL_CHUNK)].astype(BF16)
        acc = alpha * acc + _dot_nt(pr.astype(BF16), vt)
        return m_new, l_new, acc

    init = (jnp.full((m_rows, 1), NEG_INF, F32), jnp.zeros((m_rows, 1), F32), jnp.zeros((m_rows, HEAD_DIM), F32))
    _, l_s, acc_s = lax.fori_loop(0, n_chunks, body, init)
    o_s = acc_s / jnp.maximum(l_s, 1e-30)

    start = pl.multiple_of(jnp.clip(q0 - WINDOW, 0, t_len - win_keys), LANES)
    dist = qp - (start + lax.broadcasted_iota(jnp.int32, (1, win_keys), 1))
    bias_w = jnp.where((dist >= 0) & (dist < WINDOW), 0.0, NEG_INF)
    s_w = _dot(qs, kwt_ref[:, pl.ds(start, win_keys)].astype(BF16)).reshape(GQA, tq, win_keys) + bias_w[None]
    s_w = s_w.reshape(m_rows, win_keys)
    e_w = jnp.exp(s_w - jnp.max(s_w, axis=-1, keepdims=True))
    p_w = e_w / jnp.maximum(jnp.sum(e_w, axis=-1, keepdims=True), 1e-30)
    o_w = _dot_nt(p_w.astype(BF16), vwt_ref[:, pl.ds(start, win_keys)].astype(BF16))

    gate = gate_ref[...]

    def gcol(jb):
        return jnp.concatenate([jnp.broadcast_to(gate[:, g * 3 + jb:g * 3 + jb + 1], (tq, HEAD_DIM))
                                for g in range(GQA)], axis=0)

    o = gcol(0) * o_c + gcol(1) * o_s + gcol(2) * o_w
    o_ref[...] = jnp.concatenate([o[g * tq:(g + 1) * tq] for g in range(GQA)], axis=-1)


def _prompt_attention(qn, gate_hm, kc, vc, kvt, agg, expand, *, tq):
    n = qn.shape[0]
    n_seq, _, t_len = kvt.shape
    tiles = t_len // tq
    kern = functools.partial(_attn_kernel, tq=tq, t_len=t_len)

    def kv_spec(block_row):
        return pl.BlockSpec((None, HEAD_DIM, t_len), lambda b, h, i: (b, block_row + h, 0))

    sel_k, sel_v = (2 * KV_DIM) // HEAD_DIM, (3 * KV_DIM) // HEAD_DIM
    win_k, win_v = (4 * KV_DIM) // HEAD_DIM, (5 * KV_DIM) // HEAD_DIM
    return pl.pallas_call(
        kern,
        grid=(n_seq, N_KV, tiles),
        in_specs=[
            pl.BlockSpec((tq, GQA * HEAD_DIM), lambda b, h, i: (b * tiles + i, h)),
            pl.BlockSpec((None, tq, 3 * GQA), lambda b, h, i: (h, b * tiles + i, 0)),
            pl.BlockSpec((None, None, kc.shape[2], HEAD_DIM), lambda b, h, i: (b, h, 0, 0)),
            pl.BlockSpec((None, None, vc.shape[2], HEAD_DIM), lambda b, h, i: (b, h, 0, 0)),
            kv_spec(sel_k), kv_spec(sel_v), kv_spec(win_k), kv_spec(win_v),
            pl.BlockSpec(agg.shape, lambda b, h, i: (0, 0)),
            pl.BlockSpec(expand.shape, lambda b, h, i: (0, 0)),
        ],
        out_specs=pl.BlockSpec((tq, GQA * HEAD_DIM), lambda b, h, i: (b * tiles + i, h)),
        out_shape=jax.ShapeDtypeStruct((n, ATTN_DIM), F32),
        scratch_shapes=[pltpu.VMEM((tq, t_len), F32)],
        compiler_params=_cparams(("arbitrary", "arbitrary", "arbitrary")),
        name="prompt_attention",
    )(qn, gate_hm, kc, vc, kvt, kvt, kvt, kvt, agg, expand)


def _sample_select_kernel(qs_ref, kc_ref, vc_ref, agg_ref, oc_ref, idx_ref, *, q0, n_tok, n_sel, n_pick):
    rows = GQA * n_tok
    row = lax.broadcasted_iota(jnp.int32, (rows, 1), 0)
    qpos = q0 + (row % n_tok)
    imps = []
    for h in range(N_KV):
        s_c = _dot_nt(qs_ref[h].astype(BF16), kc_ref[h])
        r_i = lax.broadcasted_iota(jnp.int32, (1, s_c.shape[-1]), 1)
        mask_c = (r_i >= 1) & (CMP_STRIDE * r_i + (CMP_STRIDE - 1) <= qpos)
        p_c = _masked_softmax(s_c, mask_c)
        oc_ref[h] = _dot(p_c.astype(BF16), vc_ref[h])
        imp = p_c[0:n_tok]
        for g in range(1, GQA):
            imp = imp + p_c[g * n_tok:(g + 1) * n_tok]
        imps.append(imp)
    imp_all = jnp.concatenate(imps, axis=0)
    imp_s = _split3_dot(imp_all, agg_ref[...])
    width = imp_s.shape[-1]
    sidx = lax.broadcasted_iota(jnp.int32, (1, width), 1)
    r2 = lax.broadcasted_iota(jnp.int32, (N_KV * n_tok, 1), 0)
    qp = q0 + (r2 % n_tok)
    cur = qp // SEL_BLOCK
    forced = (sidx == 0) | (sidx == cur) | (sidx == cur - 1)
    causal = sidx * SEL_BLOCK <= qp
    score = jnp.where(causal, imp_s + jnp.where(forced, FORCE_BONUS, 0.0), NEG_INF)
    lowest = -3.0e38
    score = jnp.where((sidx >= n_sel - 1), lowest, score)
    out_lane = lax.broadcasted_iota(jnp.int32, (1, LANES), 1)
    picked = jnp.zeros((N_KV * n_tok, LANES), jnp.int32)
    for it in range(n_pick):
        mx = jnp.max(score, axis=-1, keepdims=True)
        first = jnp.min(jnp.where(score == mx, sidx, width), axis=-1, keepdims=True)
        picked = jnp.where(out_lane == it, first, picked)
        score = jnp.where(sidx == first, lowest, score)
    idx_ref[...] = picked


def _sample_select(qs, kc, vc, agg, *, q0, n_tok, n_sel):
    n_seq = qs.shape[0]
    n_pick = N_SEL - 1
    kern = functools.partial(_sample_select_kernel, q0=q0, n_tok=n_tok, n_sel=n_sel, n_pick=n_pick)
    return pl.pallas_call(
        kern,
        grid=(n_seq,),
        in_specs=[
            pl.BlockSpec((None, N_KV, GQA * n_tok, HEAD_DIM), lambda b: (b, 0, 0, 0)),
            pl.BlockSpec((None, N_KV, kc.shape[2], HEAD_DIM), lambda b: (b, 0, 0, 0)),
            pl.BlockSpec((None, N_KV, vc.shape[2], HEAD_DIM), lambda b: (b, 0, 0, 0)),
            pl.BlockSpec(agg.shape, lambda b: (0, 0)),
        ],
        out_specs=[
            pl.BlockSpec((None, N_KV, GQA * n_tok, HEAD_DIM), lambda b: (b, 0, 0, 0)),
            pl.BlockSpec((None, N_KV * n_tok, LANES), lambda b: (b, 0, 0)),
        ],
        out_shape=[
            jax.ShapeDtypeStruct((n_seq, N_KV, GQA * n_tok, HEAD_DIM), F32),
            jax.ShapeDtypeStruct((n_seq, N_KV * n_tok, LANES), jnp.int32),
        ],
        compiler_params=_cparams(("arbitrary",)),
        name="sample_select",
    )(qs, kc, vc, agg)


def _sample_attend_kernel(pg_ref, half_ref, q_ref, oc_ref, gate_ref, knew_ref, vnew_ref, kwp_ref, vwp_ref,
                          kwn_ref, vwn_ref, sel_hbm, o_ref, kbuf, vbuf, sem, *, n_tok, n_pick, layer):
    n_pages = n_tok * n_pick
    b = pl.program_id(0)
    h = pl.program_id(1)
    step = b * N_KV + h
    n_steps = pl.num_programs(0) * N_KV
    slot = step % 2

    def gather_start(stp, s):
        row0 = pl.multiple_of((stp % N_KV) * HEAD_DIM, HEAD_DIM)
        for tk in range(n_pages):
            page = pg_ref[stp * n_pages + tk]
            pltpu.make_async_copy(sel_hbm.at[layer, page, pl.ds(row0, HEAD_DIM)], kbuf.at[s, tk], sem.at[s]).start()
            pltpu.make_async_copy(sel_hbm.at[layer, page, pl.ds(KV_DIM + row0, HEAD_DIM)], vbuf.at[s, tk],
                                  sem.at[s]).start()

    @pl.when(step == 0)
    def _():
        gather_start(0, 0)

    @pl.when(step + 1 < n_steps)
    def _():
        gather_start(step + 1, 1 - slot)

    for buf in (kbuf, vbuf):
        pltpu.make_async_copy(sel_hbm.at[layer, pl.ds(0, n_pages), pl.ds(0, HEAD_DIM)], buf.at[slot], sem.at[slot]).wait()
    lane = lax.broadcasted_iota(jnp.int32, (1, LANES), 1)
    n_new = knew_ref.shape[-1]
    new_lane = lax.broadcasted_iota(jnp.int32, (1, n_new), 1)
    n_win = kwp_ref.shape[-1]
    wl = lax.broadcasted_iota(jnp.int32, (1, n_win), 1)
    k_new, v_new = knew_ref[...].astype(BF16), vnew_ref[...].astype(BF16)
    kw_past, vw_past = kwp_ref[...].astype(BF16), vwp_ref[...].astype(BF16)
    kw_new, vw_new = kwn_ref[...].astype(BF16), vwn_ref[...].astype(BF16)

    for t in range(n_tok):
        q = q_ref[t].astype(BF16)
        new_mask = (new_lane // n_tok == b) & (new_lane % n_tok <= t)

        flat = ((b * N_KV + h) * n_tok + t) * n_pick
        scores, masks = [], []
        for k in range(n_pick):
            half = half_ref[flat + k]
            scores.append(_dot(q, kbuf[slot, t * n_pick + k].astype(BF16)))
            masks.append(lane // SEL_BLOCK == half)
        s_new = _dot(q, k_new)
        mx = jnp.max(jnp.where(new_mask, s_new, NEG_INF), axis=-1, keepdims=True)
        for s, mk in zip(scores, masks):
            mx = jnp.maximum(mx, jnp.max(jnp.where(mk, s, NEG_INF), axis=-1, keepdims=True))
        e_new = jnp.exp(jnp.where(new_mask, s_new, NEG_INF) - mx) * new_mask.astype(F32)
        den = jnp.sum(e_new, axis=-1, keepdims=True)
        acc = _dot_nt(e_new.astype(BF16), v_new)
        for k in range(n_pick):
            e = jnp.exp(jnp.where(masks[k], scores[k], NEG_INF) - mx) * masks[k].astype(F32)
            den = den + jnp.sum(e, axis=-1, keepdims=True)
            acc = acc + _dot_nt(e.astype(BF16), vbuf[slot, t * n_pick + k].astype(BF16))
        o_s = acc / jnp.maximum(den, 1e-30)

        past_mask = (n_win - wl + t) < WINDOW
        s_p = jnp.where(past_mask, _dot(q, kw_past), NEG_INF)
        s_n = jnp.where(new_mask, _dot(q, kw_new), NEG_INF)
        mw = jnp.maximum(jnp.max(s_p, axis=-1, keepdims=True), jnp.max(s_n, axis=-1, keepdims=True))
        e_p = jnp.exp(s_p - mw) * past_mask.astype(F32)
        e_n = jnp.exp(s_n - mw) * new_mask.astype(F32)
        den_w = jnp.sum(e_p, axis=-1, keepdims=True) + jnp.sum(e_n, axis=-1, keepdims=True)
        acc_w = _dot_nt(e_p.astype(BF16), vw_past) + _dot_nt(e_n.astype(BF16), vw_new)
        o_w = acc_w / jnp.maximum(den_w, 1e-30)

        gate = gate_ref[t]
        o_ref[t] = gate[:, 0:1] * oc_ref[t] + gate[:, 1:2] * o_s + gate[:, 2:3] * o_w


def _sample_attend(page_of, half_of, q8, oc8, gate8, kvt_new, cache_sel_t, cache_win_t, *, n_tok, n_pick, layer):
    n_seq = q8.shape[0]
    kern = functools.partial(_sample_attend_kernel, n_tok=n_tok, n_pick=n_pick, layer=layer)
    n_new = kvt_new.shape[-1]
    n_win = cache_win_t.shape[-1]
    n_pages = n_tok * n_pick
    assert cache_sel_t.shape[1] >= n_pages

    def qspec(last):
        return pl.BlockSpec((None, None, n_tok, 8, last), lambda b, h, pg, hf: (b, h, 0, 0, 0))

    def new_spec(block_row):
        return pl.BlockSpec((None, HEAD_DIM, n_new), lambda b, h, pg, hf: (0, block_row + h, 0))

    def win_spec(block_row):
        return pl.BlockSpec((None, None, HEAD_DIM, n_win), lambda b, h, pg, hf: (layer, b, block_row + h, 0))

    sel_k, sel_v = (2 * KV_DIM) // HEAD_DIM, (3 * KV_DIM) // HEAD_DIM
    win_k, win_v = (4 * KV_DIM) // HEAD_DIM, (5 * KV_DIM) // HEAD_DIM
    in_specs = [qspec(HEAD_DIM), qspec(HEAD_DIM), qspec(3), new_spec(sel_k), new_spec(sel_v),
                win_spec(0), win_spec(N_KV), new_spec(win_k), new_spec(win_v), pl.BlockSpec(memory_space=pl.ANY)]
    return pl.pallas_call(
        kern,
        grid_spec=pltpu.PrefetchScalarGridSpec(
            num_scalar_prefetch=2,
            grid=(n_seq, N_KV),
            in_specs=in_specs,
            out_specs=qspec(HEAD_DIM),
            scratch_shapes=[pltpu.VMEM((2, n_pages, HEAD_DIM, LANES), F32), pltpu.VMEM((2, n_pages, HEAD_DIM, LANES), F32),
                            pltpu.SemaphoreType.DMA((2,))],
        ),
        out_shape=jax.ShapeDtypeStruct(q8.shape, F32),
        compiler_params=_cparams(("arbitrary", "arbitrary")),
        name="sample_attend",
    )(page_of, half_of, q8, oc8, gate8, kvt_new, kvt_new, cache_win_t, cache_win_t, kvt_new, kvt_new, cache_sel_t)


def _mix_prologue(attn, u, u1, u2, bg, cw_ref, cb_ref, og_ref, o_scr):
    og = og_ref[...]
    an = attn * lax.rsqrt(jnp.mean(attn * attn, axis=-1, keepdims=True) + EPS) * og[:, :ATTN_DIM]
    yv = cb_ref[...] + cw_ref[0:1, :] * u2
    yv = yv + cw_ref[1:2, :] * u1
    yv = yv + cw_ref[2:3, :] * u
    cv = bg * yv
    cn = cv * lax.rsqrt(jnp.mean(cv * cv, axis=-1, keepdims=True) + EPS) * og[:, ATTN_DIM:]
    o_scr[:, :ATTN_DIM] = an.astype(BF16)
    o_scr[:, ATTN_DIM:] = cn.astype(BF16)


def _outproj_seq_kernel(attn_ref, u_ref, halo_ref, bg_ref, cw_ref, cb_ref, og_ref, w_ref, y_ref, gm_ref,
                        o_ref, o_scr, *, tiles_per_seq):
    i = pl.program_id(0)

    @pl.when(pl.program_id(1) == 0)
    def _():
        u = u_ref[...]
        tm = u.shape[0]
        row = lax.broadcasted_iota(jnp.int32, (tm, 1), 0)
        halo = jnp.where(i % tiles_per_seq == 0, 0.0, halo_ref[...])
        p1 = halo[7:8, :]
        p2 = halo[6:7, :]
        u1 = jnp.where(row == 0, p1, pltpu.roll(u, 1, axis=0))
        u2 = jnp.where(row == 0, p2, jnp.where(row == 1, p1, pltpu.roll(u, 2, axis=0)))
        _mix_prologue(attn_ref[...], u, u1, u2, bg_ref[...], cw_ref, cb_ref, og_ref, o_scr)

    o_ref[...] = y_ref[...] + gm_ref[...] * _dot(o_scr[...], w_ref[...])


def _outproj_state_kernel(attn_ref, u_ref, e1_ref, e2_ref, bg_ref, cw_ref, cb_ref, og_ref, w_ref, y_ref, gm_ref,
                          o_ref, o_scr, *, n_tok):
    @pl.when(pl.program_id(1) == 0)
    def _():
        u = u_ref[...]
        tm = u.shape[0]
        k = lax.broadcasted_iota(jnp.int32, (tm, 1), 0) % n_tok
        u1 = jnp.where(k >= 1, pltpu.roll(u, 1, axis=0), e1_ref[...])
        u2 = jnp.where(k >= 2, pltpu.roll(u, 2, axis=0), e2_ref[...])
        _mix_prologue(attn_ref[...], u, u1, u2, bg_ref[...], cw_ref, cb_ref, og_ref, o_scr)

    o_ref[...] = y_ref[...] + gm_ref[...] * _dot(o_scr[...], w_ref[...])


def _outproj(attn, u, bg, cw, cb, og, w, y, gmod, *, tm, n_seq, layer, ext=None, n_tok=None):
    n, d = y.shape
    tn = 512
    tiles_per_seq = (n // n_seq) // tm
    tiles_per_mod = (n // tm) // gmod.shape[0]
    r = gmod.shape[1]
    full = lambda i, j: (i, 0)
    c0 = lambda i, j: (0, 0)
    common = [pl.BlockSpec((tm, CONV_DIM), full), pl.BlockSpec((3, CONV_DIM), c0), pl.BlockSpec((1, CONV_DIM), c0),
              pl.BlockSpec((1, d), c0), pl.BlockSpec((None, d, tn), lambda i, j: (layer, 0, j)),
              pl.BlockSpec((tm, tn), lambda i, j: (i, j)),
              pl.BlockSpec((None, r, tn), lambda i, j: (i // tiles_per_mod, 0, j))]
    head = [pl.BlockSpec((tm, ATTN_DIM), full), pl.BlockSpec((tm, CONV_DIM), full)]
    if ext is None:
        kern = functools.partial(_outproj_seq_kernel, tiles_per_seq=tiles_per_seq)
        extra = [pl.BlockSpec((8, CONV_DIM), lambda i, j: (jnp.maximum(i * (tm // 8) - 1, 0), 0))]
        args = (attn, u, u, bg, cw, cb, og, w, y, gmod)
    else:
        kern = functools.partial(_outproj_state_kernel, n_tok=n_tok)
        extra = [pl.BlockSpec((tm, CONV_DIM), full), pl.BlockSpec((tm, CONV_DIM), full)]
        args = (attn, u, ext[0], ext[1], bg, cw, cb, og, w, y, gmod)
    return pl.pallas_call(
        kern,
        grid=(n // tm, d // tn),
        in_specs=head + extra + common,
        out_specs=pl.BlockSpec((tm, tn), lambda i, j: (i, j)),
        out_shape=jax.ShapeDtypeStruct((n, d), F32),
        scratch_shapes=[pltpu.VMEM((tm, d), BF16)],
        compiler_params=_cparams(("arbitrary", "arbitrary")),
        name="output_projection",
    )(*args)


def _ffn_kernel(y_ref, sh_ref, sc_ref, gm_ref, g2_ref, w1_ref, w3_ref, w2_ref, o_ref, h_scr, acc):
    j = pl.program_id(1)

    @pl.when(j == 0)
    def _():
        h_scr[...] = _rms_modulate(y_ref[...], g2_ref[...], sh_ref[...], sc_ref[...]).astype(BF16)
        acc[...] = jnp.zeros_like(acc)

    h = h_scr[...]
    hid = _silu(_dot(h, w1_ref[...])) * _dot(h, w3_ref[...])
    acc[...] += _dot(hid.astype(BF16), w2_ref[...])

    @pl.when(j == pl.num_programs(1) - 1)
    def _():
        o_ref[...] = y_ref[...] + gm_ref[...] * acc[...]


def _dense_ffn(y, shift, scale, gmod, g2, w1, w3, w2, *, tm, layer):
    n, d = y.shape
    f = w1.shape[-1]
    tf = 512
    tiles_per_mod = (n // tm) // shift.shape[0]
    r = shift.shape[1]
    mod = pl.BlockSpec((None, r, d), lambda i, j: (i // tiles_per_mod, 0, 0))
    return pl.pallas_call(
        _ffn_kernel,
        grid=(n // tm, f // tf),
        in_specs=[pl.BlockSpec((tm, d), lambda i, j: (i, 0)), mod, mod, mod,
                  pl.BlockSpec((1, d), lambda i, j: (0, 0)),
                  pl.BlockSpec((None, d, tf), lambda i, j: (layer, 0, j)),
                  pl.BlockSpec((None, d, tf), lambda i, j: (layer, 0, j)),
                  pl.BlockSpec((None, tf, d), lambda i, j: (layer, j, 0))],
        out_specs=pl.BlockSpec((tm, d), lambda i, j: (i, 0)),
        out_shape=jax.ShapeDtypeStruct((n, d), F32),
        scratch_shapes=[pltpu.VMEM((tm, d), BF16), pltpu.VMEM((tm, d), F32)],
        compiler_params=_cparams(("arbitrary", "arbitrary")),
        name="dense_ffn",
    )(y, shift, scale, gmod, g2, w1, w3, w2)


def _top2_route(h, h_bf, rw):
    lane = lax.broadcasted_iota(jnp.int32, (1, LANES), 1)
    rw_hi = rw.astype(BF16)
    rw_lo = (rw - rw_hi.astype(F32)).astype(BF16)
    h_lo = (h - h_bf.astype(F32)).astype(BF16)
    logits = _dot_nt(h_bf, rw_hi) + _dot_nt(h_lo, rw_hi) + _dot_nt(h_bf, rw_lo)
    lowest = -3.0e38
    logits = jnp.where(lane < N_EXPERTS, logits, lowest)
    m1 = jnp.max(logits, axis=-1, keepdims=True)
    i1 = jnp.min(jnp.where(logits == m1, lane, LANES), axis=-1, keepdims=True)
    rest = jnp.where(lane == i1, lowest, logits)
    m2 = jnp.max(rest, axis=-1, keepdims=True)
    i2 = jnp.min(jnp.where(rest == m2, lane, LANES), axis=-1, keepdims=True)
    e2 = jnp.exp(m2 - m1)
    den = 1.0 + e2
    return i1, i2, 1.0 / den, e2 / den


MOE_TILE = 512
MOE_FT = 256


def _route_rows(y_ref, sh_ref, sc_ref, g2_ref, rw_ref, h_ref, route_ref):
    h = _rms_modulate(y_ref[...], g2_ref[...], sh_ref[...], sc_ref[...])
    h_ref[...] = h
    i1, i2, w1, w2 = _top2_route(h, h.astype(BF16), rw_ref[...])
    lane = lax.broadcasted_iota(jnp.int32, (1, LANES), 1)
    route_ref[...] = (jnp.where(lane == 0, i1.astype(F32), 0.0) + jnp.where(lane == 1, i2.astype(F32), 0.0)
                      + jnp.where(lane == 2, w1, 0.0) + jnp.where(lane == 3, w2, 0.0))


def _router_kernel(y_ref, sh_ref, sc_ref, g2_ref, rw_ref, h_ref, route_ref):
    real = pl.program_id(0) < pl.num_programs(0) - 1

    @pl.when(real)
    def _():
        _route_rows(y_ref, sh_ref, sc_ref, g2_ref, rw_ref, h_ref, route_ref)

    @pl.when(jnp.logical_not(real))
    def _():
        h_ref[...] = jnp.zeros_like(h_ref)
        route_ref[...] = jnp.zeros_like(route_ref)


def _router(y, shift, scale, g2, rw, *, tm):
    n, d = y.shape
    tiles = n // tm
    tiles_per_mod = tiles // shift.shape[0]
    r = shift.shape[1]
    last = tiles - 1
    mod = pl.BlockSpec((None, r, d), lambda i: (jnp.minimum(i, last) // tiles_per_mod, 0, 0))
    return pl.pallas_call(
        _router_kernel,
        grid=(tiles + 1,),
        in_specs=[pl.BlockSpec((tm, d), lambda i: (jnp.minimum(i, last), 0)), mod, mod,
                  pl.BlockSpec((1, d), lambda i: (0, 0)), pl.BlockSpec((LANES, d), lambda i: (0, 0))],
        out_specs=[pl.BlockSpec((tm, d), lambda i: (i, 0)), pl.BlockSpec((tm, LANES), lambda i: (i, 0))],
        out_shape=[jax.ShapeDtypeStruct((n + tm, d), F32), jax.ShapeDtypeStruct((n + tm, LANES), F32)],
        compiler_params=_cparams(("arbitrary",)),
        name="moe_router",
    )(y, shift, scale, g2, rw)


def _router_into_kernel(y_ref, sh_ref, sc_ref, g2_ref, rw_ref, h_in, route_in, h_ref, route_ref):
    del h_in, route_in
    _route_rows(y_ref, sh_ref, sc_ref, g2_ref, rw_ref, h_ref, route_ref)


def _router_into(y, shift, scale, g2, rw, h_all, route_all, *, row0):
    n, d = y.shape
    r = shift.shape[1]
    mod = pl.BlockSpec((None, r, d), lambda i: (0, 0, 0))
    blk = row0 // n
    return pl.pallas_call(
        _router_into_kernel,
        grid=(1,),
        in_specs=[pl.BlockSpec((n, d), lambda i: (0, 0)), mod, mod,
                  pl.BlockSpec((1, d), lambda i: (0, 0)), pl.BlockSpec((LANES, d), lambda i: (0, 0)),
                  pl.BlockSpec(memory_space=pl.ANY), pl.BlockSpec(memory_space=pl.ANY)],
        out_specs=[pl.BlockSpec((n, d), lambda i: (blk, 0)), pl.BlockSpec((n, LANES), lambda i: (blk, 0))],
        out_shape=[jax.ShapeDtypeStruct(h_all.shape, F32), jax.ShapeDtypeStruct(route_all.shape, F32)],
        input_output_aliases={5: 0, 6: 1},
        compiler_params=_cparams(("arbitrary",)),
        name="moe_router_into",
    )(y, shift, scale, g2, rw, h_all, route_all)


def _moe_plan(route, n_tiles, n_pad):
    n = route.shape[0]
    flat_e = route[:, :2].astype(jnp.int32).reshape(-1)
    onehot = (flat_e[:, None] == jnp.arange(N_EXPERTS, dtype=jnp.int32)[None, :]).astype(jnp.int32)
    csum = jnp.cumsum(onehot, axis=0)
    pos = jnp.sum((csum - 1) * onehot, axis=1)
    sizes = csum[-1]
    padded = ((sizes + MOE_TILE - 1) // MOE_TILE) * MOE_TILE
    ends = jnp.cumsum(padded)
    dest = (ends - padded)[flat_e] + pos
    rows = n_tiles * MOE_TILE
    a_of_row = jnp.full((rows,), -1, jnp.int32).at[dest].set(jnp.arange(2 * n, dtype=jnp.int32),
                                                             unique_indices=True)
    real = a_of_row >= 0
    a0 = jnp.maximum(a_of_row, 0)
    src = a0 // 2
    spare = 2 * n_pad + jnp.arange(rows, dtype=jnp.int32) % MOE_TILE
    dst = jnp.where(real, (a0 % 2) * n_pad + a0 // 2, spare)
    tile_start = jnp.arange(n_tiles, dtype=jnp.int32) * MOE_TILE
    te = jnp.sum((tile_start[:, None] >= ends[None, :]).astype(jnp.int32), axis=1)
    tv = (te < N_EXPERTS).astype(jnp.int32)
    return jnp.minimum(te, N_EXPERTS - 1), tv, src, dst


def _row_copy(src_ref, src_row, dst_ref, dst_row, sem):
    return pltpu.make_async_copy(src_ref.at[pl.ds(src_row, 1)], dst_ref.at[pl.ds(dst_row, 1)], sem)


ROW_LOOP_UNROLL = 8


def _moe_group_kernel(te_ref, tv_ref, src_ref, dst_ref, h_hbm, w1a_ref, w3a_ref, w2a_ref, w1b_ref, w3b_ref, w2b_ref,
                      out_hbm, xbuf, hb, acc, obuf, gsem, ssem, *, n_f, n_tok, n_pad):
    del te_ref
    i = pl.program_id(0)
    j = pl.program_id(1)
    n_t = pl.num_programs(0)
    n_j = pl.num_programs(1)
    valid = tv_ref[i] == 1
    slot = i % 2

    def row_loop(body):
        def step(r, c):
            body(r)
            return c
        lax.fori_loop(0, MOE_TILE, step, 0, unroll=ROW_LOOP_UNROLL)

    def gather_start(tile, s):
        row_loop(lambda r: _row_copy(h_hbm, src_ref[tile * MOE_TILE + r], xbuf.at[s], r, gsem.at[s]).start())

    def gather_wait(s):
        pltpu.make_async_copy(h_hbm.at[pl.ds(0, MOE_TILE)], xbuf.at[s], gsem.at[s]).wait()

    def scatter_start(tile, s):
        row_loop(lambda r: _row_copy(obuf.at[s], r, out_hbm, dst_ref[tile * MOE_TILE + r], ssem.at[s])
                 .start(priority=1))

    def scatter_wait(s):
        pltpu.make_async_copy(obuf.at[s], out_hbm.at[pl.ds(0, MOE_TILE)], ssem.at[s]).wait()

    @pl.when(valid & (j == 0))
    def _():
        @pl.when(i == 0)
        def _():
            gather_start(0, 0)
            obuf[1] = jnp.zeros(obuf.shape[1:], F32)
            fills = [pltpu.make_async_copy(obuf.at[1], out_hbm.at[pl.ds(2 * n_pad, MOE_TILE)], ssem.at[1])]
            if n_pad > n_tok:
                fills += [pltpu.make_async_copy(obuf.at[1, pl.ds(0, n_pad - n_tok)],
                                                out_hbm.at[pl.ds(s0 * n_pad + n_tok, n_pad - n_tok)], ssem.at[1])
                          for s0 in range(2)]
            for cp in fills:
                cp.start()
            for cp in fills:
                cp.wait()

        gather_wait(slot)
        nxt = jnp.minimum(i + 1, n_t - 1)

        @pl.when((i + 1 < n_t) & (tv_ref[nxt] == 1))
        def _():
            gather_start(i + 1, 1 - slot)

        hb[...] = xbuf[slot].astype(BF16)
        acc[...] = jnp.zeros_like(acc)

    def expert_block(w1_ref, w3_ref, w2_ref):
        h = hb[...]
        hid = _silu(_dot(h, w1_ref[...].astype(BF16))) * _dot(h, w3_ref[...].astype(BF16))
        return _dot(hid.astype(BF16), w2_ref[...].astype(BF16))

    second = 2 * j + 1 < n_f

    @pl.when(valid & second)
    def _():
        acc[...] += expert_block(w1a_ref, w3a_ref, w2a_ref) + expert_block(w1b_ref, w3b_ref, w2b_ref)

    @pl.when(valid & jnp.logical_not(second))
    def _():
        acc[...] += expert_block(w1a_ref, w3a_ref, w2a_ref)

    @pl.when(valid & (j == n_j - 1))
    def _():
        obuf[slot] = acc[...]

        @pl.when(i >= 1)
        def _():
            scatter_wait(1 - slot)

        scatter_start(i, slot)

        @pl.when(i == n_t - 1)
        def _():
            scatter_wait(slot)

    prev = jnp.maximum(i - 1, 0)

    @pl.when(jnp.logical_not(valid) & (j == 0) & (i >= 1) & (tv_ref[prev] == 1))
    def _():
        scatter_wait(1 - slot)


def _moe_grouped(h, plan, w1, w3, w2, *, layer, n_tok):
    te, tv, src, dst = plan
    n_pad, d = h.shape
    assert n_pad - n_tok <= MOE_TILE
    fe = w1.shape[-1]
    n_tiles = te.shape[0]
    n_f = fe // MOE_FT

    def blk(j, tv_i, half):
        return jnp.minimum(2 * j + half, n_f - 1) * tv_i

    def w13(half):
        return pl.BlockSpec((None, None, d, MOE_FT),
                            lambda i, j, te_r, tv_r, s_r, d_r: (layer, te_r[i], 0, blk(j, tv_r[i], half)))

    def w2m(half):
        return pl.BlockSpec((None, None, MOE_FT, d),
                            lambda i, j, te_r, tv_r, s_r, d_r: (layer, te_r[i], blk(j, tv_r[i], half), 0))

    return pl.pallas_call(
        functools.partial(_moe_group_kernel, n_f=n_f, n_tok=n_tok, n_pad=n_pad),
        grid_spec=pltpu.PrefetchScalarGridSpec(
            num_scalar_prefetch=4,
            grid=(n_tiles, (n_f + 1) // 2),
            in_specs=[pl.BlockSpec(memory_space=pl.ANY), w13(0), w13(0), w2m(0), w13(1), w13(1), w2m(1)],
            out_specs=pl.BlockSpec(memory_space=pl.ANY),
            scratch_shapes=[pltpu.VMEM((2, MOE_TILE, d), F32), pltpu.VMEM((MOE_TILE, d), BF16),
                            pltpu.VMEM((MOE_TILE, d), F32), pltpu.VMEM((2, MOE_TILE, d), F32),
                            pltpu.SemaphoreType.DMA((2,)), pltpu.SemaphoreType.DMA((2,))],
        ),
        out_shape=jax.ShapeDtypeStruct((2 * n_pad + MOE_TILE, d), F32),
        compiler_params=_cparams(("arbitrary", "arbitrary")),
        name="moe_grouped_ffn",
    )(te, tv, src, dst, h, w1, w3, w2, w1, w3, w2)


def _moe_combine_kernel(y_ref, gm_ref, route_ref, a_ref, b_ref, o_ref):
    route = route_ref[...]
    o_ref[...] = y_ref[...] + gm_ref[...] * (route[:, 2:3] * a_ref[...] + route[:, 3:4] * b_ref[...])


def _moe_combine(y, gmod, route, out01, *, tm, row0, n_pad):
    n, d = y.shape
    tiles = n // tm
    tiles_per_mod = tiles // gmod.shape[0]
    r = gmod.shape[1]
    first, second = row0 // tm, (n_pad + row0) // tm
    blk = pl.BlockSpec((tm, d), lambda i: (i, 0))
    return pl.pallas_call(
        _moe_combine_kernel,
        grid=(tiles,),
        in_specs=[blk, pl.BlockSpec((None, r, d), lambda i: (i // tiles_per_mod, 0, 0)),
                  pl.BlockSpec((tm, LANES), lambda i: (i + first, 0)),
                  pl.BlockSpec((tm, d), lambda i: (i + first, 0)),
                  pl.BlockSpec((tm, d), lambda i: (i + second, 0))],
        out_specs=blk,
        out_shape=jax.ShapeDtypeStruct((n, d), F32),
        compiler_params=_cparams(("arbitrary",)),
        name="moe_combine",
    )(y, gmod, route, out01, out01)


def _agg_matrix(n_rows, n_cmp, n_sel, width):
    a = np.zeros((n_rows, width), np.float32)
    ratio = SEL_BLOCK // CMP_STRIDE
    for s in range(n_sel):
        for tkn in range(s * ratio - 1, s * ratio + ratio):
            if 0 <= tkn < n_cmp and tkn + 1 < n_rows:
                a[tkn + 1, s] = 1.0
    return jnp.asarray(a, BF16)


def _expand_matrix(t_len):
    e = np.zeros((LANES, t_len), np.float32)
    for s in range(t_len // SEL_BLOCK):
        e[s, s * SEL_BLOCK:(s + 1) * SEL_BLOCK] = 1.0
    return jnp.asarray(e, BF16)


def _compress_weights(l, k_gain, cmp_w1, cmp_b1, cmp_w2, cmp_pe):
    w1 = cmp_w1[l]
    half = CMP_STRIDE * HEAD_DIM
    wa = w1[:, :half].reshape(2, CMP_STRIDE, HEAD_DIM, HEAD_DIM)
    wb = w1[:, half:].reshape(2, CMP_STRIDE, HEAD_DIM, HEAD_DIM)
    z = jnp.zeros_like(wa)
    top = jnp.concatenate([wa, z, wb, z], axis=-1)
    bot = jnp.concatenate([z, wa, z, wb], axis=-1)
    wbd = jnp.concatenate([top, bot], axis=-2).astype(BF16)
    w2 = cmp_w2[l]
    z2 = jnp.zeros_like(w2)
    w2bd = jnp.concatenate([jnp.concatenate([w2, z2], -1), jnp.concatenate([z2, w2], -1)], -2).astype(BF16)
    pe = cmp_pe[l].reshape(2, 2 * half, 1)
    b1 = cmp_b1[l].reshape(2, 1, HEAD_DIM)
    kg0 = jnp.tile(k_gain[l, 0], 2).reshape(1, LANES)
    return wbd, w1, pe, b1, w2bd, kg0


def kernel(x_prompt, x_sample, c_prompt, c_sample, cache_cmp, cache_sel, cache_win, state_conv, page_table, w_ada, b_ada, norm1, norm2, w_in, q_gain, k_gain, cmp_w1, cmp_b1, cmp_w2, cmp_pe, conv_w, conv_b, out_gain, w_out, ffn_w1, ffn_w3, ffn_w2, router_w, moe_w1, moe_w3, moe_w2):
    bp, t_p, d = x_prompt.shape
    bs, t_s, _ = x_sample.shape
    depth = w_in.shape[0]
    n_pool, page = cache_cmp.shape[1], cache_cmp.shape[2]
    n_pages = page_table.shape[1]
    past_len = n_pages * page
    n_p, n_s = bp * t_p, bs * t_s
    win_buf = cache_win.shape[2]
    feat = 2 * KV_DIM
    assert page == LANES and t_p % LANES == 0
    n_cmp_rows = past_len // CMP_STRIDE
    assert past_len % SEL_BLOCK == 0 and t_s < CMP_STRIDE
    n_sel_s = past_len // SEL_BLOCK + 1

    cmp_t = jnp.transpose(cache_cmp, (0, 1, 3, 4, 5, 2)).reshape(depth, n_pool, feat, page)
    sel_t = jnp.transpose(cache_sel, (0, 1, 3, 4, 5, 2)).reshape(depth, n_pool, feat, page)
    win_t = jnp.transpose(cache_win, (0, 1, 3, 4, 5, 2)).reshape(depth, bs, feat, win_buf)

    wt_all = jnp.transpose(w_in, (0, 2, 1)).astype(BF16)
    wc_all = (wt_all[:, CONV_OFF:].reshape(depth, 3, CONV_DIM // CONV_TILE, CONV_TILE, d)
              .transpose(0, 2, 1, 3, 4).reshape(depth, 3 * CONV_DIM, d))
    w_out_b = w_out.astype(BF16)
    ffn_b = (ffn_w1.astype(BF16), ffn_w3.astype(BF16), ffn_w2.astype(BF16))
    moe_w = (moe_w1, moe_w3, moe_w2)

    c_all = jnp.concatenate([c_prompt, c_sample, jnp.zeros((16 - bp - bs, d), F32)], axis=0)
    mod = _modulation(c_all, w_ada, b_ada)

    agg_p = _agg_matrix(t_p // CMP_STRIDE, t_p // CMP_STRIDE - 1, t_p // SEL_BLOCK, LANES)
    exp_p = _expand_matrix(t_p)
    sel_w = -(-n_sel_s // LANES) * LANES
    agg_s = _agg_matrix(n_cmp_rows, n_cmp_rows + 3, n_sel_s, sel_w)
    prompt_pages = jnp.broadcast_to(jnp.arange(t_p // LANES, dtype=jnp.int32), (bp, t_p // LANES))
    n_moe = n_p + n_s
    moe_tiles = (TOP_K * n_moe) // MOE_TILE + N_EXPERTS
    assert n_p % MOE_TILE == 0 and n_p % n_s == 0 and n_s <= MOE_TILE

    yp = x_prompt.reshape(n_p, d)
    ys = x_sample.reshape(n_s, d)
    outs = {k: [] for k in ("cmp_p", "sel_p", "win_p", "conv_p", "cmp_s", "sel_s", "win_s", "conv_s")}
    tm_p = 1024
    n_pick = N_SEL - 1
    for l in range(depth):
        cw = _compress_weights(l, k_gain, cmp_w1, cmp_b1, cmp_w2, cmp_pe)
        qg = jnp.tile(q_gain[l] * ATTN_SCALE, Q_TILE // HEAD_DIM).reshape(1, Q_TILE)
        kg = jnp.broadcast_to(k_gain[l][:, None, :, None], (3, N_KV, HEAD_DIM, 1))
        conv_wl, conv_bl, og = conv_w[l], conv_b[l].reshape(1, CONV_DIM), out_gain[l].reshape(1, d)
        mp = [mod[l, :bp, k * d:(k + 1) * d].reshape(bp, 1, d) for k in range(6)]
        ms = [jnp.repeat(mod[l, bp:bp + bs, k * d:(k + 1) * d], t_s, axis=0).reshape(1, n_s, d) for k in range(6)]
        g1 = norm1[l].reshape(1, d)
        g2 = norm2[l].reshape(1, d)

        qn, kvt, gate, u, bg = _inproj(yp, mp[0], mp[1], g1, wt_all, wc_all, qg, kg, tm=tm_p, n_seq=bp, layer=l)
        kc, vc = _compress(
            kvt, prompt_pages,
            lambda k, p: pl.BlockSpec((None, feat, LANES), lambda b, g, pt: (b, 0, pt[b, g * p + k])),
            *cw)
        gate_hm = gate[:, :3 * N_HEADS].reshape(n_p, N_KV, 3 * GQA).transpose(1, 0, 2)
        attn = _prompt_attention(qn, gate_hm, kc, vc, kvt, agg_p, exp_p, tq=256)
        yp = _outproj(attn, u, bg, conv_wl, conv_bl, og, w_out_b, yp, mp[2], tm=tm_p, n_seq=bp, layer=l)
        kv6 = kvt.reshape(bp, 3, 2, N_KV, HEAD_DIM, t_p)
        outs["cmp_p"].append(kv6[:, 0])
        outs["sel_p"].append(kv6[:, 1])
        outs["win_p"].append(kv6[:, 2, ..., t_p - min(WINDOW, t_p):])
        outs["conv_p"].append(u.reshape(bp, t_p, CONV_DIM)[:, t_p - 2:])

        qn_s, kvt_s, gate_s, u_s, bg_s = _inproj(ys, ms[0], ms[1], g1, wt_all, wc_all, qg, kg, tm=n_s, n_seq=1,
                                                 layer=l)
        layer = l
        kc_s, vc_s = _compress(
            cmp_t, page_table,
            lambda k, p: pl.BlockSpec((None, None, feat, LANES),
                                      lambda b, g, pt: (layer, pt[b, g * p + k], 0, 0)),
            *cw)
        q5 = qn_s.astype(F32).reshape(bs, t_s, N_KV, GQA, HEAD_DIM)
        qs_sel = q5.transpose(0, 2, 3, 1, 4).reshape(bs, N_KV, GQA * t_s, HEAD_DIM)
        oc, picked = _sample_select(qs_sel, kc_s, vc_s, agg_s, q0=past_len, n_tok=t_s, n_sel=n_sel_s)
        blocks = picked[:, :, :n_pick].reshape(bs, N_KV, t_s, n_pick)
        page_of = page_table[jnp.arange(bs)[:, None, None, None], blocks // 2].reshape(-1)
        half_of = (blocks % 2).reshape(-1)
        pad8 = ((0, 0), (0, 0), (0, 0), (0, 8 - GQA), (0, 0))
        q8 = jnp.pad(q5.transpose(0, 2, 1, 3, 4), pad8)
        oc8 = jnp.pad(oc.reshape(bs, N_KV, GQA, t_s, HEAD_DIM).transpose(0, 1, 3, 2, 4), pad8)
        gate8 = jnp.pad(gate_s[:, :3 * N_HEADS].reshape(bs, t_s, N_KV, GQA, 3).transpose(0, 2, 1, 3, 4), pad8)
        attn8 = _sample_attend(page_of, half_of, q8, oc8, gate8, kvt_s, sel_t, win_t, n_tok=t_s, n_pick=n_pick,
                               layer=l)
        attn_s = attn8[:, :, :, :GQA].transpose(0, 2, 1, 3, 4).reshape(n_s, ATTN_DIM)
        st = state_conv[l]
        zero = jnp.zeros((bs, t_s - 1, CONV_DIM), F32)
        e1 = jnp.concatenate([st[:, 1:2], zero], axis=1).reshape(n_s, CONV_DIM)
        e2 = jnp.concatenate([st[:, 0:1], st[:, 1:2], zero[:, 1:]], axis=1).reshape(n_s, CONV_DIM)
        ys = _outproj(attn_s, u_s, bg_s, conv_wl, conv_bl, og, w_out_b, ys, ms[2], tm=n_s, n_seq=1, layer=l,
                      ext=(e1, e2), n_tok=t_s)
        rows_s = kvt_s[0].T.reshape(bs, t_s, 3, 2, N_KV, HEAD_DIM)
        outs["cmp_s"].append(rows_s[:, :, 0])
        outs["sel_s"].append(rows_s[:, :, 1])
        new_win_t = kvt_s[0, 4 * KV_DIM:].reshape(feat, bs, t_s).transpose(1, 0, 2)
        outs["win_s"].append(jnp.concatenate([win_t[l], new_win_t], axis=-1)[..., t_s:])
        u_ext = jnp.concatenate([st, u_s.reshape(bs, t_s, CONV_DIM)], axis=1)
        outs["conv_s"].append(u_ext[:, t_s:])

        if l % 2 == 0:
            yp = _dense_ffn(yp, mp[3], mp[4], mp[5], g2, *ffn_b, tm=512, layer=l // 2)
            ys = _dense_ffn(ys, ms[3], ms[4], ms[5], g2, *ffn_b, tm=n_s, layer=l // 2)
        else:
            rw = jnp.zeros((LANES, d), F32).at[:N_EXPERTS].set(jnp.transpose(router_w[l // 2]))
            h_all, route_all = _router(yp, mp[3], mp[4], g2, rw, tm=MOE_TILE)
            h_all, route_all = _router_into(ys, ms[3], ms[4], g2, rw, h_all, route_all, row0=n_p)
            n_pad = h_all.shape[0]
            plan = _moe_plan(route_all[:n_moe], moe_tiles, n_pad)
            out01 = _moe_grouped(h_all, plan, *moe_w, layer=l // 2, n_tok=n_moe)
            yp = _moe_combine(yp, mp[5], route_all, out01, tm=MOE_TILE, row0=0, n_pad=n_pad)
            ys = _moe_combine(ys, ms[5], route_all, out01, tm=n_s, row0=n_p, n_pad=n_pad)

    def rows_major(x):
        return jnp.transpose(x, (0, 1, 5, 2, 3, 4))

    new_win_s = jnp.stack(outs["win_s"]).reshape(depth, bs, 2, N_KV, HEAD_DIM, win_buf)
    return (yp.reshape(bp, t_p, d), ys.reshape(bs, t_s, d),
            rows_major(jnp.stack(outs["cmp_p"])), rows_major(jnp.stack(outs["sel_p"])),
            rows_major(jnp.stack(outs["win_p"])), jnp.stack(outs["conv_p"]),
            jnp.stack(outs["cmp_s"]), jnp.stack(outs["sel_s"]),
            rows_major(new_win_s), jnp.stack(outs["conv_s"]))
```
